```python
import math
import jax, jax.numpy as jnp
from jax import lax
import numpy as np

D_MODEL = 1024
BATCH = 4
SEQ = 4096
DEPTH = 2

GRID_W = 64
CTX_LEN = 256
N_MIXERS = 4
GROUP_WIDTH = D_MODEL // N_MIXERS
HEAD_DIM = 64
A_HEADS = GROUP_WIDTH // HEAD_DIM
A_KV_HEADS = A_HEADS // 2
WINDOW = 128
BLK = 128
B_HEADS = GROUP_WIDTH // HEAD_DIM
B_QK_DIM = HEAD_DIM // 2
C_HEADS = GROUP_WIDTH // HEAD_DIM
RET_CHUNK = 128
D_HEADS = GROUP_WIDTH // HEAD_DIM
NA_ROWS = 8
NA_COLS = 16
NA_QCB = 16
NA_KCB = 2 * NA_QCB
D_FF = 4 * D_MODEL
ROPE_BASE = 10000.0
EPS = 1e-6
NEG_INF = -1e30
IN_WIDTHS = (A_HEADS * HEAD_DIM, A_KV_HEADS * HEAD_DIM, A_KV_HEADS * HEAD_DIM,
             2 * B_HEADS * B_QK_DIM, 2 * B_HEADS * B_QK_DIM, B_HEADS * HEAD_DIM,
             C_HEADS * HEAD_DIM, C_HEADS * HEAD_DIM, C_HEADS * HEAD_DIM, C_HEADS * HEAD_DIM,
             D_HEADS * HEAD_DIM, D_HEADS * HEAD_DIM, D_HEADS * HEAD_DIM)
IN_WIDTH = sum(IN_WIDTHS)

kernel_name = 'hybrid_parallel_heads_dit_block'


def rms_norm(x, g):
    xf = x.astype(jnp.float32)
    y = xf * lax.rsqrt(jnp.mean(xf * xf, axis=-1, keepdims=True) + EPS)
    return (y * g.astype(jnp.float32)).astype(x.dtype)


def softmax_f32(s):
    return jax.nn.softmax(s.astype(jnp.float32), axis=-1)


def to_heads(t, n, d):
    b, l, _ = t.shape
    return t.reshape(b, l, n, d).transpose(0, 2, 1, 3)


def from_heads(t):
    b, n, l, d = t.shape
    return t.transpose(0, 2, 1, 3).reshape(b, l, n * d)


def to_diff_heads(t):
    b, l, _ = t.shape
    return t.reshape(b, l, B_HEADS, 2, B_QK_DIM).transpose(0, 2, 3, 1, 4)


def split_in(t):
    bounds = np.cumsum(IN_WIDTHS)[:-1].tolist()
    return jnp.split(t, bounds, axis=-1)


def rope_1d(x, pos):
    half = x.shape[-1] // 2
    freqs = ROPE_BASE ** (-jnp.arange(half, dtype=jnp.float32) / half)
    ang = pos[:, None] * freqs[None, :]
    cos, sin = jnp.cos(ang), jnp.sin(ang)
    xf = x.astype(jnp.float32)
    x1, x2 = xf[..., :half], xf[..., half:]
    return jnp.concatenate([x1 * cos - x2 * sin, x1 * sin + x2 * cos], axis=-1).astype(x.dtype)


def rope_2d(x, rows, cols):
    h = x.shape[-1] // 2
    return jnp.concatenate([rope_1d(x[..., :h], rows), rope_1d(x[..., h:], cols)], axis=-1)


def window_attention(q, k, v, k_ctx, v_ctx, sink):
    b, ha, l, dh = q.shape
    g = k.shape[1]
    rep = ha // g
    nb = l // BLK
    scale = dh ** -0.5
    qb = q.reshape(b, g, rep, nb, BLK, dh)

    def band(t):
        tp = jnp.pad(t, ((0, 0), (0, 0), (BLK, BLK), (0, 0))).reshape(b, g, nb + 2, BLK, dh)
        return jnp.concatenate([tp[:, :, :-2], tp[:, :, 1:-1], tp[:, :, 2:]], axis=3)

    kw, vw = band(k), band(v)
    s_win = jnp.einsum('bgrnqd,bgnkd->bgrnqk', qb, kw).astype(jnp.float32) * scale
    blk_id = np.arange(nb)[:, None, None]
    qpos = blk_id * BLK + np.arange(BLK)[None, :, None]
    kpos = (blk_id - 1) * BLK + np.arange(3 * BLK)[None, None, :]
    valid = (np.abs(qpos - kpos) <= WINDOW) & (kpos >= 0) & (kpos < l)
    s_win = jnp.where(valid, s_win, NEG_INF)
    s_ctx = jnp.einsum('bgrnqd,bgkd->bgrnqk', qb, k_ctx).astype(jnp.float32) * scale
    s_sink = jnp.broadcast_to(sink.astype(jnp.float32).reshape(g, rep, 1, 1, 1), s_ctx.shape[:-1] + (1,))
    prob = softmax_f32(jnp.concatenate([s_win, s_ctx, s_sink], axis=-1))
    n_ctx = k_ctx.shape[2]
    p_win = prob[..., :3 * BLK]
    p_ctx = prob[..., 3 * BLK:3 * BLK + n_ctx]
    o = jnp.einsum('bgrnqk,bgnkd->bgrnqd', p_win, vw) + jnp.einsum('bgrnqk,bgkd->bgrnqd', p_ctx, v_ctx)
    return o.reshape(b, ha, l, dh).astype(q.dtype)


def ctx_gqa_attention(q, k, v, sink):
    b, ha, n, dh = q.shape
    g = k.shape[1]
    rep = ha // g
    qg = q.reshape(b, g, rep, n, dh)
    s = jnp.einsum('bgrqd,bgkd->bgrqk', qg, k).astype(jnp.float32) * dh ** -0.5
    s_sink = jnp.broadcast_to(sink.astype(jnp.float32).reshape(g, rep, 1, 1), (b, g, rep, n, 1))
    prob = softmax_f32(jnp.concatenate([s, s_sink], axis=-1))[..., :-1]
    o = jnp.einsum('bgrqk,bgkd->bgrqd', prob, v)
    return o.reshape(b, ha, n, dh).astype(q.dtype)


def diff_probs(q, k, lam):
    s = jnp.einsum('bhjqd,bhjkd->bhjqk', q, k).astype(jnp.float32) * q.shape[-1] ** -0.5
    prob = softmax_f32(s)
    return prob[:, :, 0] - lam * prob[:, :, 1]


def diff_attention(q, k, v, k_ctx, v_ctx, lam):
    b, h, _, l, dk = q.shape
    nb = l // BLK
    k_all = jnp.concatenate([k, k_ctx], axis=3)
    v_all = jnp.concatenate([v, v_ctx], axis=2)
    q_blocks = jnp.moveaxis(q.reshape(b, h, 2, nb, BLK, dk), 3, 0)

    def one_block(qb):
        return jnp.einsum('bhqk,bhkd->bhqd', diff_probs(qb, k_all, lam), v_all)

    o = lax.map(one_block, q_blocks)
    return jnp.moveaxis(o, 0, 2).reshape(b, h, l, v.shape[-1])


def retention_scan(q, k, v, log_gamma, s0):
    b, h, l, dk = q.shape
    dv = v.shape[-1]
    n_chunks = l // RET_CHUNK
    idx = jnp.arange(RET_CHUNK, dtype=jnp.float32)
    lg = log_gamma[:, None]
    rel = idx[:, None] - idx[None, :]
    inner_decay = jnp.exp(jnp.where(rel[None] >= 0, rel[None] * lg[..., None], -jnp.inf))
    q_decay = jnp.exp((idx + 1.0)[None, :] * lg)[..., None]
    k_decay = jnp.exp((RET_CHUNK - 1.0 - idx)[None, :] * lg)[..., None]
    chunk_decay = jnp.exp(RET_CHUNK * lg)[..., None]

    def chunks(t):
        return jnp.moveaxis(t.reshape(b, h, n_chunks, RET_CHUNK, t.shape[-1]), 2, 0)

    def step(state, inp):
        qi, ki, vi = inp
        inner = jnp.einsum('bhid,bhjd->bhij', qi, ki) * inner_decay
        o = jnp.einsum('bhij,bhjv->bhiv', inner, vi) + jnp.einsum('bhid,bhdv->bhiv', qi * q_decay, state)
        state = state * chunk_decay + jnp.einsum('bhjd,bhjv->bhdv', ki * k_decay, vi)
        return state, o

    s_final, o = lax.scan(step, s0, (chunks(q), chunks(k), chunks(v)))
    return jnp.moveaxis(o, 0, 2).reshape(b, h, l, dv), s_final


def neighbourhood_attention(q, k, v, k_ctx, v_ctx, rpb):
    b, h, l, dh = q.shape
    rows = l // GRID_W
    kh = min(NA_ROWS, rows)
    ncb = GRID_W // NA_QCB
    scale = dh ** -0.5
    r = np.arange(rows)
    row_idx = np.clip(r - kh // 2, 0, rows - kh)[:, None] + np.arange(kh)[None, :]
    col_idx = (np.clip(np.arange(ncb) * NA_QCB - NA_COLS // 2, 0, GRID_W - NA_KCB)[:, None]
               + np.arange(NA_KCB)[None, :])
    qcol = np.arange(ncb)[:, None] * NA_QCB + np.arange(NA_QCB)[None, :]
    cstart = np.clip(qcol - NA_COLS // 2, 0, GRID_W - NA_COLS)[..., None]
    kcol = col_idx[:, None, :]
    col_valid = (kcol >= cstart) & (kcol < cstart + NA_COLS)
    dr = row_idx - r[:, None] + NA_ROWS - 1
    dc = np.clip(kcol - qcol[..., None] + NA_COLS - 1, 0, 2 * NA_COLS - 2)
    bias = rpb[:, dr[:, None, None, :, None], dc[None, :, :, None, :]]

    def gather(t):
        tg = t.reshape(b, h, rows, GRID_W, dh)
        return tg[:, :, row_idx[:, None, :, None], col_idx[None, :, None, :]]

    kg, vg = gather(k), gather(v)
    qg = q.reshape(b, h, rows, ncb, NA_QCB, dh)
    s = jnp.einsum('bhrjqd,bhrjkwd->bhrjqkw', qg, kg).astype(jnp.float32) * scale + bias.astype(jnp.float32)
    s = jnp.where(col_valid[:, :, None, :], s, NEG_INF).reshape(b, h, rows, ncb, NA_QCB, kh * NA_KCB)
    s_ctx = jnp.einsum('bhrjqd,bhkd->bhrjqk', qg, k_ctx).astype(jnp.float32) * scale
    prob = softmax_f32(jnp.concatenate([s, s_ctx], axis=-1))
    p_nb = prob[..., :kh * NA_KCB].reshape(b, h, rows, ncb, NA_QCB, kh, NA_KCB)
    p_ctx = prob[..., kh * NA_KCB:]
    o = jnp.einsum('bhrjqkw,bhrjkwd->bhrjqd', p_nb, vg) + jnp.einsum('bhrjqk,bhkd->bhrjqd', p_ctx, v_ctx)
    return o.reshape(b, h, l, dh).astype(q.dtype)


def dense_ctx_attention(q, k, v):
    s = jnp.einsum('bhqd,bhkd->bhqk', q, k).astype(jnp.float32) * q.shape[-1] ** -0.5
    return jnp.einsum('bhqk,bhkd->bhqd', softmax_f32(s), v).astype(q.dtype)


def squared_relu_mlp(h, w1, w2):
    return jnp.square(jax.nn.relu(h @ w1)) @ w2


def hybrid_layer(x, ctx, c, c_ctx, p, layer_idx, need_ctx):
    f32 = jnp.float32
    b, l, _ = x.shape
    pos = jnp.arange(l)
    rows = (pos // GRID_W).astype(f32)
    cols = (pos % GRID_W).astype(f32)

    mod_x = (jax.nn.silu(c) @ p['w_mod'] + p['b_mod'])[:, None, :]
    mod_c = (jax.nn.silu(c_ctx) @ p['w_mod'] + p['b_mod'])[None, None, :]
    sh1x, sc1x, g1x, sh2x, sc2x, g2x = jnp.split(mod_x, 6, axis=-1)
    sh1c, sc1c, g1c, sh2c, sc2c, g2c = jnp.split(mod_c, 6, axis=-1)

    hx = rms_norm(x, p['norm1_g']) * (1 + sc1x) + sh1x
    hc = rms_norm(ctx, p['norm1_g']) * (1 + sc1c) + sh1c
    (lw_q, lw_k, lw_v, lf_q, lf_k, lf_v, lr_q, lr_k, lr_v, lr_g, ln_q, ln_k, ln_v) = split_in(hx @ p['w_in'])
    (cw_q, cw_k, cw_v, cf_q, cf_k, cf_v, cr_q, cr_k, cr_v, cr_g, cn_q, cn_k, cn_v) = split_in(hc @ p['w_in'])

    q = rope_2d(rms_norm(to_heads(lw_q, A_HEADS, HEAD_DIM), p['a_qnorm_g']), rows, cols)
    k = rope_2d(rms_norm(to_heads(lw_k, A_KV_HEADS, HEAD_DIM), p['a_knorm_g']), rows, cols)
    v = to_heads(lw_v, A_KV_HEADS, HEAD_DIM)
    k_c = rms_norm(to_heads(cw_k, A_KV_HEADS, HEAD_DIM), p['a_knorm_g'])
    v_c = to_heads(cw_v, A_KV_HEADS, HEAD_DIM)
    out_w = window_attention(q, k, v, k_c, v_c, p['a_sink'])
    if need_ctx:
        q_c = rms_norm(to_heads(cw_q, A_HEADS, HEAD_DIM), p['a_qnorm_g'])
        cout_w = ctx_gqa_attention(q_c, k_c, v_c, p['a_sink'])

    lambda_init = 0.8 - 0.6 * math.exp(-0.3 * layer_idx)
    lam = (jnp.exp(jnp.sum(p['b_lambda_q1'].astype(f32) * p['b_lambda_k1'].astype(f32)))
           - jnp.exp(jnp.sum(p['b_lambda_q2'].astype(f32) * p['b_lambda_k2'].astype(f32))) + lambda_init)
    q = rope_2d(rms_norm(to_diff_heads(lf_q), p['b_qnorm_g']), rows, cols)
    k = rope_2d(rms_norm(to_diff_heads(lf_k), p['b_knorm_g']), rows, cols)
    v = to_heads(lf_v, B_HEADS, HEAD_DIM)
    k_c = rms_norm(to_diff_heads(cf_k), p['b_knorm_g'])
    v_c = to_heads(cf_v, B_HEADS, HEAD_DIM)
    out_f = (rms_norm(diff_attention(q, k, v, k_c, v_c, lam), p['b_subln_g']) * (1 - lambda_init)).astype(x.dtype)
    if need_ctx:
        q_c = rms_norm(to_diff_heads(cf_q), p['b_qnorm_g'])
        o_c = jnp.einsum('bhqk,bhkd->bhqd', diff_probs(q_c, k_c, lam), v_c)
        cout_f = (rms_norm(o_c, p['b_subln_g']) * (1 - lambda_init)).astype(x.dtype)

    lg_f = jax.nn.log_sigmoid(p['c_decay_fwd'].astype(f32))
    lg_b = jax.nn.log_sigmoid(p['c_decay_bwd'].astype(f32))
    k_scale = HEAD_DIM ** -0.5
    flip = lambda t: jnp.flip(t, axis=2)
    q = to_heads(lr_q, C_HEADS, HEAD_DIM)
    k = to_heads(lr_k, C_HEADS, HEAD_DIM) * k_scale
    v = to_heads(lr_v, C_HEADS, HEAD_DIM)
    q_c = to_heads(cr_q, C_HEADS, HEAD_DIM)
    k_c = to_heads(cr_k, C_HEADS, HEAD_DIM) * k_scale
    v_c = to_heads(cr_v, C_HEADS, HEAD_DIM)
    s0 = jnp.zeros((b, C_HEADS, HEAD_DIM, HEAD_DIM), f32)
    o_cf, s_f = retention_scan(q_c, k_c, v_c, lg_f, s0)
    o_cb, s_b = retention_scan(flip(q_c), flip(k_c), flip(v_c), lg_b, s0)
    o_f, _ = retention_scan(q, k, v, lg_f, s_f)
    o_b, _ = retention_scan(flip(q), flip(k), flip(v), lg_b, s_b)
    gate = jax.nn.silu(to_heads(lr_g, C_HEADS, HEAD_DIM))
    out_r = (rms_norm(o_f + flip(o_b), p['c_gn_g']) * gate).astype(x.dtype)
    if need_ctx:
        gate_c = jax.nn.silu(to_heads(cr_g, C_HEADS, HEAD_DIM))
        cout_r = (rms_norm(o_cf + flip(o_cb), p['c_gn_g']) * gate_c).astype(x.dtype)

    q = rms_norm(to_heads(ln_q, D_HEADS, HEAD_DIM), p['d_qnorm_g'])
    k = rms_norm(to_heads(ln_k, D_HEADS, HEAD_DIM), p['d_knorm_g'])
    v = to_heads(ln_v, D_HEADS, HEAD_DIM)
    k_c = rms_norm(to_heads(cn_k, D_HEADS, HEAD_DIM), p['d_knorm_g'])
    v_c = to_heads(cn_v, D_HEADS, HEAD_DIM)
    out_n = neighbourhood_attention(q, k, v, k_c, v_c, p['d_rpb'])
    if need_ctx:
        q_c = rms_norm(to_heads(cn_q, D_HEADS, HEAD_DIM), p['d_qnorm_g'])
        cout_n = dense_ctx_attention(q_c, k_c, v_c)

    mix_x = jnp.concatenate([from_heads(out_w), from_heads(out_f), from_heads(out_r), from_heads(out_n)], axis=-1) @ p['w_out']
    x = x + (g1x * mix_x).astype(x.dtype)
    h2 = rms_norm(x, p['norm2_g']) * (1 + sc2x) + sh2x
    x = x + (g2x * squared_relu_mlp(h2, p['w_mlp1'], p['w_mlp2'])).astype(x.dtype)

    if need_ctx:
        mix_c = jnp.concatenate([from_heads(cout_w), from_heads(cout_f), from_heads(cout_r), from_heads(cout_n)], axis=-1) @ p['w_out']
        ctx = ctx + (g1c * mix_c).astype(ctx.dtype)
        h2c = rms_norm(ctx, p['norm2_g']) * (1 + sc2c) + sh2c
        ctx = ctx + (g2c * squared_relu_mlp(h2c, p['w_mlp1'], p['w_mlp2'])).astype(ctx.dtype)
    return x, ctx


def setup_inputs(seed: int = 0) -> dict:
    key = jax.random.key(seed)
    ks = iter(jax.random.split(key, 32))
    f32 = jnp.float32

    def nrm(shape, scale):
        return jax.random.normal(next(ks), shape, f32) * scale

    def gain(shape):
        return 1.0 + 0.05 * jax.random.normal(next(ks), shape, f32)

    decay_base = jnp.log(2.0 ** (5.0 + jnp.arange(C_HEADS, dtype=f32)) - 1.0)
    return {
        'x': nrm((BATCH, SEQ, D_MODEL), 1.0),
        'c': nrm((BATCH, D_MODEL), 1.0),
        'ctx': nrm((BATCH, CTX_LEN, D_MODEL), 1.0),
        'c_ctx': nrm((D_MODEL,), 1.0),
        'w_mod': nrm((DEPTH, D_MODEL, 6 * D_MODEL), 0.5 * D_MODEL ** -0.5),
        'b_mod': nrm((DEPTH, 6 * D_MODEL), 0.02),
        'norm1_g': gain((DEPTH, D_MODEL)),
        'norm2_g': gain((DEPTH, D_MODEL)),
        'w_in': nrm((DEPTH, D_MODEL, IN_WIDTH), D_MODEL ** -0.5),
        'w_out': nrm((DEPTH, N_MIXERS * GROUP_WIDTH, D_MODEL), (N_MIXERS * GROUP_WIDTH) ** -0.5),
        'a_qnorm_g': gain((DEPTH, HEAD_DIM)),
        'a_knorm_g': gain((DEPTH, HEAD_DIM)),
        'a_sink': nrm((DEPTH, A_HEADS), 0.5),
        'b_qnorm_g': gain((DEPTH, B_QK_DIM)),
        'b_knorm_g': gain((DEPTH, B_QK_DIM)),
        'b_lambda_q1': nrm((DEPTH, B_QK_DIM), 0.1),
        'b_lambda_k1': nrm((DEPTH, B_QK_DIM), 0.1),
        'b_lambda_q2': nrm((DEPTH, B_QK_DIM), 0.1),
        'b_lambda_k2': nrm((DEPTH, B_QK_DIM), 0.1),
        'b_subln_g': gain((DEPTH, HEAD_DIM)),
        'c_decay_fwd': decay_base[None, :] + nrm((DEPTH, C_HEADS), 0.1),
        'c_decay_bwd': decay_base[None, :] + nrm((DEPTH, C_HEADS), 0.1),
        'c_gn_g': gain((DEPTH, HEAD_DIM)),
        'd_qnorm_g': gain((DEPTH, HEAD_DIM)),
        'd_knorm_g': gain((DEPTH, HEAD_DIM)),
        'd_rpb': nrm((DEPTH, D_HEADS, 2 * NA_ROWS - 1, 2 * NA_COLS - 1), 0.1),
        'w_mlp1': nrm((DEPTH, D_MODEL, D_FF), D_MODEL ** -0.5),
        'w_mlp2': nrm((DEPTH, D_FF, D_MODEL), D_FF ** -0.5),
    }


def reference(x, c, ctx, c_ctx, w_mod, b_mod, norm1_g, norm2_g, w_in, w_out,
              a_qnorm_g, a_knorm_g, a_sink, b_qnorm_g, b_knorm_g,
              b_lambda_q1, b_lambda_k1, b_lambda_q2, b_lambda_k2, b_subln_g,
              c_decay_fwd, c_decay_bwd, c_gn_g, d_qnorm_g, d_knorm_g, d_rpb,
              w_mlp1, w_mlp2):
    for layer in range(DEPTH):
        p = {
            'w_mod': w_mod[layer], 'b_mod': b_mod[layer],
            'norm1_g': norm1_g[layer], 'norm2_g': norm2_g[layer],
            'w_in': w_in[layer], 'w_out': w_out[layer],
            'a_qnorm_g': a_qnorm_g[layer], 'a_knorm_g': a_knorm_g[layer], 'a_sink': a_sink[layer],
            'b_qnorm_g': b_qnorm_g[layer], 'b_knorm_g': b_knorm_g[layer],
            'b_lambda_q1': b_lambda_q1[layer], 'b_lambda_k1': b_lambda_k1[layer],
            'b_lambda_q2': b_lambda_q2[layer], 'b_lambda_k2': b_lambda_k2[layer],
            'b_subln_g': b_subln_g[layer],
            'c_decay_fwd': c_decay_fwd[layer], 'c_decay_bwd': c_decay_bwd[layer], 'c_gn_g': c_gn_g[layer],
            'd_qnorm_g': d_qnorm_g[layer], 'd_knorm_g': d_knorm_g[layer], 'd_rpb': d_rpb[layer],
            'w_mlp1': w_mlp1[layer], 'w_mlp2': w_mlp2[layer],
        }
        x, ctx = hybrid_layer(x, ctx, c, c_ctx, p, layer, layer < DEPTH - 1)
    return x
```

```python
import functools
import math

import numpy as np
import jax
import jax.numpy as jnp
from jax import lax
from jax.experimental import pallas as pl
from jax.experimental.pallas import tpu as pltpu

F32 = jnp.float32
BF16 = jnp.bfloat16

D_MODEL = 1024
SEQ = 4096
CTX_LEN = 256
T_ALL = SEQ + CTX_LEN
GRID_W = 64
HEAD_DIM = 64
B_QK_DIM = 32
BLK = 128
N_LAT_BLK = SEQ // BLK
N_ALL_BLK = T_ALL // BLK
NA_ROWS = 8
NA_COLS = 16
D_FF = 4 * D_MODEL
IN_WIDTH = 3072
ROPE_BASE = 10000.0
EPS = 1e-6
NEG_INF = -1e30
LANES = 128
VMEM_LIMIT = 52 * 1024 * 1024

G_AQ, G_AK, G_AV = 0, 2, 3
G_BQ, G_BK, G_BV = 4, 6, 8
G_CQ, G_CK, G_CV, G_CG = 10, 12, 14, 16
G_DQ, G_DK, G_DV = 18, 20, 22
N_GROUPS = IN_WIDTH // LANES

_NT = (((1,), (1,)), ((), ()))
_TN = (((0,), (0,)), ((), ()))


def _params(sem):
    return pltpu.CompilerParams(dimension_semantics=sem, vmem_limit_bytes=VMEM_LIMIT)


def _rms(x):
    return x * lax.rsqrt(jnp.mean(x * x, axis=-1, keepdims=True) + EPS)


def _mod_kernel(c_ref, w_ref, b_ref, o_ref):
    c = c_ref[...]
    s = c / (1.0 + jnp.exp(-c))
    s_hi = s.astype(BF16)
    s_lo = (s - s_hi.astype(F32)).astype(BF16)
    w = w_ref[0]
    w_hi = w.astype(BF16)
    w_lo = (w - w_hi.astype(F32)).astype(BF16)
    acc = jnp.dot(s_hi, w_hi, preferred_element_type=F32)
    acc += jnp.dot(s_lo, w_hi, preferred_element_type=F32)
    acc += jnp.dot(s_hi, w_lo, preferred_element_type=F32)
    o_ref[0] = acc + b_ref[0]


def _modulation(cc, w_mod, b_mod):
    depth = w_mod.shape[0]
    n_col = 6 * D_MODEL // 1024
    return pl.pallas_call(
        _mod_kernel,
        out_shape=jax.ShapeDtypeStruct((depth, 8, 6 * D_MODEL), F32),
        grid=(depth, n_col),
        in_specs=[
            pl.BlockSpec((8, D_MODEL), lambda l, j: (0, 0)),
            pl.BlockSpec((1, D_MODEL, 1024), lambda l, j: (l, 0, j)),
            pl.BlockSpec((1, 1, 1024), lambda l, j: (l, 0, j)),
        ],
        out_specs=pl.BlockSpec((1, 8, 1024), lambda l, j: (l, 0, j)),
        compiler_params=_params(("parallel", "parallel")),
        name="adaln_mod",
    )(cc, w_mod, b_mod.reshape(depth, 1, 6 * D_MODEL))


def _seg_inv_rms(v, seg):
    sq = v * v
    hi = sq.astype(BF16)
    lo = (sq - hi.astype(F32)).astype(BF16)
    r = lax.broadcasted_iota(jnp.int32, (LANES, LANES), 0) // seg
    c = lax.broadcasted_iota(jnp.int32, (LANES, LANES), 1) // seg
    bd = jnp.where(r == c, 1.0, 0.0).astype(BF16)
    ssum = jnp.dot(hi, bd, preferred_element_type=F32) + jnp.dot(lo, bd, preferred_element_type=F32)
    return lax.rsqrt(ssum * (1.0 / seg) + EPS)


def _rotate(v, cos, sin_signed, half):
    up = pltpu.roll(v, LANES - half, 1)
    dn = pltpu.roll(v, half, 1)
    lane = lax.broadcasted_iota(jnp.int32, v.shape, 1)
    partner = jnp.where(lane % (2 * half) < half, up, dn)
    return v * cos + partner * sin_signed


def _inproj_kernel(x_ref, mod_ref, g1_ref, w_ref, gain_ref, ca_ref, sa_ref, cb_ref, sb_ref,
                   qkv_ref, gate_ref):
    x = x_ref[0]
    sh = mod_ref[0, 0:1, :]
    sc = mod_ref[0, 1:2, :]
    h = (_rms(x) * g1_ref[...] * (1.0 + sc) + sh).astype(BF16)
    acc = jnp.dot(h, w_ref[...], preferred_element_type=F32)
    for g in range(N_GROUPS):
        cols = slice(g * LANES, (g + 1) * LANES)
        v = acc[:, cols]
        gain = gain_ref[:, cols]
        if g in (G_AQ, G_AQ + 1, G_AK):
            v = v * _seg_inv_rms(v, HEAD_DIM) * gain
            v = _rotate(v, ca_ref[...], sa_ref[...], HEAD_DIM // 4)
        elif g in (G_BQ, G_BQ + 1, G_BK, G_BK + 1):
            v = v * _seg_inv_rms(v, B_QK_DIM) * gain
            v = _rotate(v, cb_ref[...], sb_ref[...], B_QK_DIM // 4)
        elif g in (G_DQ, G_DQ + 1, G_DK, G_DK + 1):
            v = v * _seg_inv_rms(v, HEAD_DIM) * gain
        elif g in (G_CK, G_CK + 1):
            v = v * gain
        elif g in (G_CG, G_CG + 1):
            gcols = slice((g - G_CG) * LANES, (g - G_CG + 1) * LANES)
            gate_ref[0, :, gcols] = v / (1.0 + jnp.exp(-v))
        qkv_ref[0, :, cols] = v.astype(BF16)


def _in_proj(xs, mod, g1, w_in, gain, tabs, tm=256):
    b = xs.shape[0]
    nt = T_ALL // tm
    ctx_row = mod.shape[0] - 1

    def mod_map(t, i):
        return (jnp.where(t == nt - 1, ctx_row, i), 0, 0)

    tab_spec = pl.BlockSpec((tm, LANES), lambda t, i: (t, 0))
    return pl.pallas_call(
        _inproj_kernel,
        out_shape=(jax.ShapeDtypeStruct((b, T_ALL, IN_WIDTH), BF16),
                   jax.ShapeDtypeStruct((b, T_ALL, 2 * LANES), F32)),
        grid=(nt, b),
        in_specs=[
            pl.BlockSpec((1, tm, D_MODEL), lambda t, i: (i, t, 0)),
            pl.BlockSpec((1, 6, D_MODEL), mod_map),
            pl.BlockSpec((1, D_MODEL), lambda t, i: (0, 0)),
            pl.BlockSpec((D_MODEL, IN_WIDTH), lambda t, i: (0, 0)),
            pl.BlockSpec((1, IN_WIDTH), lambda t, i: (0, 0)),
            tab_spec, tab_spec, tab_spec, tab_spec,
        ],
        out_specs=(pl.BlockSpec((1, tm, IN_WIDTH), lambda t, i: (i, t, 0)),
                   pl.BlockSpec((1, tm, 2 * LANES), lambda t, i: (i, t, 0))),
        compiler_params=_params(("parallel", "parallel")),
        name="in_proj",
    )(xs, mod, g1, w_in, gain, *tabs)


def _softmax_sink_pv(s, sink_col, v):
    m = jnp.maximum(jnp.max(s, axis=1, keepdims=True), sink_col)
    e = jnp.exp(s - m)
    den = jnp.sum(e, axis=1, keepdims=True) + jnp.exp(sink_col - m)
    o = jnp.dot(e.astype(BF16), v, preferred_element_type=F32)
    return o / den


def _a_kernel(sink_ref, q_ref, kp_ref, kc_ref, kn_ref, vp_ref, vc_ref, vn_ref, kx_ref, vx_ref,
              o_ref, *, ctx_queries):
    j = pl.program_id(1)
    nq = 2 * BLK
    row = lax.broadcasted_iota(jnp.int32, (nq, 1), 0)

    def sink_col(p):
        return jnp.where(row < BLK, sink_ref[2 * p], sink_ref[2 * p + 1])

    def stacked_q(p):
        qp = q_ref[0, :, p * LANES:(p + 1) * LANES]
        return jnp.concatenate([qp[:, :HEAD_DIM], qp[:, HEAD_DIM:]], axis=0)

    def store(p, o):
        lo = p * HEAD_DIM
        o = o[:, lo:lo + HEAD_DIM].astype(BF16)
        o_ref[0, :, p * LANES:p * LANES + HEAD_DIM] = o[:BLK]
        o_ref[0, :, p * LANES + HEAD_DIM:(p + 1) * LANES] = o[BLK:]

    def latent():
        k_all = jnp.concatenate([kx_ref[0], kp_ref[0], kc_ref[0], kn_ref[0]], axis=0)
        v_all = jnp.concatenate([vx_ref[0], vp_ref[0], vc_ref[0], vn_ref[0]], axis=0)
        nk = CTX_LEN + 3 * BLK
        qi = lax.broadcasted_iota(jnp.int32, (nq, nk), 0) % BLK
        col = lax.broadcasted_iota(jnp.int32, (nq, nk), 1)
        c = col - CTX_LEN
        c_lo = jnp.where(j > 0, 0, BLK)
        c_hi = jnp.where(j < N_LAT_BLK - 1, 3 * BLK, 2 * BLK)
        valid = (col < CTX_LEN) | ((jnp.abs(c - BLK - qi) <= BLK) & (c >= c_lo) & (c < c_hi))
        for p in range(2):
            kg = k_all[:, p * HEAD_DIM:(p + 1) * HEAD_DIM]
            s = lax.dot_general(stacked_q(p), kg, _NT, preferred_element_type=F32)
            s = jnp.where(valid, s, NEG_INF)
            store(p, _softmax_sink_pv(s, sink_col(p), v_all))

    def context():
        for p in range(2):
            kg = kx_ref[0, :, p * HEAD_DIM:(p + 1) * HEAD_DIM]
            s = lax.dot_general(stacked_q(p), kg, _NT, preferred_element_type=F32)
            store(p, _softmax_sink_pv(s, sink_col(p), vx_ref[0]))

    if ctx_queries:
        pl.when(j < N_LAT_BLK)(latent)
        pl.when(j >= N_LAT_BLK)(context)
    else:
        latent()


def _attn_a(qkv, sink, ctx_queries):
    b = qkv.shape[0]
    nqb = N_ALL_BLK if ctx_queries else N_LAT_BLK
    last = N_LAT_BLK - 1

    def blk(colblk, fn):
        return pl.BlockSpec((1, BLK, LANES), lambda i, j: (i, fn(j), colblk))

    prev = lambda j: jnp.clip(j - 1, 0, last)
    cur = lambda j: jnp.minimum(j, last)
    nxt = lambda j: jnp.clip(j + 1, 0, last)
    ctx_spec = lambda colblk: pl.BlockSpec((1, CTX_LEN, LANES), lambda i, j: (i, SEQ // CTX_LEN, colblk))
    return pl.pallas_call(
        functools.partial(_a_kernel, ctx_queries=ctx_queries),
        out_shape=jax.ShapeDtypeStruct((b, nqb * BLK, 2 * LANES), BF16),
        grid=(b, nqb),
        in_specs=[
            pl.BlockSpec(memory_space=pltpu.SMEM),
            pl.BlockSpec((1, BLK, 2 * LANES), lambda i, j: (i, j, 0)),
            blk(G_AK, prev), blk(G_AK, cur), blk(G_AK, nxt),
            blk(G_AV, prev), blk(G_AV, cur), blk(G_AV, nxt),
            ctx_spec(G_AK), ctx_spec(G_AV),
        ],
        out_specs=pl.BlockSpec((1, BLK, 2 * LANES), lambda i, j: (i, j, 0)),
        compiler_params=_params(("parallel", "parallel")),
        name="attn_window",
    )(sink, qkv, qkv, qkv, qkv, qkv, qkv, qkv, qkv, qkv)


def _b_kernel(lam_ref, subg_ref, q_ref, kt_ref, v_ref, o_ref, *, ctx_queries, lambda_init):
    j = pl.program_id(2)
    lv = lam_ref[...]
    lam = (jnp.exp(jnp.sum(lv[0:1] * lv[1:2], axis=1, keepdims=True))
           - jnp.exp(jnp.sum(lv[2:3] * lv[3:4], axis=1, keepdims=True)) + lambda_init)
    q = q_ref[0]
    lane = lax.broadcasted_iota(jnp.int32, (1, LANES), 1)

    def run(k0, nk):
        outs = []
        for hh in range(2):
            probs = []
            for sub in range(2):
                r0 = hh * HEAD_DIM + sub * B_QK_DIM
                qs = jnp.where((lane >= r0) & (lane < r0 + B_QK_DIM), q, jnp.zeros_like(q))
                s = jnp.dot(qs, kt_ref[0, :, k0:k0 + nk], preferred_element_type=F32)
                e = jnp.exp(s - jnp.max(s, axis=1, keepdims=True))
                probs.append(e * (1.0 / jnp.sum(e, axis=1, keepdims=True)))
            p = (probs[0] - lam * probs[1]).astype(BF16)
            o = jnp.dot(p, v_ref[0, k0:k0 + nk, :], preferred_element_type=F32)
            o = o[:, hh * HEAD_DIM:(hh + 1) * HEAD_DIM]
            outs.append(_rms(o) * subg_ref[...] * (1.0 - lambda_init))
        o_ref[0] = jnp.concatenate(outs, axis=1).astype(BF16)

    if ctx_queries:
        pl.when(j < N_LAT_BLK)(lambda: run(0, T_ALL))
        pl.when(j >= N_LAT_BLK)(lambda: run(SEQ, CTX_LEN))
    else:
        run(0, T_ALL)


def _attn_b(qkv, kt, lam_vecs, subg, ctx_queries, lambda_init):
    b = qkv.shape[0]
    nqb = N_ALL_BLK if ctx_queries else N_LAT_BLK
    return pl.pallas_call(
        functools.partial(_b_kernel, ctx_queries=ctx_queries, lambda_init=lambda_init),
        out_shape=jax.ShapeDtypeStruct((b, nqb * BLK, 2 * LANES), BF16),
        grid=(b, 2, nqb),
        in_specs=[
            pl.BlockSpec((4, B_QK_DIM), lambda i, p, j: (0, 0)),
            pl.BlockSpec((1, HEAD_DIM), lambda i, p, j: (0, 0)),
            pl.BlockSpec((1, BLK, LANES), lambda i, p, j: (i, j, G_BQ + p)),
            pl.BlockSpec((1, LANES, T_ALL), lambda i, p, j: (i, p, 0)),
            pl.BlockSpec((1, T_ALL, LANES), lambda i, p, j: (i, 0, G_BV + p)),
        ],
        out_specs=pl.BlockSpec((1, BLK, LANES), lambda i, p, j: (i, j, p)),
        compiler_params=_params(("parallel", "parallel", "parallel")),
        name="attn_diff",
    )(lam_vecs, subg, qkv, kt, qkv)


def _c_kernel(dec_ref, gn_ref, q_ref, k_ref, v_ref, g_ref, o_ref, fwd_ref, *, ctx_out):
    ii = lax.broadcasted_iota(jnp.int32, (BLK, BLK), 0)
    jj = lax.broadcasted_iota(jnp.int32, (BLK, BLK), 1)
    pos = lax.broadcasted_iota(jnp.int32, (BLK, 1), 0).astype(F32)
    rel = (ii - jj).astype(F32)
    n_ctx = CTX_LEN // BLK

    for hh in range(2):
        lanes = slice(hh * HEAD_DIM, (hh + 1) * HEAD_DIM)

        def log_gamma(col):
            x = dec_ref[0, :, col:col + 1]
            return jnp.minimum(x, 0.0) - jnp.log1p(jnp.exp(-jnp.abs(x)))

        def chunk_out(r0, state, inner_decay, q_decay, k_decay, chunk_decay):
            q = q_ref[0, pl.ds(r0, BLK), lanes]
            k = k_ref[0, pl.ds(r0, BLK), lanes]
            v = v_ref[0, pl.ds(r0, BLK), lanes]
            inner = lax.dot_general(q, k, _NT, preferred_element_type=F32) * inner_decay
            o = jnp.dot(inner.astype(BF16), v, preferred_element_type=F32)
            o += jnp.dot((q.astype(F32) * q_decay).astype(BF16), state.astype(BF16),
                         preferred_element_type=F32)
            kv = lax.dot_general((k.astype(F32) * k_decay).astype(BF16), v, _TN,
                                 preferred_element_type=F32)
            return o, state * chunk_decay + kv

        lg = log_gamma(hh)
        f_decays = (jnp.where(rel >= 0, jnp.exp(rel * lg), 0.0), jnp.exp((pos + 1.0) * lg),
                    jnp.exp((BLK - 1.0 - pos) * lg), jnp.exp(BLK * lg))

        def fwd_step(i, state):
            c = jnp.where(i < n_ctx, N_LAT_BLK + i, i - n_ctx)
            r0 = pl.multiple_of(c * BLK, BLK)
            o, state = chunk_out(r0, state, *f_decays)
            fwd_ref[pl.ds(r0, BLK), lanes] = o
            return state

        lax.fori_loop(0, N_ALL_BLK, fwd_step, jnp.zeros((HEAD_DIM, HEAD_DIM), F32))

        lg = log_gamma(2 + hh)
        b_decays = (jnp.where(rel <= 0, jnp.exp(-rel * lg), 0.0), jnp.exp((BLK - pos) * lg),
                    jnp.exp(pos * lg), jnp.exp(BLK * lg))

        def bwd_step(i, state, write):
            c = N_ALL_BLK - 1 - i
            r0 = pl.multiple_of(c * BLK, BLK)
            o, state = chunk_out(r0, state, *b_decays)
            if write:
                tot = fwd_ref[pl.ds(r0, BLK), lanes] + o
                y = _rms(tot) * gn_ref[...] * g_ref[0, pl.ds(r0, BLK), lanes]
                o_ref[0, pl.ds(r0, BLK), lanes] = y.astype(BF16)
            return state

        state = lax.fori_loop(0, n_ctx, functools.partial(bwd_step, write=ctx_out),
                              jnp.zeros((HEAD_DIM, HEAD_DIM), F32))
        lax.fori_loop(n_ctx, N_ALL_BLK, functools.partial(bwd_step, write=True), state)


def _retention(qkv, gate, dec, gn, ctx_out):
    b = qkv.shape[0]
    t_out = T_ALL if ctx_out else SEQ
    col = lambda g: pl.BlockSpec((1, T_ALL, LANES), lambda i, p: (i, 0, g + p))
    return pl.pallas_call(
        functools.partial(_c_kernel, ctx_out=ctx_out),
        out_shape=jax.ShapeDtypeStruct((b, t_out, 2 * LANES), BF16),
        grid=(b, 2),
        in_specs=[
            pl.BlockSpec((1, 1, 4), lambda i, p: (p, 0, 0)),
            pl.BlockSpec((1, HEAD_DIM), lambda i, p: (0, 0)),
            col(G_CQ), col(G_CK), col(G_CV),
            pl.BlockSpec((1, T_ALL, LANES), lambda i, p: (i, 0, p)),
        ],
        out_specs=pl.BlockSpec((1, t_out, LANES), lambda i, p: (i, 0, p)),
        scratch_shapes=[pltpu.VMEM((T_ALL, LANES), F32)],
        compiler_params=_params(("parallel", "parallel")),
        name="retention",
    )(dec, gn, qkv, qkv, qkv, gate)


def _d_kernel(q_ref, k_ref, v_ref, bias_ref, o_ref, *, ctx_queries):
    rows = SEQ // GRID_W
    slab = NA_ROWS * GRID_W
    kx = k_ref[0, SEQ:T_ALL, :]
    vx = v_ref[0, SEQ:T_ALL, :]
    head0 = lax.broadcasted_iota(jnp.int32, (1, LANES), 1) < HEAD_DIM

    def attend(q, score_parts):
        out = None
        for hh in range(2):
            mine = head0 if hh == 0 else jnp.logical_not(head0)
            qh = jnp.where(mine, q, jnp.zeros_like(q))
            parts = []
            for k, v, bias in score_parts:
                s = lax.dot_general(qh, k, _NT, preferred_element_type=F32)
                parts.append((s if bias is None else s + bias(hh), v))
            m = functools.reduce(jnp.maximum, [jnp.max(s, axis=1, keepdims=True) for s, _ in parts])
            es = [(jnp.exp(s - m), v) for s, v in parts]
            den = sum(jnp.sum(e, axis=1, keepdims=True) for e, _ in es)
            o = sum(jnp.dot(e.astype(BF16), v, preferred_element_type=F32) for e, v in es) / den
            out = o if out is None else jnp.where(mine, o, out)
        return out.astype(BF16)

    def row_step(r, carry):
        start = pl.multiple_of(jnp.clip(r - NA_ROWS // 2, 0, rows - NA_ROWS) * GRID_W, GRID_W)
        dr0 = jnp.where(r <= NA_ROWS // 2, NA_ROWS - 1 - r,
                        jnp.where(r >= rows - NA_ROWS // 2, rows - 1 - r, NA_ROWS // 2 - 1))
        q0 = pl.multiple_of(r * GRID_W, GRID_W)
        q = q_ref[0, pl.ds(q0, GRID_W), :]
        ks = k_ref[0, pl.ds(start, slab), :]
        vs = v_ref[0, pl.ds(start, slab), :]
        o_ref[0, pl.ds(q0, GRID_W), :] = attend(
            q, [(ks, vs, lambda hh: bias_ref[hh, dr0]), (kx, vx, None)])
        return carry

    lax.fori_loop(0, rows, row_step, 0)
    if ctx_queries:
        o_ref[0, SEQ:T_ALL, :] = attend(q_ref[0, SEQ:T_ALL, :], [(kx, vx, None)])


def _attn_d(qkv, bias, ctx_queries):
    b = qkv.shape[0]
    t_out = T_ALL if ctx_queries else SEQ
    col = lambda g: pl.BlockSpec((1, T_ALL, LANES), lambda i, p: (i, 0, g + p))
    return pl.pallas_call(
        functools.partial(_d_kernel, ctx_queries=ctx_queries),
        out_shape=jax.ShapeDtypeStruct((b, t_out, 2 * LANES), BF16),
        grid=(b, 2),
        in_specs=[
            col(G_DQ), col(G_DK), col(G_DV),
            pl.BlockSpec((2, NA_ROWS, GRID_W, NA_ROWS * GRID_W), lambda i, p: (p, 0, 0, 0)),
        ],
        out_specs=pl.BlockSpec((1, t_out, LANES), lambda i, p: (i, 0, p)),
        compiler_params=_params(("parallel", "parallel")),
        name="attn_nbr",
    )(qkv, qkv, qkv, bias)


def _outmlp_kernel(x_ref, a_ref, b_ref, c_ref, d_ref, mod_ref, g2_ref, wo_ref, w1_ref, w2_ref, o_ref):
    gw = 2 * LANES
    mix = None
    for n, r in enumerate((a_ref, b_ref, c_ref, d_ref)):
        part = jnp.dot(r[0], wo_ref[n * gw:(n + 1) * gw, :], preferred_element_type=F32)
        mix = part if mix is None else mix + part
    x1 = x_ref[0] + mod_ref[0, 2:3, :] * mix
    h2 = (_rms(x1) * g2_ref[...] * (1.0 + mod_ref[0, 4:5, :]) + mod_ref[0, 3:4, :]).astype(BF16)
    acc = None
    for n in range(D_FF // D_MODEL):
        cols = slice(n * D_MODEL, (n + 1) * D_MODEL)
        hid = jnp.maximum(jnp.dot(h2, w1_ref[:, cols], preferred_element_type=F32), 0.0)
        part = jnp.dot((hid * hid).astype(BF16), w2_ref[cols, :], preferred_element_type=F32)
        acc = part if acc is None else acc + part
    o_ref[0] = x1 + mod_ref[0, 5:6, :] * acc


def _out_mlp(xs, outs, mod, g2, w_out, w1, w2, ctx_tokens, tm=256):
    b = xs.shape[0]
    t_out = T_ALL if ctx_tokens else SEQ
    nt = t_out // tm
    ctx_row = mod.shape[0] - 1
    ctx_tile = SEQ // tm

    def mod_map(i, t):
        return (jnp.where(t >= ctx_tile, ctx_row, i), 0, 0)

    tile = lambda w: pl.BlockSpec((1, tm, w), lambda i, t: (i, t, 0))
    whole = lambda s: pl.BlockSpec(s, lambda i, t: (0, 0))
    return pl.pallas_call(
        _outmlp_kernel,
        out_shape=jax.ShapeDtypeStruct((b, t_out, D_MODEL), F32),
        grid=(b, nt),
        in_specs=[
            tile(D_MODEL), tile(2 * LANES), tile(2 * LANES), tile(2 * LANES), tile(2 * LANES),
            pl.BlockSpec((1, 6, D_MODEL), mod_map),
            whole((1, D_MODEL)), whole((D_MODEL, D_MODEL)), whole((D_MODEL, D_FF)), whole((D_FF, D_MODEL)),
        ],
        out_specs=tile(D_MODEL),
        compiler_params=_params(("parallel", "parallel")),
        name="out_mlp",
    )(xs, *outs, mod, g2, w_out, w1, w2)


def _rope_tables():
    pos = np.arange(SEQ)
    rows, cols = (pos // GRID_W).astype(np.float32), (pos % GRID_W).astype(np.float32)

    def pattern(half):
        freqs = (np.float32(ROPE_BASE) ** (-np.arange(half, dtype=np.float32) / np.float32(half))).astype(np.float32)
        ang = [(p[:, None] * freqs[None, :]).astype(np.float32) for p in (rows, cols)]
        cos = np.concatenate([np.cos(a) for a in (ang[0], ang[0], ang[1], ang[1])], axis=1)
        sin = np.concatenate([-np.sin(ang[0]), np.sin(ang[0]), -np.sin(ang[1]), np.sin(ang[1])], axis=1)
        reps = LANES // (4 * half)
        cos = np.concatenate([np.tile(cos, (1, reps)), np.ones((CTX_LEN, LANES))], axis=0)
        sin = np.concatenate([np.tile(sin, (1, reps)), np.zeros((CTX_LEN, LANES))], axis=0)
        return jnp.asarray(cos, F32), jnp.asarray(sin, F32)

    ca, sa = pattern(HEAD_DIM // 4)
    cb, sb = pattern(B_QK_DIM // 4)
    return ca, sa, cb, sb


def _gain_row(aq, ak, bq, bk, dq, dk):
    one = jnp.ones((LANES,), F32)
    t = lambda g, s=1.0: jnp.tile(g.astype(F32), LANES // g.shape[0]) * s
    groups = [one] * N_GROUPS
    groups[G_AQ] = groups[G_AQ + 1] = t(aq, HEAD_DIM ** -0.5)
    groups[G_AK] = t(ak)
    groups[G_BQ] = groups[G_BQ + 1] = t(bq, B_QK_DIM ** -0.5)
    groups[G_BK] = groups[G_BK + 1] = t(bk)
    groups[G_CK] = groups[G_CK + 1] = one * HEAD_DIM ** -0.5
    groups[G_DQ] = groups[G_DQ + 1] = t(dq, HEAD_DIM ** -0.5)
    groups[G_DK] = groups[G_DK + 1] = t(dk)
    return jnp.concatenate(groups)[None, :]


def _nbr_bias(rpb):
    qc = np.arange(GRID_W)[:, None]
    kc = np.arange(GRID_W)[None, :]
    cstart = np.clip(qc - NA_COLS // 2, 0, GRID_W - NA_COLS)
    valid = (kc >= cstart) & (kc < cstart + NA_COLS)
    dc = np.clip(kc - qc + NA_COLS - 1, 0, 2 * NA_COLS - 2)
    dr = np.arange(NA_ROWS)[:, None] + np.arange(NA_ROWS)[None, :]
    b = rpb.astype(F32)[:, dr[:, None, :, None], dc[None, :, None, :]]
    b = jnp.where(valid[None, None, :, None, :], b, NEG_INF)
    return b.reshape(rpb.shape[0], NA_ROWS, GRID_W, NA_ROWS * GRID_W)


def kernel(x, c, ctx, c_ctx, w_mod, b_mod, norm1_g, norm2_g, w_in, w_out, a_qnorm_g, a_knorm_g, a_sink, b_qnorm_g, b_knorm_g, b_lambda_q1, b_lambda_k1, b_lambda_q2, b_lambda_k2, b_subln_g, c_decay_fwd, c_decay_bwd, c_gn_g, d_qnorm_g, d_knorm_g, d_rpb, w_mlp1, w_mlp2):
    depth = w_mod.shape[0]
    b = x.shape[0]
    assert b + 1 <= 8
    cc = jnp.concatenate([c, c_ctx[None, :], jnp.zeros((8 - b - 1, D_MODEL), F32)], axis=0)
    mods = _modulation(cc, w_mod, b_mod)[:, :b + 1].reshape(depth, b + 1, 6, D_MODEL)
    tabs = _rope_tables()
    xs = jnp.concatenate([x, ctx], axis=1)

    for layer in range(depth):
        need_ctx = layer < depth - 1
        lambda_init = 0.8 - 0.6 * math.exp(-0.3 * layer)
        gain = _gain_row(a_qnorm_g[layer], a_knorm_g[layer], b_qnorm_g[layer], b_knorm_g[layer],
                         d_qnorm_g[layer], d_knorm_g[layer])
        qkv, gate = _in_proj(xs, mods[layer], norm1_g[layer][None, :], w_in[layer].astype(BF16), gain, tabs)

        out_a = _attn_a(qkv, a_sink[layer].astype(F32), need_ctx)
        kt = jnp.swapaxes(qkv[:, :, G_BK * LANES:(G_BK + 2) * LANES], 1, 2)
        lam_vecs = jnp.stack([b_lambda_q1[layer], b_lambda_k1[layer],
                              b_lambda_q2[layer], b_lambda_k2[layer]]).astype(F32)
        out_b = _attn_b(qkv, kt, lam_vecs, b_subln_g[layer][None, :], need_ctx, lambda_init)
        dec = jnp.stack([c_decay_fwd[layer].reshape(2, 2), c_decay_bwd[layer].reshape(2, 2)],
                        axis=1).reshape(2, 1, 4).astype(F32)
        out_c = _retention(qkv, gate, dec, c_gn_g[layer][None, :], need_ctx)
        out_d = _attn_d(qkv, _nbr_bias(d_rpb[layer]), need_ctx)

        xs = _out_mlp(xs, (out_a, out_b, out_c, out_d), mods[layer], norm2_g[layer][None, :],
                      w_out[layer].astype(BF16), w_mlp1[layer].astype(BF16), w_mlp2[layer].astype(BF16),
                      need_ctx)
    return xs
```

```python
import functools
import math

import numpy as np
import jax
import jax.numpy as jnp
from jax import lax
from jax.experimental import pallas as pl
from jax.experimental.pallas import tpu as pltpu

F32 = jnp.float32
BF16 = jnp.bfloat16

D_MODEL = 1024
SEQ = 4096
CTX_LEN = 256
T_ALL = SEQ + CTX_LEN
GRID_W = 64
HEAD_DIM = 64
B_QK_DIM = 32
BLK = 128
N_LAT_BLK = SEQ // BLK
N_ALL_BLK = T_ALL // BLK
NA_ROWS = 8
NA_COLS = 16
D_FF = 4 * D_MODEL
IN_WIDTH = 3072
ROPE_BASE = 10000.0
EPS = 1e-6
NEG_INF = -1e30
LANES = 128
VMEM_LIMIT = 52 * 1024 * 1024

G_AQ, G_AK, G_AV = 0, 2, 3
G_BQ, G_BK, G_BV = 4, 6, 8
G_CQ, G_CK, G_CV, G_CG = 10, 12, 14, 16
G_DQ, G_DK, G_DV = 18, 20, 22
N_GROUPS = IN_WIDTH // LANES

_NT = (((1,), (1,)), ((), ()))
_TN = (((0,), (0,)), ((), ()))


def _params(sem):
    return pltpu.CompilerParams(dimension_semantics=sem, vmem_limit_bytes=VMEM_LIMIT)


def _rms(x):
    return x * lax.rsqrt(jnp.mean(x * x, axis=-1, keepdims=True) + EPS)


def _mod_kernel(c_ref, w_ref, b_ref, o_ref):
    c = c_ref[...]
    s = c / (1.0 + jnp.exp(-c))
    s_hi = s.astype(BF16)
    s_lo = (s - s_hi.astype(F32)).astype(BF16)
    w = w_ref[0]
    w_hi = w.astype(BF16)
    w_lo = (w - w_hi.astype(F32)).astype(BF16)
    acc = jnp.dot(s_hi, w_hi, preferred_element_type=F32)
    acc += jnp.dot(s_lo, w_hi, preferred_element_type=F32)
    acc += jnp.dot(s_hi, w_lo, preferred_element_type=F32)
    o_ref[0] = acc + b_ref[0]


def _modulation(cc, w_mod, b_mod):
    depth = w_mod.shape[0]
    n_col = 6 * D_MODEL // 1024
    return pl.pallas_call(
        _mod_kernel,
        out_shape=jax.ShapeDtypeStruct((depth, 8, 6 * D_MODEL), F32),
        grid=(depth, n_col),
        in_specs=[
            pl.BlockSpec((8, D_MODEL), lambda l, j: (0, 0)),
            pl.BlockSpec((1, D_MODEL, 1024), lambda l, j: (l, 0, j)),
            pl.BlockSpec((1, 1, 1024), lambda l, j: (l, 0, j)),
        ],
        out_specs=pl.BlockSpec((1, 8, 1024), lambda l, j: (l, 0, j)),
        compiler_params=_params(("parallel", "parallel")),
        name="adaln_mod",
    )(cc, w_mod, b_mod.reshape(depth, 1, 6 * D_MODEL))


def _seg_inv_rms(v, seg):
    sq = v * v
    hi = sq.astype(BF16)
    lo = (sq - hi.astype(F32)).astype(BF16)
    r = lax.broadcasted_iota(jnp.int32, (LANES, LANES), 0) // seg
    c = lax.broadcasted_iota(jnp.int32, (LANES, LANES), 1) // seg
    bd = jnp.where(r == c, 1.0, 0.0).astype(BF16)
    ssum = jnp.dot(hi, bd, preferred_element_type=F32) + jnp.dot(lo, bd, preferred_element_type=F32)
    return lax.rsqrt(ssum * (1.0 / seg) + EPS)


def _rotate(v, cos, sin_signed, half):
    up = pltpu.roll(v, LANES - half, 1)
    dn = pltpu.roll(v, half, 1)
    lane = lax.broadcasted_iota(jnp.int32, v.shape, 1)
    partner = jnp.where(lane % (2 * half) < half, up, dn)
    return v * cos + partner * sin_signed


def _inproj_kernel(x_ref, mod_ref, g1_ref, w_ref, gain_ref, ca_ref, sa_ref, cb_ref, sb_ref,
                   qkv_ref, gate_ref):
    x = x_ref[0]
    sh = mod_ref[0, 0:1, :]
    sc = mod_ref[0, 1:2, :]
    h = (_rms(x) * g1_ref[...] * (1.0 + sc) + sh).astype(BF16)
    acc = jnp.dot(h, w_ref[...], preferred_element_type=F32)
    for g in range(N_GROUPS):
        cols = slice(g * LANES, (g + 1) * LANES)
        v = acc[:, cols]
        gain = gain_ref[:, cols]
        if g in (G_AQ, G_AQ + 1, G_AK):
            v = v * _seg_inv_rms(v, HEAD_DIM) * gain
            v = _rotate(v, ca_ref[...], sa_ref[...], HEAD_DIM // 4)
        elif g in (G_BQ, G_BQ + 1, G_BK, G_BK + 1):
            v = v * _seg_inv_rms(v, B_QK_DIM) * gain
            v = _rotate(v, cb_ref[...], sb_ref[...], B_QK_DIM // 4)
        elif g in (G_DQ, G_DQ + 1, G_DK, G_DK + 1):
            v = v * _seg_inv_rms(v, HEAD_DIM) * gain
        elif g in (G_CK, G_CK + 1):
            v = v * gain
        elif g in (G_CG, G_CG + 1):
            gcols = slice((g - G_CG) * LANES, (g - G_CG + 1) * LANES)
            gate_ref[0, :, gcols] = v / (1.0 + jnp.exp(-v))
        qkv_ref[0, :, cols] = v.astype(BF16)


def _in_proj(xs, mod, g1, w_in, gain, tabs, tm=256):
    b = xs.shape[0]
    nt = T_ALL // tm
    ctx_row = mod.shape[0] - 1

    def mod_map(t, i):
        return (jnp.where(t == nt - 1, ctx_row, i), 0, 0)

    tab_spec = pl.BlockSpec((tm, LANES), lambda t, i: (t, 0))
    return pl.pallas_call(
        _inproj_kernel,
        out_shape=(jax.ShapeDtypeStruct((b, T_ALL, IN_WIDTH), BF16),
                   jax.ShapeDtypeStruct((b, T_ALL, 2 * LANES), F32)),
        grid=(nt, b),
        in_specs=[
            pl.BlockSpec((1, tm, D_MODEL), lambda t, i: (i, t, 0)),
            pl.BlockSpec((1, 6, D_MODEL), mod_map),
            pl.BlockSpec((1, D_MODEL), lambda t, i: (0, 0)),
            pl.BlockSpec((D_MODEL, IN_WIDTH), lambda t, i: (0, 0)),
            pl.BlockSpec((1, IN_WIDTH), lambda t, i: (0, 0)),
            tab_spec, tab_spec, tab_spec, tab_spec,
        ],
        out_specs=(pl.BlockSpec((1, tm, IN_WIDTH), lambda t, i: (i, t, 0)),
                   pl.BlockSpec((1, tm, 2 * LANES), lambda t, i: (i, t, 0))),
        compiler_params=_params(("parallel", "parallel")),
        name="in_proj",
    )(xs, mod, g1, w_in, gain, *tabs)


def _softmax_sink_pv(s, sink_col, v):
    m = jnp.maximum(jnp.max(s, axis=1, keepdims=True), sink_col)
    e = jnp.exp(s - m)
    den = jnp.sum(e, axis=1, keepdims=True) + jnp.exp(sink_col - m)
    o = jnp.dot(e.astype(BF16), v, preferred_element_type=F32)
    return o / den


def _a_kernel(sink_ref, q_ref, kp_ref, kc_ref, kn_ref, vp_ref, vc_ref, vn_ref, kx_ref, vx_ref,
              o_ref, *, ctx_queries):
    j = pl.program_id(1)
    nq = 2 * BLK
    row = lax.broadcasted_iota(jnp.int32, (nq, 1), 0)

    def sink_col(p):
        return jnp.where(row < BLK, sink_ref[2 * p], sink_ref[2 * p + 1])

    def stacked_q(p):
        qp = q_ref[0, :, p * LANES:(p + 1) * LANES]
        return jnp.concatenate([qp[:, :HEAD_DIM], qp[:, HEAD_DIM:]], axis=0)

    def store(p, o):
        lo = p * HEAD_DIM
        o = o[:, lo:lo + HEAD_DIM].astype(BF16)
        o_ref[0, :, p * LANES:p * LANES + HEAD_DIM] = o[:BLK]
        o_ref[0, :, p * LANES + HEAD_DIM:(p + 1) * LANES] = o[BLK:]

    def latent():
        k_all = jnp.concatenate([kx_ref[0], kp_ref[0], kc_ref[0], kn_ref[0]], axis=0)
        v_all = jnp.concatenate([vx_ref[0], vp_ref[0], vc_ref[0], vn_ref[0]], axis=0)
        nk = CTX_LEN + 3 * BLK
        qi = lax.broadcasted_iota(jnp.int32, (nq, nk), 0) % BLK
        col = lax.broadcasted_iota(jnp.int32, (nq, nk), 1)
        c = col - CTX_LEN
        c_lo = jnp.where(j > 0, 0, BLK)
        c_hi = jnp.where(j < N_LAT_BLK - 1, 3 * BLK, 2 * BLK)
        valid = (col < CTX_LEN) | ((jnp.abs(c - BLK - qi) <= BLK) & (c >= c_lo) & (c < c_hi))
        for p in range(2):
            kg = k_all[:, p * HEAD_DIM:(p + 1) * HEAD_DIM]
            s = lax.dot_general(stacked_q(p), kg, _NT, preferred_element_type=F32)
            s = jnp.where(valid, s, NEG_INF)
            store(p, _softmax_sink_pv(s, sink_col(p), v_all))

    def context():
        for p in range(2):
            kg = kx_ref[0, :, p * HEAD_DIM:(p + 1) * HEAD_DIM]
            s = lax.dot_general(stacked_q(p), kg, _NT, preferred_element_type=F32)
            store(p, _softmax_sink_pv(s, sink_col(p), vx_ref[0]))

    if ctx_queries:
        pl.when(j < N_LAT_BLK)(latent)
        pl.when(j >= N_LAT_BLK)(context)
    else:
        latent()


def _attn_a(qkv, sink, ctx_queries):
    b = qkv.shape[0]
    nqb = N_ALL_BLK if ctx_queries else N_LAT_BLK
    last = N_LAT_BLK - 1

    def blk(colblk, fn):
        return pl.BlockSpec((1, BLK, LANES), lambda i, j: (i, fn(j), colblk))

    prev = lambda j: jnp.clip(j - 1, 0, last)
    cur = lambda j: jnp.minimum(j, last)
    nxt = lambda j: jnp.clip(j + 1, 0, last)
    ctx_spec = lambda colblk: pl.BlockSpec((1, CTX_LEN, LANES), lambda i, j: (i, SEQ // CTX_LEN, colblk))
    return pl.pallas_call(
        functools.partial(_a_kernel, ctx_queries=ctx_queries),
        out_shape=jax.ShapeDtypeStruct((b, nqb * BLK, 2 * LANES), BF16),
        grid=(b, nqb),
        in_specs=[
            pl.BlockSpec(memory_space=pltpu.SMEM),
            pl.BlockSpec((1, BLK, 2 * LANES), lambda i, j: (i, j, 0)),
            blk(G_AK, prev), blk(G_AK, cur), blk(G_AK, nxt),
            blk(G_AV, prev), blk(G_AV, cur), blk(G_AV, nxt),
            ctx_spec(G_AK), ctx_spec(G_AV),
        ],
        out_specs=pl.BlockSpec((1, BLK, 2 * LANES), lambda i, j: (i, j, 0)),
        compiler_params=_params(("parallel", "parallel")),
        name="attn_window",
    )(sink, qkv, qkv, qkv, qkv, qkv, qkv, qkv, qkv, qkv)


def _b_kernel(lam_ref, subg_ref, q_ref, kt_ref, v_ref, o_ref, *, ctx_queries, lambda_init):
    j = pl.program_id(2)
    lv = lam_ref[...]
    lam = (jnp.exp(jnp.sum(lv[0:1] * lv[1:2], axis=1, keepdims=True))
           - jnp.exp(jnp.sum(lv[2:3] * lv[3:4], axis=1, keepdims=True)) + lambda_init)
    q = q_ref[0]
    lane = lax.broadcasted_iota(jnp.int32, (1, LANES), 1)

    def run(k0, nk):
        outs = []
        for hh in range(2):
            probs = []
            for sub in range(2):
                r0 = hh * HEAD_DIM + sub * B_QK_DIM
                qs = jnp.where((lane >= r0) & (lane < r0 + B_QK_DIM), q, jnp.zeros_like(q))
                s = jnp.dot(qs, kt_ref[0, :, k0:k0 + nk], preferred_element_type=F32)
                e = jnp.exp(s - jnp.max(s, axis=1, keepdims=True))
                probs.append(e * (1.0 / jnp.sum(e, axis=1, keepdims=True)))
            p = (probs[0] - lam * probs[1]).astype(BF16)
            o = jnp.dot(p, v_ref[0, k0:k0 + nk, :], preferred_element_type=F32)
            o = o[:, hh * HEAD_DIM:(hh + 1) * HEAD_DIM]
            outs.append(_rms(o) * subg_ref[...] * (1.0 - lambda_init))
        o_ref[0] = jnp.concatenate(outs, axis=1).astype(BF16)

    if ctx_queries:
        pl.when(j < N_LAT_BLK)(lambda: run(0, T_ALL))
        pl.when(j >= N_LAT_BLK)(lambda: run(SEQ, CTX_LEN))
    else:
        run(0, T_ALL)


def _attn_b(qkv, kt, lam_vecs, subg, ctx_queries, lambda_init):
    b = qkv.shape[0]
    nqb = N_ALL_BLK if ctx_queries else N_LAT_BLK
    return pl.pallas_call(
        functools.partial(_b_kernel, ctx_queries=ctx_queries, lambda_init=lambda_init),
        out_shape=jax.ShapeDtypeStruct((b, nqb * BLK, 2 * LANES), BF16),
        grid=(b, 2, nqb),
        in_specs=[
            pl.BlockSpec((4, B_QK_DIM), lambda i, p, j: (0, 0)),
            pl.BlockSpec((1, HEAD_DIM), lambda i, p, j: (0, 0)),
            pl.BlockSpec((1, BLK, LANES), lambda i, p, j: (i, j, G_BQ + p)),
            pl.BlockSpec((1, LANES, T_ALL), lambda i, p, j: (i, p, 0)),
            pl.BlockSpec((1, T_ALL, LANES), lambda i, p, j: (i, 0, G_BV + p)),
        ],
        out_specs=pl.BlockSpec((1, BLK, LANES), lambda i, p, j: (i, j, p)),
        compiler_params=_params(("parallel", "parallel", "parallel")),
        name="attn_diff",
    )(lam_vecs, subg, qkv, kt, qkv)


def _c_kernel(dec_ref, gn_ref, q_ref, k_ref, v_ref, g_ref, o_ref, fwd_ref, *, ctx_out):
    ii = lax.broadcasted_iota(jnp.int32, (BLK, BLK), 0)
    jj = lax.broadcasted_iota(jnp.int32, (BLK, BLK), 1)
    pos = lax.broadcasted_iota(jnp.int32, (BLK, 1), 0).astype(F32)
    rel = (ii - jj).astype(F32)
    n_ctx = CTX_LEN // BLK

    for hh in range(2):
        lanes = slice(hh * HEAD_DIM, (hh + 1) * HEAD_DIM)

        def log_gamma(col):
            x = dec_ref[0, :, col:col + 1]
            return jnp.minimum(x, 0.0) - jnp.log1p(jnp.exp(-jnp.abs(x)))

        def chunk_out(r0, state, inner_decay, q_decay, k_decay, chunk_decay):
            q = q_ref[0, pl.ds(r0, BLK), lanes]
            k = k_ref[0, pl.ds(r0, BLK), lanes]
            v = v_ref[0, pl.ds(r0, BLK), lanes]
            inner = lax.dot_general(q, k, _NT, preferred_element_type=F32) * inner_decay
            o = jnp.dot(inner.astype(BF16), v, preferred_element_type=F32)
            o += jnp.dot((q.astype(F32) * q_decay).astype(BF16), state.astype(BF16),
                         preferred_element_type=F32)
            kv = lax.dot_general((k.astype(F32) * k_decay).astype(BF16), v, _TN,
                                 preferred_element_type=F32)
            return o, state * chunk_decay + kv

        lg = log_gamma(hh)
        f_decays = (jnp.where(rel >= 0, jnp.exp(rel * lg), 0.0), jnp.exp((pos + 1.0) * lg),
                    jnp.exp((BLK - 1.0 - pos) * lg), jnp.exp(BLK * lg))

        def fwd_step(i, state):
            c = jnp.where(i < n_ctx, N_LAT_BLK + i, i - n_ctx)
            r0 = pl.multiple_of(c * BLK, BLK)
            o, state = chunk_out(r0, state, *f_decays)
            fwd_ref[pl.ds(r0, BLK), lanes] = o
            return state

        lax.fori_loop(0, N_ALL_BLK, fwd_step, jnp.zeros((HEAD_DIM, HEAD_DIM), F32))

        lg = log_gamma(2 + hh)
        b_decays = (jnp.where(rel <= 0, jnp.exp(-rel * lg), 0.0), jnp.exp((BLK - pos) * lg),
                    jnp.exp(pos * lg), jnp.exp(BLK * lg))

        def bwd_step(i, state, write):
            c = N_ALL_BLK - 1 - i
            r0 = pl.multiple_of(c * BLK, BLK)
            o, state = chunk_out(r0, state, *b_decays)
            if write:
                tot = fwd_ref[pl.ds(r0, BLK), lanes] + o
                y = _rms(tot) * gn_ref[...] * g_ref[0, pl.ds(r0, BLK), lanes]
                o_ref[0, pl.ds(r0, BLK), lanes] = y.astype(BF16)
            return state

        state = lax.fori_loop(0, n_ctx, functools.partial(bwd_step, write=ctx_out),
                              jnp.zeros((HEAD_DIM, HEAD_DIM), F32))
        lax.fori_loop(n_ctx, N_ALL_BLK, functools.partial(bwd_step, write=True), state)


def _retention(qkv, gate, dec, gn, ctx_out):
    b = qkv.shape[0]
    t_out = T_ALL if ctx_out else SEQ
    col = lambda g: pl.BlockSpec((1, T_ALL, LANES), lambda i, p: (i, 0, g + p))
    return pl.pallas_call(
        functools.partial(_c_kernel, ctx_out=ctx_out),
        out_shape=jax.ShapeDtypeStruct((b, t_out, 2 * LANES), BF16),
        grid=(b, 2),
        in_specs=[
            pl.BlockSpec((1, 1, 4), lambda i, p: (p, 0, 0)),
            pl.BlockSpec((1, HEAD_DIM), lambda i, p: (0, 0)),
            col(G_CQ), col(G_CK), col(G_CV),
            pl.BlockSpec((1, T_ALL, LANES), lambda i, p: (i, 0, p)),
        ],
        out_specs=pl.BlockSpec((1, t_out, LANES), lambda i, p: (i, 0, p)),
        scratch_shapes=[pltpu.VMEM((T_ALL, LANES), F32)],
        compiler_params=_params(("parallel", "parallel")),
        name="retention",
    )(dec, gn, qkv, qkv, qkv, gate)


def _d_kernel(q_ref, k_ref, v_ref, bias_ref, o_ref, *, ctx_queries):
    rows = SEQ // GRID_W
    slab = NA_ROWS * GRID_W
    kx = k_ref[0, SEQ:T_ALL, :]
    vx = v_ref[0, SEQ:T_ALL, :]
    head0 = lax.broadcasted_iota(jnp.int32, (1, LANES), 1) < HEAD_DIM

    def attend(q, score_parts):
        out = None
        for hh in range(2):
            mine = head0 if hh == 0 else jnp.logical_not(head0)
            qh = jnp.where(mine, q, jnp.zeros_like(q))
            parts = []
            for k, v, bias in score_parts:
                s = lax.dot_general(qh, k, _NT, preferred_element_type=F32)
                parts.append((s if bias is None else s + bias(hh), v))
            m = functools.reduce(jnp.maximum, [jnp.max(s, axis=1, keepdims=True) for s, _ in parts])
            es = [(jnp.exp(s - m), v) for s, v in parts]
            den = sum(jnp.sum(e, axis=1, keepdims=True) for e, _ in es)
            o = sum(jnp.dot(e.astype(BF16), v, preferred_element_type=F32) for e, v in es) / den
            out = o if out is None else jnp.where(mine, o, out)
        return out.astype(BF16)

    def row_step(r, carry):
        start = pl.multiple_of(jnp.clip(r - NA_ROWS // 2, 0, rows - NA_ROWS) * GRID_W, GRID_W)
        dr0 = jnp.where(r <= NA_ROWS // 2, NA_ROWS - 1 - r,
                        jnp.where(r >= rows - NA_ROWS // 2, rows - 1 - r, NA_ROWS // 2 - 1))
        q0 = pl.multiple_of(r * GRID_W, GRID_W)
        q = q_ref[0, pl.ds(q0, GRID_W), :]
        ks = k_ref[0, pl.ds(start, slab), :]
        vs = v_ref[0, pl.ds(start, slab), :]
        o_ref[0, pl.ds(q0, GRID_W), :] = attend(
            q, [(ks, vs, lambda hh: bias_ref[hh, dr0]), (kx, vx, None)])
        return carry

    lax.fori_loop(0, rows, row_step, 0)
    if ctx_queries:
        o_ref[0, SEQ:T_ALL, :] = attend(q_ref[0, SEQ:T_ALL, :], [(kx, vx, None)])


def _attn_d(qkv, bias, ctx_queries):
    b = qkv.shape[0]
    t_out = T_ALL if ctx_queries else SEQ
    col = lambda g: pl.BlockSpec((1, T_ALL, LANES), lambda i, p: (i, 0, g + p))
    return pl.pallas_call(
        functools.partial(_d_kernel, ctx_queries=ctx_queries),
        out_shape=jax.ShapeDtypeStruct((b, t_out, 2 * LANES), BF16),
        grid=(b, 2),
        in_specs=[
            col(G_DQ), col(G_DK), col(G_DV),
            pl.BlockSpec((2, NA_ROWS, GRID_W, NA_ROWS * GRID_W), lambda i, p: (p, 0, 0, 0)),
        ],
        out_specs=pl.BlockSpec((1, t_out, LANES), lambda i, p: (i, 0, p)),
        compiler_params=_params(("parallel", "parallel")),
        name="attn_nbr",
    )(qkv, qkv, qkv, bias)


def _outmlp_kernel(x_ref, a_ref, b_ref, c_ref, d_ref, mod_ref, g2_ref, wo_ref, w1_ref, w2_ref, o_ref):
    gw = 2 * LANES
    mix = None
    for n, r in enumerate((a_ref, b_ref, c_ref, d_ref)):
        part = jnp.dot(r[0], wo_ref[n * gw:(n + 1) * gw, :], preferred_element_type=F32)
        mix = part if mix is None else mix + part
    x1 = x_ref[0] + mod_ref[0, 2:3, :] * mix
    h2 = (_rms(x1) * g2_ref[...] * (1.0 + mod_ref[0, 4:5, :]) + mod_ref[0, 3:4, :]).astype(BF16)
    acc = None
    for n in range(D_FF // D_MODEL):
        cols = slice(n * D_MODEL, (n + 1) * D_MODEL)
        hid = jnp.maximum(jnp.dot(h2, w1_ref[:, cols], preferred_element_type=F32), 0.0)
        part = jnp.dot((hid * hid).astype(BF16), w2_ref[cols, :], preferred_element_type=F32)
        acc = part if acc is None else acc + part
    o_ref[0] = x1 + mod_ref[0, 5:6, :] * acc


def _out_mlp(xs, outs, mod, g2, w_out, w1, w2, ctx_tokens, tm=256):
    b = xs.shape[0]
    t_out = T_ALL if ctx_tokens else SEQ
    nt = t_out // tm
    ctx_row = mod.shape[0] - 1
    ctx_tile = SEQ // tm

    def mod_map(i, t):
        return (jnp.where(t >= ctx_tile, ctx_row, i), 0, 0)

    tile = lambda w: pl.BlockSpec((1, tm, w), lambda i, t: (i, t, 0))
    whole = lambda s: pl.BlockSpec(s, lambda i, t: (0, 0))
    return pl.pallas_call(
        _outmlp_kernel,
        out_shape=jax.ShapeDtypeStruct((b, t_out, D_MODEL), F32),
        grid=(b, nt),
        in_specs=[
            tile(D_MODEL), tile(2 * LANES), tile(2 * LANES), tile(2 * LANES), tile(2 * LANES),
            pl.BlockSpec((1, 6, D_MODEL), mod_map),
            whole((1, D_MODEL)), whole((D_MODEL, D_MODEL)), whole((D_MODEL, D_FF)), whole((D_FF, D_MODEL)),
        ],
        out_specs=tile(D_MODEL),
        compiler_params=_params(("parallel", "parallel")),
        name="out_mlp",
    )(xs, *outs, mod, g2, w_out, w1, w2)


def _rope_tables():
    pos = np.arange(SEQ)
    rows, cols = (pos // GRID_W).astype(np.float32), (pos % GRID_W).astype(np.float32)

    def pattern(half):
        freqs = (np.float32(ROPE_BASE) ** (-np.arange(half, dtype=np.float32) / np.float32(half))).astype(np.float32)
        ang = [(p[:, None] * freqs[None, :]).astype(np.float32) for p in (rows, cols)]
        cos = np.concatenate([np.cos(a) for a in (ang[0], ang[0], ang[1], ang[1])], axis=1)
        sin = np.concatenate([-np.sin(ang[0]), np.sin(ang[0]), -np.sin(ang[1]), np.sin(ang[1])], axis=1)
        reps = LANES // (4 * half)
        cos = np.concatenate([np.tile(cos, (1, reps)), np.ones((CTX_LEN, LANES))], axis=0)
        sin = np.concatenate([np.tile(sin, (1, reps)), np.zeros((CTX_LEN, LANES))], axis=0)
        return jnp.asarray(cos, F32), jnp.asarray(sin, F32)

    ca, sa = pattern(HEAD_DIM // 4)
    cb, sb = pattern(B_QK_DIM // 4)
    return ca, sa, cb, sb


def _gain_row(aq, ak, bq, bk, dq, dk):
    one = jnp.ones((LANES,), F32)
    t = lambda g, s=1.0: jnp.tile(g.astype(F32), LANES // g.shape[0]) * s
    groups = [one] * N_GROUPS
    groups[G_AQ] = groups[G_AQ + 1] = t(aq, HEAD_DIM ** -0.5)
    groups[G_AK] = t(ak)
    groups[G_BQ] = groups[G_BQ + 1] = t(bq, B_QK_DIM ** -0.5)
    groups[G_BK] = groups[G_BK + 1] = t(bk)
    groups[G_CK] = groups[G_CK + 1] = one * HEAD_DIM ** -0.5
    groups[G_DQ] = groups[G_DQ + 1] = t(dq, HEAD_DIM ** -0.5)
    groups[G_DK] = groups[G_DK + 1] = t(dk)
    return jnp.concatenate(groups)[None, :]


def _nbr_bias(rpb):
    qc = np.arange(GRID_W)[:, None]
    kc = np.arange(GRID_W)[None, :]
    cstart = np.clip(qc - NA_COLS // 2, 0, GRID_W - NA_COLS)
    valid = (kc >= cstart) & (kc < cstart + NA_COLS)
    pad = GRID_W - NA_COLS
    rp = jnp.pad(rpb.astype(F32), ((0, 0), (0, 0), (pad, pad)))
    toep = jnp.stack([rp[:, :, GRID_W - 1 - q:2 * GRID_W - 1 - q] for q in range(GRID_W)], axis=2)
    toep = jnp.where(valid[None, None], toep, NEG_INF)
    b = jnp.stack([toep[:, kr:kr + NA_ROWS] for kr in range(NA_ROWS)], axis=3)
    return b.reshape(rpb.shape[0], NA_ROWS, GRID_W, NA_ROWS * GRID_W)


def kernel(x, c, ctx, c_ctx, w_mod, b_mod, norm1_g, norm2_g, w_in, w_out, a_qnorm_g, a_knorm_g, a_sink, b_qnorm_g, b_knorm_g, b_lambda_q1, b_lambda_k1, b_lambda_q2, b_lambda_k2, b_subln_g, c_decay_fwd, c_decay_bwd, c_gn_g, d_qnorm_g, d_knorm_g, d_rpb, w_mlp1, w_mlp2):
    depth = w_mod.shape[0]
    b = x.shape[0]
    assert b + 1 <= 8
    cc = jnp.concatenate([c, c_ctx[None, :], jnp.zeros((8 - b - 1, D_MODEL), F32)], axis=0)
    mods = _modulation(cc, w_mod, b_mod)[:, :b + 1].reshape(depth, b + 1, 6, D_MODEL)
    tabs = _rope_tables()
    xs = jnp.concatenate([x, ctx], axis=1)

    for layer in range(depth):
        need_ctx = layer < depth - 1
        lambda_init = 0.8 - 0.6 * math.exp(-0.3 * layer)
        gain = _gain_row(a_qnorm_g[layer], a_knorm_g[layer], b_qnorm_g[layer], b_knorm_g[layer],
                         d_qnorm_g[layer], d_knorm_g[layer])
        qkv, gate = _in_proj(xs, mods[layer], norm1_g[layer][None, :], w_in[layer].astype(BF16), gain, tabs)

        out_a = _attn_a(qkv, a_sink[layer].astype(F32), need_ctx)
        kt = jnp.swapaxes(qkv[:, :, G_BK * LANES:(G_BK + 2) * LANES], 1, 2)
        lam_vecs = jnp.stack([b_lambda_q1[layer], b_lambda_k1[layer],
                              b_lambda_q2[layer], b_lambda_k2[layer]]).astype(F32)
        out_b = _attn_b(qkv, kt, lam_vecs, b_subln_g[layer][None, :], need_ctx, lambda_init)
        dec = jnp.stack([c_decay_fwd[layer].reshape(2, 2), c_decay_bwd[layer].reshape(2, 2)],
                        axis=1).reshape(2, 1, 4).astype(F32)
        out_c = _retention(qkv, gate, dec, c_gn_g[layer][None, :], need_ctx)
        out_d = _attn_d(qkv, _nbr_bias(d_rpb[layer]), need_ctx)

        xs = _out_mlp(xs, (out_a, out_b, out_c, out_d), mods[layer], norm2_g[layer][None, :],
                      w_out[layer].astype(BF16), w_mlp1[layer].astype(BF16), w_mlp2[layer].astype(BF16),
                      need_ctx)
    return xs
```

```python
import functools
import math

import numpy as np
import jax
import jax.numpy as jnp
from jax import lax
from jax.experimental import pallas as pl
from jax.experimental.pallas import tpu as pltpu

F32 = jnp.float32
BF16 = jnp.bfloat16

D_MODEL = 1024
SEQ = 4096
CTX_LEN = 256
T_ALL = SEQ + CTX_LEN
GRID_W = 64
HEAD_DIM = 64
B_QK_DIM = 32
BLK = 128
N_LAT_BLK = SEQ // BLK
N_ALL_BLK = T_ALL // BLK
NA_ROWS = 8
NA_COLS = 16
D_FF = 4 * D_MODEL
IN_WIDTH = 3072
ROPE_BASE = 10000.0
EPS = 1e-6
NEG_INF = -1e30
LANES = 128
VMEM_LIMIT = 52 * 1024 * 1024

G_AQ, G_AK, G_AV = 0, 2, 3
G_BQ, G_BK, G_BV = 4, 6, 8
G_CQ, G_CK, G_CV, G_CG = 10, 12, 14, 16
G_DQ, G_DK, G_DV = 18, 20, 22
N_GROUPS = IN_WIDTH // LANES

_NT = (((1,), (1,)), ((), ()))
_TN = (((0,), (0,)), ((), ()))


def _params(sem):
    return pltpu.CompilerParams(dimension_semantics=sem, vmem_limit_bytes=VMEM_LIMIT)


def _rms(x):
    return x * lax.rsqrt(jnp.mean(x * x, axis=-1, keepdims=True) + EPS)


def _mod_kernel(c_ref, w_ref, b_ref, o_ref):
    c = c_ref[...]
    s = c / (1.0 + jnp.exp(-c))
    s_hi = s.astype(BF16)
    s_lo = (s - s_hi.astype(F32)).astype(BF16)
    w = w_ref[0]
    w_hi = w.astype(BF16)
    w_lo = (w - w_hi.astype(F32)).astype(BF16)
    acc = jnp.dot(s_hi, w_hi, preferred_element_type=F32)
    acc += jnp.dot(s_lo, w_hi, preferred_element_type=F32)
    acc += jnp.dot(s_hi, w_lo, preferred_element_type=F32)
    o_ref[0] = acc + b_ref[0]


def _modulation(cc, w_mod, b_mod):
    depth = w_mod.shape[0]
    n_col = 6 * D_MODEL // 1024
    return pl.pallas_call(
        _mod_kernel,
        out_shape=jax.ShapeDtypeStruct((depth, 8, 6 * D_MODEL), F32),
        grid=(depth, n_col),
        in_specs=[
            pl.BlockSpec((8, D_MODEL), lambda l, j: (0, 0)),
            pl.BlockSpec((1, D_MODEL, 1024), lambda l, j: (l, 0, j)),
            pl.BlockSpec((1, 1, 1024), lambda l, j: (l, 0, j)),
        ],
        out_specs=pl.BlockSpec((1, 8, 1024), lambda l, j: (l, 0, j)),
        compiler_params=_params(("parallel", "parallel")),
        name="adaln_mod",
    )(cc, w_mod, b_mod.reshape(depth, 1, 6 * D_MODEL))


def _seg_inv_rms(v, seg):
    sq = v * v
    hi = sq.astype(BF16)
    lo = (sq - hi.astype(F32)).astype(BF16)
    r = lax.broadcasted_iota(jnp.int32, (LANES, LANES), 0) // seg
    c = lax.broadcasted_iota(jnp.int32, (LANES, LANES), 1) // seg
    bd = jnp.where(r == c, 1.0, 0.0).astype(BF16)
    ssum = jnp.dot(hi, bd, preferred_element_type=F32) + jnp.dot(lo, bd, preferred_element_type=F32)
    return lax.rsqrt(ssum * (1.0 / seg) + EPS)


def _rotate(v, cos, sin_signed, half):
    up = pltpu.roll(v, LANES - half, 1)
    dn = pltpu.roll(v, half, 1)
    lane = lax.broadcasted_iota(jnp.int32, v.shape, 1)
    partner = jnp.where(lane % (2 * half) < half, up, dn)
    return v * cos + partner * sin_signed


def _inproj_kernel(x_ref, mod_ref, g1_ref, w_ref, gain_ref, ca_ref, sa_ref, cb_ref, sb_ref,
                   qkv_ref, gate_ref):
    x = x_ref[0]
    sh = mod_ref[0, 0:1, :]
    sc = mod_ref[0, 1:2, :]
    h = (_rms(x) * g1_ref[...] * (1.0 + sc) + sh).astype(BF16)
    acc = jnp.dot(h, w_ref[...], preferred_element_type=F32)
    for g in range(N_GROUPS):
        cols = slice(g * LANES, (g + 1) * LANES)
        v = acc[:, cols]
        gain = gain_ref[:, cols]
        if g in (G_AQ, G_AQ + 1, G_AK):
            v = v * _seg_inv_rms(v, HEAD_DIM) * gain
            v = _rotate(v, ca_ref[...], sa_ref[...], HEAD_DIM // 4)
        elif g in (G_BQ, G_BQ + 1, G_BK, G_BK + 1):
            v = v * _seg_inv_rms(v, B_QK_DIM) * gain
            v = _rotate(v, cb_ref[...], sb_ref[...], B_QK_DIM // 4)
        elif g in (G_DQ, G_DQ + 1, G_DK, G_DK + 1):
            v = v * _seg_inv_rms(v, HEAD_DIM) * gain
        elif g in (G_CK, G_CK + 1):
            v = v * gain
        elif g in (G_CG, G_CG + 1):
            gcols = slice((g - G_CG) * LANES, (g - G_CG + 1) * LANES)
            gate_ref[0, :, gcols] = v / (1.0 + jnp.exp(-v))
        qkv_ref[0, :, cols] = v.astype(BF16)


def _in_proj(xs, mod, g1, w_in, gain, tabs, tm=256):
    b = xs.shape[0]
    nt = T_ALL // tm
    ctx_row = mod.shape[0] - 1

    def mod_map(t, i):
        return (jnp.where(t == nt - 1, ctx_row, i), 0, 0)

    tab_spec = pl.BlockSpec((tm, LANES), lambda t, i: (t, 0))
    return pl.pallas_call(
        _inproj_kernel,
        out_shape=(jax.ShapeDtypeStruct((b, T_ALL, IN_WIDTH), BF16),
                   jax.ShapeDtypeStruct((b, T_ALL, 2 * LANES), F32)),
        grid=(nt, b),
        in_specs=[
            pl.BlockSpec((1, tm, D_MODEL), lambda t, i: (i, t, 0)),
            pl.BlockSpec((1, 6, D_MODEL), mod_map),
            pl.BlockSpec((1, D_MODEL), lambda t, i: (0, 0)),
            pl.BlockSpec((D_MODEL, IN_WIDTH), lambda t, i: (0, 0)),
            pl.BlockSpec((1, IN_WIDTH), lambda t, i: (0, 0)),
            tab_spec, tab_spec, tab_spec, tab_spec,
        ],
        out_specs=(pl.BlockSpec((1, tm, IN_WIDTH), lambda t, i: (i, t, 0)),
                   pl.BlockSpec((1, tm, 2 * LANES), lambda t, i: (i, t, 0))),
        compiler_params=_params(("parallel", "parallel")),
        name="in_proj",
    )(xs, mod, g1, w_in, gain, *tabs)


def _softmax_sink_pv(s, sink_col, v):
    m = jnp.maximum(jnp.max(s, axis=1, keepdims=True), sink_col)
    e = jnp.exp(s - m)
    den = jnp.sum(e, axis=1, keepdims=True) + jnp.exp(sink_col - m)
    o = jnp.dot(e.astype(BF16), v, preferred_element_type=F32)
    return o / den


def _a_kernel(sink_ref, q_ref, kp_ref, kc_ref, kn_ref, vp_ref, vc_ref, vn_ref, kx_ref, vx_ref,
              o_ref, *, ctx_queries):
    j = pl.program_id(1)
    nq = 2 * BLK
    row = lax.broadcasted_iota(jnp.int32, (nq, 1), 0)

    def sink_col(p):
        return jnp.where(row < BLK, sink_ref[2 * p], sink_ref[2 * p + 1])

    def stacked_q(p):
        qp = q_ref[0, :, p * LANES:(p + 1) * LANES]
        return jnp.concatenate([qp[:, :HEAD_DIM], qp[:, HEAD_DIM:]], axis=0)

    def store(p, o):
        lo = p * HEAD_DIM
        o = o[:, lo:lo + HEAD_DIM].astype(BF16)
        o_ref[0, :, p * LANES:p * LANES + HEAD_DIM] = o[:BLK]
        o_ref[0, :, p * LANES + HEAD_DIM:(p + 1) * LANES] = o[BLK:]

    def latent():
        k_all = jnp.concatenate([kx_ref[0], kp_ref[0], kc_ref[0], kn_ref[0]], axis=0)
        v_all = jnp.concatenate([vx_ref[0], vp_ref[0], vc_ref[0], vn_ref[0]], axis=0)
        nk = CTX_LEN + 3 * BLK
        qi = lax.broadcasted_iota(jnp.int32, (nq, nk), 0) % BLK
        col = lax.broadcasted_iota(jnp.int32, (nq, nk), 1)
        c = col - CTX_LEN
        c_lo = jnp.where(j > 0, 0, BLK)
        c_hi = jnp.where(j < N_LAT_BLK - 1, 3 * BLK, 2 * BLK)
        valid = (col < CTX_LEN) | ((jnp.abs(c - BLK - qi) <= BLK) & (c >= c_lo) & (c < c_hi))
        for p in range(2):
            kg = k_all[:, p * HEAD_DIM:(p + 1) * HEAD_DIM]
            s = lax.dot_general(stacked_q(p), kg, _NT, preferred_element_type=F32)
            s = jnp.where(valid, s, NEG_INF)
            store(p, _softmax_sink_pv(s, sink_col(p), v_all))

    def context():
        for p in range(2):
            kg = kx_ref[0, :, p * HEAD_DIM:(p + 1) * HEAD_DIM]
            s = lax.dot_general(stacked_q(p), kg, _NT, preferred_element_type=F32)
            store(p, _softmax_sink_pv(s, sink_col(p), vx_ref[0]))

    if ctx_queries:
        pl.when(j < N_LAT_BLK)(latent)
        pl.when(j >= N_LAT_BLK)(context)
    else:
        latent()


def _attn_a(qkv, sink, ctx_queries):
    b = qkv.shape[0]
    nqb = N_ALL_BLK if ctx_queries else N_LAT_BLK
    last = N_LAT_BLK - 1

    def blk(colblk, fn):
        return pl.BlockSpec((1, BLK, LANES), lambda i, j: (i, fn(j), colblk))

    prev = lambda j: jnp.clip(j - 1, 0, last)
    cur = lambda j: jnp.minimum(j, last)
    nxt = lambda j: jnp.clip(j + 1, 0, last)
    ctx_spec = lambda colblk: pl.BlockSpec((1, CTX_LEN, LANES), lambda i, j: (i, SEQ // CTX_LEN, colblk))
    return pl.pallas_call(
        functools.partial(_a_kernel, ctx_queries=ctx_queries),
        out_shape=jax.ShapeDtypeStruct((b, nqb * BLK, 2 * LANES), BF16),
        grid=(b, nqb),
        in_specs=[
            pl.BlockSpec(memory_space=pltpu.SMEM),
            pl.BlockSpec((1, BLK, 2 * LANES), lambda i, j: (i, j, 0)),
            blk(G_AK, prev), blk(G_AK, cur), blk(G_AK, nxt),
            blk(G_AV, prev), blk(G_AV, cur), blk(G_AV, nxt),
            ctx_spec(G_AK), ctx_spec(G_AV),
        ],
        out_specs=pl.BlockSpec((1, BLK, 2 * LANES), lambda i, j: (i, j, 0)),
        compiler_params=_params(("parallel", "parallel")),
        name="attn_window",
    )(sink, qkv, qkv, qkv, qkv, qkv, qkv, qkv, qkv, qkv)


def _b_kernel(lam_ref, subg_ref, q_ref, kt_ref, v_ref, o_ref, *, ctx_queries, lambda_init):
    j = pl.program_id(2)
    lv = lam_ref[...]
    lam = (jnp.exp(jnp.sum(lv[0:1] * lv[1:2], axis=1, keepdims=True))
           - jnp.exp(jnp.sum(lv[2:3] * lv[3:4], axis=1, keepdims=True)) + lambda_init)
    q = q_ref[0]
    lane = lax.broadcasted_iota(jnp.int32, (1, LANES), 1)

    def run(k0, nk):
        outs = []
        for hh in range(2):
            probs = []
            for sub in range(2):
                r0 = hh * HEAD_DIM + sub * B_QK_DIM
                qs = jnp.where((lane >= r0) & (lane < r0 + B_QK_DIM), q, jnp.zeros_like(q))
                s = jnp.dot(qs, kt_ref[0, :, k0:k0 + nk], preferred_element_type=F32)
                e = jnp.exp(s - jnp.max(s, axis=1, keepdims=True))
                probs.append(e * (1.0 / jnp.sum(e, axis=1, keepdims=True)))
            p = (probs[0] - lam * probs[1]).astype(BF16)
            o = jnp.dot(p, v_ref[0, k0:k0 + nk, :], preferred_element_type=F32)
            o = o[:, hh * HEAD_DIM:(hh + 1) * HEAD_DIM]
            outs.append(_rms(o) * subg_ref[...] * (1.0 - lambda_init))
        o_ref[0] = jnp.concatenate(outs, axis=1).astype(BF16)

    if ctx_queries:
        pl.when(j < N_LAT_BLK)(lambda: run(0, T_ALL))
        pl.when(j >= N_LAT_BLK)(lambda: run(SEQ, CTX_LEN))
    else:
        run(0, T_ALL)


def _attn_b(qkv, kt, lam_vecs, subg, ctx_queries, lambda_init):
    b = qkv.shape[0]
    nqb = N_ALL_BLK if ctx_queries else N_LAT_BLK
    return pl.pallas_call(
        functools.partial(_b_kernel, ctx_queries=ctx_queries, lambda_init=lambda_init),
        out_shape=jax.ShapeDtypeStruct((b, nqb * BLK, 2 * LANES), BF16),
        grid=(b, 2, nqb),
        in_specs=[
            pl.BlockSpec((4, B_QK_DIM), lambda i, p, j: (0, 0)),
            pl.BlockSpec((1, HEAD_DIM), lambda i, p, j: (0, 0)),
            pl.BlockSpec((1, BLK, LANES), lambda i, p, j: (i, j, G_BQ + p)),
            pl.BlockSpec((1, LANES, T_ALL), lambda i, p, j: (i, p, 0)),
            pl.BlockSpec((1, T_ALL, LANES), lambda i, p, j: (i, 0, G_BV + p)),
        ],
        out_specs=pl.BlockSpec((1, BLK, LANES), lambda i, p, j: (i, j, p)),
        compiler_params=_params(("parallel", "parallel", "parallel")),
        name="attn_diff",
    )(lam_vecs, subg, qkv, kt, qkv)


def _c_kernel(dec_ref, gn_ref, q_ref, kt_ref, v_ref, g_ref, o_ref, of_ref, ob_ref, tab_ref, st_ref,
              *, ctx_out):
    n_ctx = CTX_LEN // BLK
    lane = lax.broadcasted_iota(jnp.int32, (1, LANES), 1)
    rowi = lax.broadcasted_iota(jnp.int32, (BLK, 1), 0)
    head0_l = lane < HEAD_DIM
    head0_r = rowi < HEAD_DIM
    rel = (lax.broadcasted_iota(jnp.int32, (BLK, BLK), 0)
           - lax.broadcasted_iota(jnp.int32, (BLK, BLK), 1)).astype(F32)
    pos_r = rowi.astype(F32)
    pos_l = lane.astype(F32)

    def log_gamma(col):
        x = dec_ref[0, :, col:col + 1]
        return jnp.minimum(x, 0.0) - jnp.log1p(jnp.exp(-jnp.abs(x)))

    lg = [log_gamma(n) for n in range(4)]
    chunk_decay = []
    for d in range(2):
        lg_l = jnp.where(head0_l, lg[2 * d], lg[2 * d + 1])
        lg_r = jnp.where(head0_r, lg[2 * d], lg[2 * d + 1])
        for hh in range(2):
            if d == 0:
                tab_ref[hh] = jnp.where(rel >= 0, jnp.exp(rel * lg[hh]), 0.0)
            else:
                tab_ref[2 + hh] = jnp.where(rel <= 0, jnp.exp(-rel * lg[2 + hh]), 0.0)
        if d == 0:
            tab_ref[4] = jnp.exp((pos_r + 1.0) * lg_l)
            tab_ref[5] = jnp.exp((BLK - 1.0 - pos_l) * lg_r)
        else:
            tab_ref[6] = jnp.exp((BLK - pos_r) * lg_l)
            tab_ref[7] = jnp.exp(pos_l * lg_r)
        chunk_decay.append(jnp.exp(BLK * lg_l))
    st_ref[...] = jnp.zeros_like(st_ref)

    def chunk(c, d):
        r0 = pl.multiple_of(c * BLK, BLK)
        q = q_ref[0, pl.ds(r0, BLK), :]
        kt = kt_ref[0, c]
        v = v_ref[0, pl.ds(r0, BLK), :]
        zq, zv = jnp.zeros_like(q), jnp.zeros_like(v)
        in0 = jnp.dot(jnp.where(head0_l, q, zq), kt, preferred_element_type=F32) * tab_ref[2 * d]
        in1 = jnp.dot(jnp.where(head0_l, zq, q), kt, preferred_element_type=F32) * tab_ref[2 * d + 1]
        inner = jnp.concatenate([in0, in1], axis=1).astype(BF16)
        vv = jnp.concatenate([jnp.where(head0_l, v, zv), jnp.where(head0_l, zv, v)], axis=0)
        state = st_ref[d]
        o = jnp.dot(inner, vv, preferred_element_type=F32)
        o += jnp.dot((q.astype(F32) * tab_ref[4 + 2 * d]).astype(BF16), state.astype(BF16),
                     preferred_element_type=F32)
        kv = jnp.dot((kt.astype(F32) * tab_ref[5 + 2 * d]).astype(BF16), v, preferred_element_type=F32)
        st_ref[d] = state * chunk_decay[d] + jnp.where(head0_r == head0_l, kv, 0.0)
        return r0, o

    def step(i, carry):
        r0, o = chunk(jnp.where(i < n_ctx, N_LAT_BLK + i, i - n_ctx), 0)
        of_ref[pl.ds(r0, BLK), :] = o
        r0, o = chunk(N_ALL_BLK - 1 - i, 1)
        ob_ref[pl.ds(r0, BLK), :] = o
        return carry

    lax.fori_loop(0, N_ALL_BLK, step, 0, unroll=2)

    def finish(c, carry):
        r0 = pl.multiple_of(c * BLK, BLK)
        tot = of_ref[pl.ds(r0, BLK), :] + ob_ref[pl.ds(r0, BLK), :]
        sq = tot * tot
        ms0 = jnp.sum(jnp.where(head0_l, sq, 0.0), axis=1, keepdims=True) * (1.0 / HEAD_DIM)
        ms1 = jnp.sum(jnp.where(head0_l, 0.0, sq), axis=1, keepdims=True) * (1.0 / HEAD_DIM)
        inv = jnp.where(head0_l, lax.rsqrt(ms0 + EPS), lax.rsqrt(ms1 + EPS))
        o_ref[0, pl.ds(r0, BLK), :] = (tot * inv * gn_ref[...] * g_ref[0, pl.ds(r0, BLK), :]).astype(BF16)
        return carry

    lax.fori_loop(0, N_ALL_BLK if ctx_out else N_LAT_BLK, finish, 0)


def _retention(qkv, kt, gate, dec, gn, ctx_out):
    b = qkv.shape[0]
    t_out = T_ALL if ctx_out else SEQ
    col = lambda g: pl.BlockSpec((1, T_ALL, LANES), lambda i, p: (i, 0, g + p))
    return pl.pallas_call(
        functools.partial(_c_kernel, ctx_out=ctx_out),
        out_shape=jax.ShapeDtypeStruct((b, t_out, 2 * LANES), BF16),
        grid=(b, 2),
        in_specs=[
            pl.BlockSpec((1, 1, 4), lambda i, p: (p, 0, 0)),
            pl.BlockSpec((1, LANES), lambda i, p: (0, 0)),
            col(G_CQ),
            pl.BlockSpec((1, N_ALL_BLK, LANES, BLK), lambda i, p: (i, 0, p, 0)),
            col(G_CV),
            pl.BlockSpec((1, T_ALL, LANES), lambda i, p: (i, 0, p)),
        ],
        out_specs=pl.BlockSpec((1, t_out, LANES), lambda i, p: (i, 0, p)),
        scratch_shapes=[pltpu.VMEM((T_ALL, LANES), F32), pltpu.VMEM((T_ALL, LANES), F32),
                        pltpu.VMEM((8, BLK, BLK), F32), pltpu.VMEM((2, LANES, LANES), F32)],
        compiler_params=_params(("parallel", "parallel")),
        name="retention",
    )(dec, gn, qkv, kt, qkv, gate)


def _d_kernel(q_ref, k_ref, v_ref, bias_ref, o_ref, *, ctx_queries):
    rows = SEQ // GRID_W
    slab = NA_ROWS * GRID_W
    kx = k_ref[0, SEQ:T_ALL, :]
    vx = v_ref[0, SEQ:T_ALL, :]
    head0 = lax.broadcasted_iota(jnp.int32, (1, LANES), 1) < HEAD_DIM

    def attend(q, parts):
        n = q.shape[0]
        zq = jnp.zeros_like(q)
        qq = jnp.concatenate([jnp.where(head0, q, zq), jnp.where(head0, zq, q)], axis=0)
        scores = []
        for k, _, bias in parts:
            s = lax.dot_general(qq, k, _NT, preferred_element_type=F32)
            scores.append(s if bias is None else s + bias)
        m = functools.reduce(jnp.maximum, [jnp.max(s, axis=1, keepdims=True) for s in scores])
        es = [jnp.exp(s - m) for s in scores]
        den = sum(jnp.sum(e, axis=1, keepdims=True) for e in es)
        o = sum(jnp.dot(e.astype(BF16), v, preferred_element_type=F32)
                for e, (_, v, _) in zip(es, parts)) / den
        return jnp.where(head0, o[:n], o[n:]).astype(BF16)

    def row_step(r, carry):
        start = pl.multiple_of(jnp.clip(r - NA_ROWS // 2, 0, rows - NA_ROWS) * GRID_W, GRID_W)
        dr0 = jnp.where(r <= NA_ROWS // 2, NA_ROWS - 1 - r,
                        jnp.where(r >= rows - NA_ROWS // 2, rows - 1 - r, NA_ROWS // 2 - 1))
        q0 = pl.multiple_of(r * GRID_W, GRID_W)
        q = q_ref[0, pl.ds(q0, GRID_W), :]
        ks = k_ref[0, pl.ds(start, slab), :]
        vs = v_ref[0, pl.ds(start, slab), :]
        o_ref[0, pl.ds(q0, GRID_W), :] = attend(q, [(ks, vs, bias_ref[0, dr0]), (kx, vx, None)])
        return carry

    lax.fori_loop(0, rows, row_step, 0, unroll=4)
    if ctx_queries:
        o_ref[0, SEQ:T_ALL, :] = attend(q_ref[0, SEQ:T_ALL, :], [(kx, vx, None)])


def _attn_d(qkv, bias, ctx_queries):
    b = qkv.shape[0]
    t_out = T_ALL if ctx_queries else SEQ
    col = lambda g: pl.BlockSpec((1, T_ALL, LANES), lambda i, p: (i, 0, g + p))
    return pl.pallas_call(
        functools.partial(_d_kernel, ctx_queries=ctx_queries),
        out_shape=jax.ShapeDtypeStruct((b, t_out, 2 * LANES), BF16),
        grid=(b, 2),
        in_specs=[
            col(G_DQ), col(G_DK), col(G_DV),
            pl.BlockSpec((1, NA_ROWS, 2 * GRID_W, NA_ROWS * GRID_W), lambda i, p: (p, 0, 0, 0)),
        ],
        out_specs=pl.BlockSpec((1, t_out, LANES), lambda i, p: (i, 0, p)),
        compiler_params=_params(("parallel", "parallel")),
        name="attn_nbr",
    )(qkv, qkv, qkv, bias)


def _outmlp_kernel(x_ref, a_ref, b_ref, c_ref, d_ref, mod_ref, g2_ref, wo_ref, w1_ref, w2_ref, o_ref):
    gw = 2 * LANES
    mix = None
    for n, r in enumerate((a_ref, b_ref, c_ref, d_ref)):
        part = jnp.dot(r[0], wo_ref[n * gw:(n + 1) * gw, :], preferred_element_type=F32)
        mix = part if mix is None else mix + part
    x1 = x_ref[0] + mod_ref[0, 2:3, :] * mix
    h2 = (_rms(x1) * g2_ref[...] * (1.0 + mod_ref[0, 4:5, :]) + mod_ref[0, 3:4, :]).astype(BF16)
    acc = None
    for n in range(D_FF // D_MODEL):
        cols = slice(n * D_MODEL, (n + 1) * D_MODEL)
        hid = jnp.maximum(jnp.dot(h2, w1_ref[:, cols], preferred_element_type=F32), 0.0)
        part = jnp.dot((hid * hid).astype(BF16), w2_ref[cols, :], preferred_element_type=F32)
        acc = part if acc is None else acc + part
    o_ref[0] = x1 + mod_ref[0, 5:6, :] * acc


def _out_mlp(xs, outs, mod, g2, w_out, w1, w2, ctx_tokens, tm=256):
    b = xs.shape[0]
    t_out = T_ALL if ctx_tokens else SEQ
    nt = t_out // tm
    ctx_row = mod.shape[0] - 1
    ctx_tile = SEQ // tm

    def mod_map(i, t):
        return (jnp.where(t >= ctx_tile, ctx_row, i), 0, 0)

    tile = lambda w: pl.BlockSpec((1, tm, w), lambda i, t: (i, t, 0))
    whole = lambda s: pl.BlockSpec(s, lambda i, t: (0, 0))
    return pl.pallas_call(
        _outmlp_kernel,
        out_shape=jax.ShapeDtypeStruct((b, t_out, D_MODEL), F32),
        grid=(b, nt),
        in_specs=[
            tile(D_MODEL), tile(2 * LANES), tile(2 * LANES), tile(2 * LANES), tile(2 * LANES),
            pl.BlockSpec((1, 6, D_MODEL), mod_map),
            whole((1, D_MODEL)), whole((D_MODEL, D_MODEL)), whole((D_MODEL, D_FF)), whole((D_FF, D_MODEL)),
        ],
        out_specs=tile(D_MODEL),
        compiler_params=_params(("parallel", "parallel")),
        name="out_mlp",
    )(xs, *outs, mod, g2, w_out, w1, w2)


def _rope_tables():
    pos = np.arange(SEQ)
    rows, cols = (pos // GRID_W).astype(np.float32), (pos % GRID_W).astype(np.float32)

    def pattern(half):
        freqs = (np.float32(ROPE_BASE) ** (-np.arange(half, dtype=np.float32) / np.float32(half))).astype(np.float32)
        ang = [(p[:, None] * freqs[None, :]).astype(np.float32) for p in (rows, cols)]
        cos = np.concatenate([np.cos(a) for a in (ang[0], ang[0], ang[1], ang[1])], axis=1)
        sin = np.concatenate([-np.sin(ang[0]), np.sin(ang[0]), -np.sin(ang[1]), np.sin(ang[1])], axis=1)
        reps = LANES // (4 * half)
        cos = np.concatenate([np.tile(cos, (1, reps)), np.ones((CTX_LEN, LANES))], axis=0)
        sin = np.concatenate([np.tile(sin, (1, reps)), np.zeros((CTX_LEN, LANES))], axis=0)
        return jnp.asarray(cos, F32), jnp.asarray(sin, F32)

    ca, sa = pattern(HEAD_DIM // 4)
    cb, sb = pattern(B_QK_DIM // 4)
    return ca, sa, cb, sb


def _gain_row(aq, ak, bq, bk, dq, dk):
    one = jnp.ones((LANES,), F32)
    t = lambda g, s=1.0: jnp.tile(g.astype(F32), LANES // g.shape[0]) * s
    groups = [one] * N_GROUPS
    groups[G_AQ] = groups[G_AQ + 1] = t(aq, HEAD_DIM ** -0.5)
    groups[G_AK] = t(ak)
    groups[G_BQ] = groups[G_BQ + 1] = t(bq, B_QK_DIM ** -0.5)
    groups[G_BK] = groups[G_BK + 1] = t(bk)
    groups[G_CK] = groups[G_CK + 1] = one * HEAD_DIM ** -0.5
    groups[G_DQ] = groups[G_DQ + 1] = t(dq, HEAD_DIM ** -0.5)
    groups[G_DK] = groups[G_DK + 1] = t(dk)
    return jnp.concatenate(groups)[None, :]


def _nbr_bias(rpb):
    qc = np.arange(GRID_W)[:, None]
    kc = np.arange(GRID_W)[None, :]
    cstart = np.clip(qc - NA_COLS // 2, 0, GRID_W - NA_COLS)
    valid = (kc >= cstart) & (kc < cstart + NA_COLS)
    pad = GRID_W - NA_COLS
    rp = jnp.pad(rpb.astype(F32), ((0, 0), (0, 0), (pad, pad)))
    toep = jnp.stack([rp[:, :, GRID_W - 1 - q:2 * GRID_W - 1 - q] for q in range(GRID_W)], axis=2)
    toep = jnp.where(valid[None, None], toep, NEG_INF)
    b = jnp.stack([toep[:, kr:kr + NA_ROWS] for kr in range(NA_ROWS)], axis=3)
    b = b.reshape(rpb.shape[0] // 2, 2, NA_ROWS, GRID_W, NA_ROWS * GRID_W)
    return b.transpose(0, 2, 1, 3, 4).reshape(rpb.shape[0] // 2, NA_ROWS, 2 * GRID_W, NA_ROWS * GRID_W)


def kernel(x, c, ctx, c_ctx, w_mod, b_mod, norm1_g, norm2_g, w_in, w_out, a_qnorm_g, a_knorm_g, a_sink, b_qnorm_g, b_knorm_g, b_lambda_q1, b_lambda_k1, b_lambda_q2, b_lambda_k2, b_subln_g, c_decay_fwd, c_decay_bwd, c_gn_g, d_qnorm_g, d_knorm_g, d_rpb, w_mlp1, w_mlp2):
    depth = w_mod.shape[0]
    b = x.shape[0]
    assert b + 1 <= 8
    cc = jnp.concatenate([c, c_ctx[None, :], jnp.zeros((8 - b - 1, D_MODEL), F32)], axis=0)
    mods = _modulation(cc, w_mod, b_mod)[:, :b + 1].reshape(depth, b + 1, 6, D_MODEL)
    tabs = _rope_tables()
    xs = jnp.concatenate([x, ctx], axis=1)

    for layer in range(depth):
        need_ctx = layer < depth - 1
        lambda_init = 0.8 - 0.6 * math.exp(-0.3 * layer)
        gain = _gain_row(a_qnorm_g[layer], a_knorm_g[layer], b_qnorm_g[layer], b_knorm_g[layer],
                         d_qnorm_g[layer], d_knorm_g[layer])
        qkv, gate = _in_proj(xs, mods[layer], norm1_g[layer][None, :], w_in[layer].astype(BF16), gain, tabs)

        out_a = _attn_a(qkv, a_sink[layer].astype(F32), need_ctx)
        kt = jnp.swapaxes(qkv[:, :, G_BK * LANES:(G_BK + 2) * LANES], 1, 2)
        lam_vecs = jnp.stack([b_lambda_q1[layer], b_lambda_k1[layer],
                              b_lambda_q2[layer], b_lambda_k2[layer]]).astype(F32)
        out_b = _attn_b(qkv, kt, lam_vecs, b_subln_g[layer][None, :], need_ctx, lambda_init)
        dec = jnp.stack([c_decay_fwd[layer].reshape(2, 2), c_decay_bwd[layer].reshape(2, 2)],
                        axis=1).reshape(2, 1, 4).astype(F32)
        kt_c = qkv[:, :, G_CK * LANES:(G_CK + 2) * LANES].reshape(b, N_ALL_BLK, BLK, 2 * LANES)
        kt_c = kt_c.transpose(0, 1, 3, 2)
        out_c = _retention(qkv, kt_c, gate, dec, jnp.tile(c_gn_g[layer], 2)[None, :], need_ctx)
        out_d = _attn_d(qkv, _nbr_bias(d_rpb[layer]), need_ctx)

        xs = _out_mlp(xs, (out_a, out_b, out_c, out_d), mods[layer], norm2_g[layer][None, :],
                      w_out[layer].astype(BF16), w_mlp1[layer].astype(BF16), w_mlp2[layer].astype(BF16),
                      need_ctx)
    return xs
```

```python
import functools
import math

import numpy as np
import jax
import jax.numpy as jnp
from jax import lax
from jax.experimental import pallas as pl
from jax.experimental.pallas import tpu as pltpu

F32 = jnp.float32
BF16 = jnp.bfloat16

D_MODEL = 1024
SEQ = 4096
CTX_LEN = 256
T_ALL = SEQ + CTX_LEN
GRID_W = 64
HEAD_DIM = 64
B_QK_DIM = 32
BLK = 128
N_LAT_BLK = SEQ // BLK
N_ALL_BLK = T_ALL // BLK
B_KEY_CHUNK = 256
B_ROW_SLAB = 32
NA_ROWS = 8
NA_COLS = 16
D_FF = 4 * D_MODEL
IN_WIDTH = 3072
ROPE_BASE = 10000.0
EPS = 1e-6
NEG_INF = -1e30
LANES = 128
VMEM_LIMIT = 52 * 1024 * 1024

G_AQ, G_AK, G_AV = 0, 2, 3
G_BQ, G_BK, G_BV = 4, 6, 8
G_CQ, G_CK, G_CV, G_CG = 10, 12, 14, 16
G_DQ, G_DK, G_DV = 18, 20, 22
N_GROUPS = IN_WIDTH // LANES

_NT = (((1,), (1,)), ((), ()))
_TN = (((0,), (0,)), ((), ()))


def _params(sem):
    return pltpu.CompilerParams(dimension_semantics=sem, vmem_limit_bytes=VMEM_LIMIT)


def _rms(x):
    return x * lax.rsqrt(jnp.mean(x * x, axis=-1, keepdims=True) + EPS)


def _mod_kernel(c_ref, w_ref, b_ref, o_ref):
    c = c_ref[...]
    s = c / (1.0 + jnp.exp(-c))
    s_hi = s.astype(BF16)
    s_lo = (s - s_hi.astype(F32)).astype(BF16)
    w = w_ref[0]
    w_hi = w.astype(BF16)
    w_lo = (w - w_hi.astype(F32)).astype(BF16)
    acc = jnp.dot(s_hi, w_hi, preferred_element_type=F32)
    acc += jnp.dot(s_lo, w_hi, preferred_element_type=F32)
    acc += jnp.dot(s_hi, w_lo, preferred_element_type=F32)
    o_ref[0] = acc + b_ref[0]


def _modulation(cc, w_mod, b_mod):
    depth = w_mod.shape[0]
    n_col = 6 * D_MODEL // 1024
    return pl.pallas_call(
        _mod_kernel,
        out_shape=jax.ShapeDtypeStruct((depth, 8, 6 * D_MODEL), F32),
        grid=(depth, n_col),
        in_specs=[
            pl.BlockSpec((8, D_MODEL), lambda l, j: (0, 0)),
            pl.BlockSpec((1, D_MODEL, 1024), lambda l, j: (l, 0, j)),
            pl.BlockSpec((1, 1, 1024), lambda l, j: (l, 0, j)),
        ],
        out_specs=pl.BlockSpec((1, 8, 1024), lambda l, j: (l, 0, j)),
        compiler_params=_params(("parallel", "parallel")),
        name="adaln_mod",
    )(cc, w_mod, b_mod.reshape(depth, 1, 6 * D_MODEL))


def _seg_inv_rms(v, seg):
    sq = v * v
    hi = sq.astype(BF16)
    lo = (sq - hi.astype(F32)).astype(BF16)
    r = lax.broadcasted_iota(jnp.int32, (LANES, LANES), 0) // seg
    c = lax.broadcasted_iota(jnp.int32, (LANES, LANES), 1) // seg
    bd = jnp.where(r == c, 1.0, 0.0).astype(BF16)
    ssum = jnp.dot(hi, bd, preferred_element_type=F32) + jnp.dot(lo, bd, preferred_element_type=F32)
    return lax.rsqrt(ssum * (1.0 / seg) + EPS)


def _rotate(v, cos, sin_signed, half):
    up = pltpu.roll(v, LANES - half, 1)
    dn = pltpu.roll(v, half, 1)
    lane = lax.broadcasted_iota(jnp.int32, v.shape, 1)
    partner = jnp.where(lane % (2 * half) < half, up, dn)
    return v * cos + partner * sin_signed


def _inproj_kernel(x_ref, mod_ref, g1_ref, w_ref, gain_ref, ca_ref, sa_ref, cb_ref, sb_ref,
                   qkv_ref, gate_ref):
    x = x_ref[0]
    sh = mod_ref[0, 0:1, :]
    sc = mod_ref[0, 1:2, :]
    h = (_rms(x) * g1_ref[...] * (1.0 + sc) + sh).astype(BF16)
    acc = jnp.dot(h, w_ref[...], preferred_element_type=F32)
    for g in range(N_GROUPS):
        cols = slice(g * LANES, (g + 1) * LANES)
        v = acc[:, cols]
        gain = gain_ref[:, cols]
        if g in (G_AQ, G_AQ + 1, G_AK):
            v = v * _seg_inv_rms(v, HEAD_DIM) * gain
            v = _rotate(v, ca_ref[...], sa_ref[...], HEAD_DIM // 4)
        elif g in (G_BQ, G_BQ + 1, G_BK, G_BK + 1):
            v = v * _seg_inv_rms(v, B_QK_DIM) * gain
            v = _rotate(v, cb_ref[...], sb_ref[...], B_QK_DIM // 4)
        elif g in (G_DQ, G_DQ + 1, G_DK, G_DK + 1):
            v = v * _seg_inv_rms(v, HEAD_DIM) * gain
        elif g in (G_CK, G_CK + 1):
            v = v * gain
        elif g in (G_CG, G_CG + 1):
            gcols = slice((g - G_CG) * LANES, (g - G_CG + 1) * LANES)
            gate_ref[0, :, gcols] = v / (1.0 + jnp.exp(-v))
        qkv_ref[0, :, cols] = v.astype(BF16)


def _in_proj(xs, mod, g1, w_in, gain, tabs, tm=256):
    b = xs.shape[0]
    nt = T_ALL // tm
    ctx_row = mod.shape[0] - 1

    def mod_map(t, i):
        return (jnp.where(t == nt - 1, ctx_row, i), 0, 0)

    tab_spec = pl.BlockSpec((tm, LANES), lambda t, i: (t, 0))
    return pl.pallas_call(
        _inproj_kernel,
        out_shape=(jax.ShapeDtypeStruct((b, T_ALL, IN_WIDTH), BF16),
                   jax.ShapeDtypeStruct((b, T_ALL, 2 * LANES), F32)),
        grid=(nt, b),
        in_specs=[
            pl.BlockSpec((1, tm, D_MODEL), lambda t, i: (i, t, 0)),
            pl.BlockSpec((1, 6, D_MODEL), mod_map),
            pl.BlockSpec((1, D_MODEL), lambda t, i: (0, 0)),
            pl.BlockSpec((D_MODEL, IN_WIDTH), lambda t, i: (0, 0)),
            pl.BlockSpec((1, IN_WIDTH), lambda t, i: (0, 0)),
            tab_spec, tab_spec, tab_spec, tab_spec,
        ],
        out_specs=(pl.BlockSpec((1, tm, IN_WIDTH), lambda t, i: (i, t, 0)),
                   pl.BlockSpec((1, tm, 2 * LANES), lambda t, i: (i, t, 0))),
        compiler_params=_params(("parallel", "parallel")),
        name="in_proj",
    )(xs, mod, g1, w_in, gain, *tabs)


def _softmax_sink_pv(s, sink_col, v):
    m = jnp.maximum(jnp.max(s, axis=1, keepdims=True), sink_col)
    e = jnp.exp(s - m)
    den = jnp.sum(e, axis=1, keepdims=True) + jnp.exp(sink_col - m)
    o = jnp.dot(e.astype(BF16), v, preferred_element_type=F32)
    return o / den


def _a_kernel(sink_ref, q_ref, kp_ref, kc_ref, kn_ref, vp_ref, vc_ref, vn_ref, kx_ref, vx_ref,
              o_ref, *, ctx_queries):
    j = pl.program_id(1)
    nq = 2 * BLK
    row = lax.broadcasted_iota(jnp.int32, (nq, 1), 0)

    def sink_col(p):
        return jnp.where(row < BLK, sink_ref[2 * p], sink_ref[2 * p + 1])

    def stacked_q(p):
        qp = q_ref[0, :, p * LANES:(p + 1) * LANES]
        return jnp.concatenate([qp[:, :HEAD_DIM], qp[:, HEAD_DIM:]], axis=0)

    def store(p, o):
        lo = p * HEAD_DIM
        o = o[:, lo:lo + HEAD_DIM].astype(BF16)
        o_ref[0, :, p * LANES:p * LANES + HEAD_DIM] = o[:BLK]
        o_ref[0, :, p * LANES + HEAD_DIM:(p + 1) * LANES] = o[BLK:]

    def latent():
        k_all = jnp.concatenate([kx_ref[0], kp_ref[0], kc_ref[0], kn_ref[0]], axis=0)
        v_all = jnp.concatenate([vx_ref[0], vp_ref[0], vc_ref[0], vn_ref[0]], axis=0)
        nk = CTX_LEN + 3 * BLK
        qi = lax.broadcasted_iota(jnp.int32, (nq, nk), 0) % BLK
        col = lax.broadcasted_iota(jnp.int32, (nq, nk), 1)
        c = col - CTX_LEN
        c_lo = jnp.where(j > 0, 0, BLK)
        c_hi = jnp.where(j < N_LAT_BLK - 1, 3 * BLK, 2 * BLK)
        valid = (col < CTX_LEN) | ((jnp.abs(c - BLK - qi) <= BLK) & (c >= c_lo) & (c < c_hi))
        for p in range(2):
            kg = k_all[:, p * HEAD_DIM:(p + 1) * HEAD_DIM]
            s = lax.dot_general(stacked_q(p), kg, _NT, preferred_element_type=F32)
            s = jnp.where(valid, s, NEG_INF)
            store(p, _softmax_sink_pv(s, sink_col(p), v_all))

    def context():
        for p in range(2):
            kg = kx_ref[0, :, p * HEAD_DIM:(p + 1) * HEAD_DIM]
            s = lax.dot_general(stacked_q(p), kg, _NT, preferred_element_type=F32)
            store(p, _softmax_sink_pv(s, sink_col(p), vx_ref[0]))

    if ctx_queries:
        pl.when(j < N_LAT_BLK)(latent)
        pl.when(j >= N_LAT_BLK)(context)
    else:
        latent()


def _attn_a(qkv, sink, ctx_queries):
    b = qkv.shape[0]
    nqb = N_ALL_BLK if ctx_queries else N_LAT_BLK
    last = N_LAT_BLK - 1

    def blk(colblk, fn):
        return pl.BlockSpec((1, BLK, LANES), lambda i, j: (i, fn(j), colblk))

    prev = lambda j: jnp.clip(j - 1, 0, last)
    cur = lambda j: jnp.minimum(j, last)
    nxt = lambda j: jnp.clip(j + 1, 0, last)
    ctx_spec = lambda colblk: pl.BlockSpec((1, CTX_LEN, LANES), lambda i, j: (i, SEQ // CTX_LEN, colblk))
    return pl.pallas_call(
        functools.partial(_a_kernel, ctx_queries=ctx_queries),
        out_shape=jax.ShapeDtypeStruct((b, nqb * BLK, 2 * LANES), BF16),
        grid=(b, nqb),
        in_specs=[
            pl.BlockSpec(memory_space=pltpu.SMEM),
            pl.BlockSpec((1, BLK, 2 * LANES), lambda i, j: (i, j, 0)),
            blk(G_AK, prev), blk(G_AK, cur), blk(G_AK, nxt),
            blk(G_AV, prev), blk(G_AV, cur), blk(G_AV, nxt),
            ctx_spec(G_AK), ctx_spec(G_AV),
        ],
        out_specs=pl.BlockSpec((1, BLK, 2 * LANES), lambda i, j: (i, j, 0)),
        compiler_params=_params(("parallel", "parallel")),
        name="attn_window",
    )(sink, qkv, qkv, qkv, qkv, qkv, qkv, qkv, qkv, qkv)


def _b_kernel(lam_ref, subg_ref, q_ref, kt_ref, v_ref, o_ref, s_ref, m_ref, mx_ref, l_ref, acc_ref, qq_ref,
              *, ctx_queries, lambda_init):
    lv = lam_ref[...]
    lam = (jnp.exp(jnp.sum(lv[0:1] * lv[1:2], axis=1, keepdims=True))
           - jnp.exp(jnp.sum(lv[2:3] * lv[3:4], axis=1, keepdims=True)) + lambda_init)
    lane = lax.broadcasted_iota(jnp.int32, (1, LANES), 1)
    rows = 4 * BLK

    def chunks_of(k0, nk):
        return [(c, min(B_KEY_CHUNK, k0 + nk - c)) for c in range(k0, k0 + nk, B_KEY_CHUNK)]

    def tiles(t):
        return [t[:, c:c + LANES] for c in range(0, t.shape[1], LANES)]

    def stack_q(blk):
        q = q_ref[0, pl.ds(pl.multiple_of(blk * BLK, BLK), BLK), :]
        for n in range(4):
            mine = (lane >= n * B_QK_DIM) & (lane < (n + 1) * B_QK_DIM)
            qq_ref[n * BLK:(n + 1) * BLK] = jnp.where(mine, q, jnp.zeros_like(q))

    def score_chunk(slot, c0, cn, first):
        s = jnp.dot(qq_ref[...], kt_ref[0, :, c0:c0 + cn], preferred_element_type=F32)
        s_ref[slot, :, c0:c0 + cn] = s
        for r in range(0, rows, B_ROW_SLAB):
            t = functools.reduce(jnp.maximum, tiles(s[r:r + B_ROW_SLAB]))
            mx_ref[r:r + B_ROW_SLAB] = t if first else jnp.maximum(mx_ref[r:r + B_ROW_SLAB], t)

    def keep_max(slot):
        m_ref[slot] = jnp.broadcast_to(jnp.max(mx_ref[...], axis=1, keepdims=True), (rows, LANES))

    def prob_chunk(slot, c0, cn, first):
        parts = []
        for r in range(0, rows, B_ROW_SLAB):
            m = m_ref[slot, r:r + B_ROW_SLAB]
            es = [jnp.exp2(t - m) for t in tiles(s_ref[slot, r:r + B_ROW_SLAB, c0:c0 + cn])]
            t = functools.reduce(jnp.add, es)
            l_ref[r:r + B_ROW_SLAB] = t if first else l_ref[r:r + B_ROW_SLAB] + t
            parts.append(jnp.concatenate(es, axis=1).astype(BF16))
        part = jnp.dot(jnp.concatenate(parts, axis=0), v_ref[0, c0:c0 + cn, :],
                       preferred_element_type=F32)
        acc_ref[...] = part if first else acc_ref[...] + part

    def finish(blk):
        o = acc_ref[...] * (1.0 / jnp.sum(l_ref[...], axis=1, keepdims=True))
        outs = []
        for hh in range(2):
            r0 = 2 * hh * BLK
            oh = o[r0:r0 + BLK] - lam * o[r0 + BLK:r0 + 2 * BLK]
            oh = oh[:, hh * HEAD_DIM:(hh + 1) * HEAD_DIM]
            outs.append(_rms(oh) * subg_ref[...] * (1.0 - lambda_init))
        o_ref[0, pl.ds(pl.multiple_of(blk * BLK, BLK), BLK), :] = jnp.concatenate(outs, axis=1).astype(BF16)

    def sweep(score_blk, score_slot, prob_blk, prob_slot, chunks):
        if score_blk is not None:
            stack_q(score_blk)
        for n, (c0, cn) in enumerate(chunks):
            if score_blk is not None:
                score_chunk(score_slot, c0, cn, n == 0)
            if prob_blk is not None:
                prob_chunk(prob_slot, c0, cn, n == 0)
        if score_blk is not None:
            keep_max(score_slot)
        if prob_blk is not None:
            finish(prob_blk)

    all_keys = chunks_of(0, T_ALL)
    sweep(0, 0, None, None, all_keys)

    def body(n, carry):
        blk = 2 * n + 1
        sweep(blk, 1, blk - 1, 0, all_keys)
        sweep(blk + 1, 0, blk, 1, all_keys)
        return carry

    lax.fori_loop(0, (N_LAT_BLK - 2) // 2, body, 0)
    sweep(N_LAT_BLK - 1, 1, N_LAT_BLK - 2, 0, all_keys)
    sweep(None, None, N_LAT_BLK - 1, 1, all_keys)
    if ctx_queries:
        ctx_keys = chunks_of(SEQ, CTX_LEN)
        for blk in range(N_LAT_BLK, N_ALL_BLK):
            sweep(blk, 0, None, None, ctx_keys)
            sweep(None, None, blk, 0, ctx_keys)


def _attn_b(qkv, kt, lam_vecs, subg, ctx_queries, lambda_init):
    b = qkv.shape[0]
    t_out = T_ALL if ctx_queries else SEQ
    return pl.pallas_call(
        functools.partial(_b_kernel, ctx_queries=ctx_queries, lambda_init=lambda_init),
        out_shape=jax.ShapeDtypeStruct((b, t_out, 2 * LANES), BF16),
        grid=(b, 2),
        in_specs=[
            pl.BlockSpec((4, B_QK_DIM), lambda i, p: (0, 0)),
            pl.BlockSpec((1, HEAD_DIM), lambda i, p: (0, 0)),
            pl.BlockSpec((1, T_ALL, LANES), lambda i, p: (i, 0, G_BQ + p)),
            pl.BlockSpec((1, LANES, T_ALL), lambda i, p: (i, p, 0)),
            pl.BlockSpec((1, T_ALL, LANES), lambda i, p: (i, 0, G_BV + p)),
        ],
        out_specs=pl.BlockSpec((1, t_out, LANES), lambda i, p: (i, 0, p)),
        scratch_shapes=[pltpu.VMEM((2, 4 * BLK, T_ALL), F32), pltpu.VMEM((2, 4 * BLK, LANES), F32)]
        + [pltpu.VMEM((4 * BLK, LANES), F32)] * 3 + [pltpu.VMEM((4 * BLK, LANES), BF16)],
        compiler_params=_params(("parallel", "parallel")),
        name="attn_diff",
    )(lam_vecs, subg, qkv, kt, qkv)


def _c_kernel(dec_ref, gn_ref, q_ref, kt_ref, v_ref, g_ref, o_ref, of_ref, ob_ref, tab_ref, st_ref,
              *, ctx_out):
    n_ctx = CTX_LEN // BLK
    lane = lax.broadcasted_iota(jnp.int32, (1, LANES), 1)
    rowi = lax.broadcasted_iota(jnp.int32, (BLK, 1), 0)
    head0_l = lane < HEAD_DIM
    head0_r = rowi < HEAD_DIM
    rel = (lax.broadcasted_iota(jnp.int32, (BLK, BLK), 0)
           - lax.broadcasted_iota(jnp.int32, (BLK, BLK), 1)).astype(F32)
    pos_r = rowi.astype(F32)
    pos_l = lane.astype(F32)

    def log_gamma(col):
        x = dec_ref[0, :, col:col + 1]
        return jnp.minimum(x, 0.0) - jnp.log1p(jnp.exp(-jnp.abs(x)))

    lg = [log_gamma(n) for n in range(4)]
    chunk_decay = []
    for d in range(2):
        lg_l = jnp.where(head0_l, lg[2 * d], lg[2 * d + 1])
        lg_r = jnp.where(head0_r, lg[2 * d], lg[2 * d + 1])
        for hh in range(2):
            if d == 0:
                tab_ref[hh] = jnp.where(rel >= 0, jnp.exp(rel * lg[hh]), 0.0)
            else:
                tab_ref[2 + hh] = jnp.where(rel <= 0, jnp.exp(-rel * lg[2 + hh]), 0.0)
        if d == 0:
            tab_ref[4] = jnp.exp((pos_r + 1.0) * lg_l)
            tab_ref[5] = jnp.exp((BLK - 1.0 - pos_l) * lg_r)
        else:
            tab_ref[6] = jnp.exp((BLK - pos_r) * lg_l)
            tab_ref[7] = jnp.exp(pos_l * lg_r)
        chunk_decay.append(jnp.exp(BLK * lg_l))
    st_ref[...] = jnp.zeros_like(st_ref)

    def chunk(c, d):
        r0 = pl.multiple_of(c * BLK, BLK)
        q = q_ref[0, pl.ds(r0, BLK), :]
        kt = kt_ref[0, c]
        v = v_ref[0, pl.ds(r0, BLK), :]
        zq, zv = jnp.zeros_like(q), jnp.zeros_like(v)
        in0 = jnp.dot(jnp.where(head0_l, q, zq), kt, preferred_element_type=F32) * tab_ref[2 * d]
        in1 = jnp.dot(jnp.where(head0_l, zq, q), kt, preferred_element_type=F32) * tab_ref[2 * d + 1]
        inner = jnp.concatenate([in0, in1], axis=1).astype(BF16)
        vv = jnp.concatenate([jnp.where(head0_l, v, zv), jnp.where(head0_l, zv, v)], axis=0)
        state = st_ref[d]
        o = jnp.dot(inner, vv, preferred_element_type=F32)
        o += jnp.dot((q.astype(F32) * tab_ref[4 + 2 * d]).astype(BF16), state.astype(BF16),
                     preferred_element_type=F32)
        kv = jnp.dot((kt.astype(F32) * tab_ref[5 + 2 * d]).astype(BF16), v, preferred_element_type=F32)
        st_ref[d] = state * chunk_decay[d] + jnp.where(head0_r == head0_l, kv, 0.0)
        return r0, o

    def step(i, carry):
        r0, o = chunk(jnp.where(i < n_ctx, N_LAT_BLK + i, i - n_ctx), 0)
        of_ref[pl.ds(r0, BLK), :] = o
        r0, o = chunk(N_ALL_BLK - 1 - i, 1)
        ob_ref[pl.ds(r0, BLK), :] = o
        return carry

    lax.fori_loop(0, N_ALL_BLK, step, 0, unroll=2)

    def finish(c, carry):
        r0 = pl.multiple_of(c * BLK, BLK)
        tot = of_ref[pl.ds(r0, BLK), :] + ob_ref[pl.ds(r0, BLK), :]
        sq = tot * tot
        ms0 = jnp.sum(jnp.where(head0_l, sq, 0.0), axis=1, keepdims=True) * (1.0 / HEAD_DIM)
        ms1 = jnp.sum(jnp.where(head0_l, 0.0, sq), axis=1, keepdims=True) * (1.0 / HEAD_DIM)
        inv = jnp.where(head0_l, lax.rsqrt(ms0 + EPS), lax.rsqrt(ms1 + EPS))
        o_ref[0, pl.ds(r0, BLK), :] = (tot * inv * gn_ref[...] * g_ref[0, pl.ds(r0, BLK), :]).astype(BF16)
        return carry

    lax.fori_loop(0, N_ALL_BLK if ctx_out else N_LAT_BLK, finish, 0)


def _retention(qkv, kt, gate, dec, gn, ctx_out):
    b = qkv.shape[0]
    t_out = T_ALL if ctx_out else SEQ
    col = lambda g: pl.BlockSpec((1, T_ALL, LANES), lambda i, p: (i, 0, g + p))
    return pl.pallas_call(
        functools.partial(_c_kernel, ctx_out=ctx_out),
        out_shape=jax.ShapeDtypeStruct((b, t_out, 2 * LANES), BF16),
        grid=(b, 2),
        in_specs=[
            pl.BlockSpec((1, 1, 4), lambda i, p: (p, 0, 0)),
            pl.BlockSpec((1, LANES), lambda i, p: (0, 0)),
            col(G_CQ),
            pl.BlockSpec((1, N_ALL_BLK, LANES, BLK), lambda i, p: (i, 0, p, 0)),
            col(G_CV),
            pl.BlockSpec((1, T_ALL, LANES), lambda i, p: (i, 0, p)),
        ],
        out_specs=pl.BlockSpec((1, t_out, LANES), lambda i, p: (i, 0, p)),
        scratch_shapes=[pltpu.VMEM((T_ALL, LANES), F32), pltpu.VMEM((T_ALL, LANES), F32),
                        pltpu.VMEM((8, BLK, BLK), F32), pltpu.VMEM((2, LANES, LANES), F32)],
        compiler_params=_params(("parallel", "parallel")),
        name="retention",
    )(dec, gn, qkv, kt, qkv, gate)


def _d_kernel(q_ref, k_ref, v_ref, bias_ref, o_ref, *, ctx_queries):
    rows = SEQ // GRID_W
    slab = NA_ROWS * GRID_W
    kx = k_ref[0, SEQ:T_ALL, :]
    vx = v_ref[0, SEQ:T_ALL, :]
    head0 = lax.broadcasted_iota(jnp.int32, (1, LANES), 1) < HEAD_DIM

    def attend(q, parts):
        n = q.shape[0]
        zq = jnp.zeros_like(q)
        qq = jnp.concatenate([jnp.where(head0, q, zq), jnp.where(head0, zq, q)], axis=0)
        scores = []
        for k, _, bias in parts:
            s = lax.dot_general(qq, k, _NT, preferred_element_type=F32)
            scores.append(s if bias is None else s + bias)
        m = functools.reduce(jnp.maximum, [jnp.max(s, axis=1, keepdims=True) for s in scores])
        es = [jnp.exp(s - m) for s in scores]
        den = sum(jnp.sum(e, axis=1, keepdims=True) for e in es)
        o = sum(jnp.dot(e.astype(BF16), v, preferred_element_type=F32)
                for e, (_, v, _) in zip(es, parts)) / den
        return jnp.where(head0, o[:n], o[n:]).astype(BF16)

    def row_step(r, carry):
        start = pl.multiple_of(jnp.clip(r - NA_ROWS // 2, 0, rows - NA_ROWS) * GRID_W, GRID_W)
        dr0 = jnp.where(r <= NA_ROWS // 2, NA_ROWS - 1 - r,
                        jnp.where(r >= rows - NA_ROWS // 2, rows - 1 - r, NA_ROWS // 2 - 1))
        q0 = pl.multiple_of(r * GRID_W, GRID_W)
        q = q_ref[0, pl.ds(q0, GRID_W), :]
        ks = k_ref[0, pl.ds(start, slab), :]
        vs = v_ref[0, pl.ds(start, slab), :]
        o_ref[0, pl.ds(q0, GRID_W), :] = attend(q, [(ks, vs, bias_ref[0, dr0]), (kx, vx, None)])
        return carry

    lax.fori_loop(0, rows, row_step, 0, unroll=4)
    if ctx_queries:
        o_ref[0, SEQ:T_ALL, :] = attend(q_ref[0, SEQ:T_ALL, :], [(kx, vx, None)])


def _attn_d(qkv, bias, ctx_queries):
    b = qkv.shape[0]
    t_out = T_ALL if ctx_queries else SEQ
    col = lambda g: pl.BlockSpec((1, T_ALL, LANES), lambda i, p: (i, 0, g + p))
    return pl.pallas_call(
        functools.partial(_d_kernel, ctx_queries=ctx_queries),
        out_shape=jax.ShapeDtypeStruct((b, t_out, 2 * LANES), BF16),
        grid=(b, 2),
        in_specs=[
            col(G_DQ), col(G_DK), col(G_DV),
            pl.BlockSpec((1, NA_ROWS, 2 * GRID_W, NA_ROWS * GRID_W), lambda i, p: (p, 0, 0, 0)),
        ],
        out_specs=pl.BlockSpec((1, t_out, LANES), lambda i, p: (i, 0, p)),
        compiler_params=_params(("parallel", "parallel")),
        name="attn_nbr",
    )(qkv, qkv, qkv, bias)


def _outmlp_kernel(x_ref, a_ref, b_ref, c_ref, d_ref, mod_ref, g2_ref, wo_ref, w1_ref, w2_ref, o_ref):
    gw = 2 * LANES
    mix = None
    for n, r in enumerate((a_ref, b_ref, c_ref, d_ref)):
        part = jnp.dot(r[0], wo_ref[n * gw:(n + 1) * gw, :], preferred_element_type=F32)
        mix = part if mix is None else mix + part
    x1 = x_ref[0] + mod_ref[0, 2:3, :] * mix
    h2 = (_rms(x1) * g2_ref[...] * (1.0 + mod_ref[0, 4:5, :]) + mod_ref[0, 3:4, :]).astype(BF16)
    acc = None
    for n in range(D_FF // D_MODEL):
        cols = slice(n * D_MODEL, (n + 1) * D_MODEL)
        hid = jnp.maximum(jnp.dot(h2, w1_ref[:, cols], preferred_element_type=F32), 0.0)
        part = jnp.dot((hid * hid).astype(BF16), w2_ref[cols, :], preferred_element_type=F32)
        acc = part if acc is None else acc + part
    o_ref[0] = x1 + mod_ref[0, 5:6, :] * acc


def _out_mlp(xs, outs, mod, g2, w_out, w1, w2, ctx_tokens, tm=256):
    b = xs.shape[0]
    t_out = T_ALL if ctx_tokens else SEQ
    nt = t_out // tm
    ctx_row = mod.shape[0] - 1
    ctx_tile = SEQ // tm

    def mod_map(i, t):
        return (jnp.where(t >= ctx_tile, ctx_row, i), 0, 0)

    tile = lambda w: pl.BlockSpec((1, tm, w), lambda i, t: (i, t, 0))
    whole = lambda s: pl.BlockSpec(s, lambda i, t: (0, 0))
    return pl.pallas_call(
        _outmlp_kernel,
        out_shape=jax.ShapeDtypeStruct((b, t_out, D_MODEL), F32),
        grid=(b, nt),
        in_specs=[
            tile(D_MODEL), tile(2 * LANES), tile(2 * LANES), tile(2 * LANES), tile(2 * LANES),
            pl.BlockSpec((1, 6, D_MODEL), mod_map),
            whole((1, D_MODEL)), whole((D_MODEL, D_MODEL)), whole((D_MODEL, D_FF)), whole((D_FF, D_MODEL)),
        ],
        out_specs=tile(D_MODEL),
        compiler_params=_params(("parallel", "parallel")),
        name="out_mlp",
    )(xs, *outs, mod, g2, w_out, w1, w2)


def _rope_tables():
    pos = np.arange(SEQ)
    rows, cols = (pos // GRID_W).astype(np.float32), (pos % GRID_W).astype(np.float32)

    def pattern(half):
        freqs = (np.float32(ROPE_BASE) ** (-np.arange(half, dtype=np.float32) / np.float32(half))).astype(np.float32)
        ang = [(p[:, None] * freqs[None, :]).astype(np.float32) for p in (rows, cols)]
        cos = np.concatenate([np.cos(a) for a in (ang[0], ang[0], ang[1], ang[1])], axis=1)
        sin = np.concatenate([-np.sin(ang[0]), np.sin(ang[0]), -np.sin(ang[1]), np.sin(ang[1])], axis=1)
        reps = LANES // (4 * half)
        cos = np.concatenate([np.tile(cos, (1, reps)), np.ones((CTX_LEN, LANES))], axis=0)
        sin = np.concatenate([np.tile(sin, (1, reps)), np.zeros((CTX_LEN, LANES))], axis=0)
        return jnp.asarray(cos, F32), jnp.asarray(sin, F32)

    ca, sa = pattern(HEAD_DIM // 4)
    cb, sb = pattern(B_QK_DIM // 4)
    return ca, sa, cb, sb


def _gain_row(aq, ak, bq, bk, dq, dk):
    one = jnp.ones((LANES,), F32)
    t = lambda g, s=1.0: jnp.tile(g.astype(F32), LANES // g.shape[0]) * s
    groups = [one] * N_GROUPS
    groups[G_AQ] = groups[G_AQ + 1] = t(aq, HEAD_DIM ** -0.5)
    groups[G_AK] = t(ak)
    groups[G_BQ] = groups[G_BQ + 1] = t(bq, B_QK_DIM ** -0.5 * math.log2(math.e))
    groups[G_BK] = groups[G_BK + 1] = t(bk)
    groups[G_CK] = groups[G_CK + 1] = one * HEAD_DIM ** -0.5
    groups[G_DQ] = groups[G_DQ + 1] = t(dq, HEAD_DIM ** -0.5)
    groups[G_DK] = groups[G_DK + 1] = t(dk)
    return jnp.concatenate(groups)[None, :]


def _nbr_bias(rpb):
    qc = np.arange(GRID_W)[:, None]
    kc = np.arange(GRID_W)[None, :]
    cstart = np.clip(qc - NA_COLS // 2, 0, GRID_W - NA_COLS)
    valid = (kc >= cstart) & (kc < cstart + NA_COLS)
    pad = GRID_W - NA_COLS
    rp = jnp.pad(rpb.astype(F32), ((0, 0), (0, 0), (pad, pad)))
    toep = jnp.stack([rp[:, :, GRID_W - 1 - q:2 * GRID_W - 1 - q] for q in range(GRID_W)], axis=2)
    toep = jnp.where(valid[None, None], toep, NEG_INF)
    b = jnp.stack([toep[:, kr:kr + NA_ROWS] for kr in range(NA_ROWS)], axis=3)
    b = b.reshape(rpb.shape[0] // 2, 2, NA_ROWS, GRID_W, NA_ROWS * GRID_W)
    return b.transpose(0, 2, 1, 3, 4).reshape(rpb.shape[0] // 2, NA_ROWS, 2 * GRID_W, NA_ROWS * GRID_W)


def kernel(x, c, ctx, c_ctx, w_mod, b_mod, norm1_g, norm2_g, w_in, w_out, a_qnorm_g, a_knorm_g, a_sink, b_qnorm_g, b_knorm_g, b_lambda_q1, b_lambda_k1, b_lambda_q2, b_lambda_k2, b_subln_g, c_decay_fwd, c_decay_bwd, c_gn_g, d_qnorm_g, d_knorm_g, d_rpb, w_mlp1, w_mlp2):
    depth = w_mod.shape[0]
    b = x.shape[0]
    assert b + 1 <= 8
    cc = jnp.concatenate([c, c_ctx[None, :], jnp.zeros((8 - b - 1, D_MODEL), F32)], axis=0)
    mods = _modulation(cc, w_mod, b_mod)[:, :b + 1].reshape(depth, b + 1, 6, D_MODEL)
    tabs = _rope_tables()
    xs = jnp.concatenate([x, ctx], axis=1)

    for layer in range(depth):
        need_ctx = layer < depth - 1
        lambda_init = 0.8 - 0.6 * math.exp(-0.3 * layer)
        gain = _gain_row(a_qnorm_g[layer], a_knorm_g[layer], b_qnorm_g[layer], b_knorm_g[layer],
                         d_qnorm_g[layer], d_knorm_g[layer])
        qkv, gate = _in_proj(xs, mods[layer], norm1_g[layer][None, :], w_in[layer].astype(BF16), gain, tabs)

        out_a = _attn_a(qkv, a_sink[layer].astype(F32), need_ctx)
        kt = jnp.swapaxes(qkv[:, :, G_BK * LANES:(G_BK + 2) * LANES], 1, 2)
        lam_vecs = jnp.stack([b_lambda_q1[layer], b_lambda_k1[layer],
                              b_lambda_q2[layer], b_lambda_k2[layer]]).astype(F32)
        out_b = _attn_b(qkv, kt, lam_vecs, b_subln_g[layer][None, :], need_ctx, lambda_init)
        dec = jnp.stack([c_decay_fwd[layer].reshape(2, 2), c_decay_bwd[layer].reshape(2, 2)],
                        axis=1).reshape(2, 1, 4).astype(F32)
        kt_c = qkv[:, :, G_CK * LANES:(G_CK + 2) * LANES].reshape(b, N_ALL_BLK, BLK, 2 * LANES)
        kt_c = kt_c.transpose(0, 1, 3, 2)
        out_c = _retention(qkv, kt_c, gate, dec, jnp.tile(c_gn_g[layer], 2)[None, :], need_ctx)
        out_d = _attn_d(qkv, _nbr_bias(d_rpb[layer]), need_ctx)

        xs = _out_mlp(xs, (out_a, out_b, out_c, out_d), mods[layer], norm2_g[layer][None, :],
                      w_out[layer].astype(BF16), w_mlp1[layer].astype(BF16), w_mlp2[layer].astype(BF16),
                      need_ctx)
    return xs
```

```python
import functools
import math

import numpy as np
import jax
import jax.numpy as jnp
from jax import lax
from jax.experimental import pallas as pl
from jax.experimental.pallas import tpu as pltpu

F32 = jnp.float32
BF16 = jnp.bfloat16

D_MODEL = 1024
SEQ = 4096
CTX_LEN = 256
T_ALL = SEQ + CTX_LEN
GRID_W = 64
HEAD_DIM = 64
B_QK_DIM = 32
BLK = 128
N_LAT_BLK = SEQ // BLK
N_ALL_BLK = T_ALL // BLK
B_KEY_CHUNK = 256
B_ROW_SLAB = 32
WINDOW = 128
NA_ROWS = 8
NA_COLS = 16
D_FF = 4 * D_MODEL
IN_WIDTH = 3072
ROPE_BASE = 10000.0
EPS = 1e-6
NEG_INF = -1e30
LANES = 128
VMEM_LIMIT = 52 * 1024 * 1024

G_AQ, G_AK, G_AV = 0, 2, 3
G_BQ, G_BK, G_BV = 4, 6, 8
G_CQ, G_CK, G_CV, G_CG = 10, 12, 14, 16
G_DQ, G_DK, G_DV = 18, 20, 22
N_GROUPS = IN_WIDTH // LANES

_NT = (((1,), (1,)), ((), ()))
_TN = (((0,), (0,)), ((), ()))


def _params(sem):
    return pltpu.CompilerParams(dimension_semantics=sem, vmem_limit_bytes=VMEM_LIMIT)


def _rms(x):
    return x * lax.rsqrt(jnp.mean(x * x, axis=-1, keepdims=True) + EPS)


def _mod_kernel(c_ref, w_ref, b_ref, o_ref):
    c = c_ref[...]
    s = c / (1.0 + jnp.exp(-c))
    s_hi = s.astype(BF16)
    s_lo = (s - s_hi.astype(F32)).astype(BF16)
    w = w_ref[0]
    w_hi = w.astype(BF16)
    w_lo = (w - w_hi.astype(F32)).astype(BF16)
    acc = jnp.dot(s_hi, w_hi, preferred_element_type=F32)
    acc += jnp.dot(s_lo, w_hi, preferred_element_type=F32)
    acc += jnp.dot(s_hi, w_lo, preferred_element_type=F32)
    o_ref[0] = acc + b_ref[0]


def _modulation(cc, w_mod, b_mod):
    depth = w_mod.shape[0]
    n_col = 6 * D_MODEL // 1024
    return pl.pallas_call(
        _mod_kernel,
        out_shape=jax.ShapeDtypeStruct((depth, 8, 6 * D_MODEL), F32),
        grid=(depth, n_col),
        in_specs=[
            pl.BlockSpec((8, D_MODEL), lambda l, j: (0, 0)),
            pl.BlockSpec((1, D_MODEL, 1024), lambda l, j: (l, 0, j)),
            pl.BlockSpec((1, 1, 1024), lambda l, j: (l, 0, j)),
        ],
        out_specs=pl.BlockSpec((1, 8, 1024), lambda l, j: (l, 0, j)),
        compiler_params=_params(("parallel", "parallel")),
        name="adaln_mod",
    )(cc, w_mod, b_mod.reshape(depth, 1, 6 * D_MODEL))


def _seg_inv_rms(v, seg):
    sq = v * v
    hi = sq.astype(BF16)
    lo = (sq - hi.astype(F32)).astype(BF16)
    r = lax.broadcasted_iota(jnp.int32, (LANES, LANES), 0) // seg
    c = lax.broadcasted_iota(jnp.int32, (LANES, LANES), 1) // seg
    bd = jnp.where(r == c, 1.0, 0.0).astype(BF16)
    ssum = jnp.dot(hi, bd, preferred_element_type=F32) + jnp.dot(lo, bd, preferred_element_type=F32)
    return lax.rsqrt(ssum * (1.0 / seg) + EPS)


def _rotate(v, cos, sin_signed, half):
    up = pltpu.roll(v, LANES - half, 1)
    dn = pltpu.roll(v, half, 1)
    lane = lax.broadcasted_iota(jnp.int32, v.shape, 1)
    partner = jnp.where(lane % (2 * half) < half, up, dn)
    return v * cos + partner * sin_signed


def _inproj_kernel(x_ref, mod_ref, g1_ref, w_ref, gain_ref, ca_ref, sa_ref, cb_ref, sb_ref,
                   qkv_ref, gate_ref, akv_ref, ktb_ref, ktc_ref):
    x = x_ref[0]
    sh = mod_ref[0, 0:1, :]
    sc = mod_ref[0, 1:2, :]
    h = (_rms(x) * g1_ref[...] * (1.0 + sc) + sh).astype(BF16)
    acc = jnp.dot(h, w_ref[...], preferred_element_type=F32)
    tm = acc.shape[0]
    head0 = lax.broadcasted_iota(jnp.int32, (1, LANES), 1) < HEAD_DIM
    for g in range(N_GROUPS):
        cols = slice(g * LANES, (g + 1) * LANES)
        v = acc[:, cols]
        gain = gain_ref[:, cols]
        if g in (G_AQ, G_AQ + 1, G_AK):
            v = v * _seg_inv_rms(v, HEAD_DIM) * gain
            v = _rotate(v, ca_ref[...], sa_ref[...], HEAD_DIM // 4)
        elif g in (G_BQ, G_BQ + 1, G_BK, G_BK + 1):
            v = v * _seg_inv_rms(v, B_QK_DIM) * gain
            v = _rotate(v, cb_ref[...], sb_ref[...], B_QK_DIM // 4)
        elif g in (G_DQ, G_DQ + 1, G_DK, G_DK + 1):
            v = v * _seg_inv_rms(v, HEAD_DIM) * gain
        elif g in (G_CK, G_CK + 1):
            v = v * gain
        elif g in (G_CG, G_CG + 1):
            gcols = slice((g - G_CG) * LANES, (g - G_CG + 1) * LANES)
            gate_ref[0, :, gcols] = v / (1.0 + jnp.exp(-v))
        qkv_ref[0, :, cols] = v.astype(BF16)
        if g in (G_AK, G_AV):
            n = 2 * (0 if g == G_AK else 1)
            swapped = pltpu.roll(v, HEAD_DIM, 1)
            akv_ref[0, :, n * LANES:(n + 1) * LANES] = jnp.where(head0, v, swapped).astype(BF16)
            akv_ref[0, :, (n + 1) * LANES:(n + 2) * LANES] = jnp.where(head0, swapped, v).astype(BF16)
        elif g in (G_BK, G_BK + 1):
            r0 = (g - G_BK) * LANES
            ktb_ref[0, r0:r0 + LANES, :] = v.T.astype(BF16)
        elif g in (G_CK, G_CK + 1):
            r0 = (g - G_CK) * LANES
            for c in range(tm // BLK):
                ktc_ref[0, c, r0:r0 + LANES, :] = v[c * BLK:(c + 1) * BLK].T.astype(BF16)


def _in_proj(xs, mod, g1, w_in, gain, tabs, tm=256):
    b = xs.shape[0]
    nt = T_ALL // tm
    ctx_row = mod.shape[0] - 1

    def mod_map(t, i):
        return (jnp.where(t == nt - 1, ctx_row, i), 0, 0)

    tab_spec = pl.BlockSpec((tm, LANES), lambda t, i: (t, 0))
    return pl.pallas_call(
        _inproj_kernel,
        out_shape=(jax.ShapeDtypeStruct((b, T_ALL, IN_WIDTH), BF16),
                   jax.ShapeDtypeStruct((b, T_ALL, 2 * LANES), F32),
                   jax.ShapeDtypeStruct((b, T_ALL, 4 * LANES), BF16),
                   jax.ShapeDtypeStruct((b, 2 * LANES, T_ALL), BF16),
                   jax.ShapeDtypeStruct((b, N_ALL_BLK, 2 * LANES, BLK), BF16)),
        grid=(nt, b),
        in_specs=[
            pl.BlockSpec((1, tm, D_MODEL), lambda t, i: (i, t, 0)),
            pl.BlockSpec((1, 6, D_MODEL), mod_map),
            pl.BlockSpec((1, D_MODEL), lambda t, i: (0, 0)),
            pl.BlockSpec((D_MODEL, IN_WIDTH), lambda t, i: (0, 0)),
            pl.BlockSpec((1, IN_WIDTH), lambda t, i: (0, 0)),
            tab_spec, tab_spec, tab_spec, tab_spec,
        ],
        out_specs=(pl.BlockSpec((1, tm, IN_WIDTH), lambda t, i: (i, t, 0)),
                   pl.BlockSpec((1, tm, 2 * LANES), lambda t, i: (i, t, 0)),
                   pl.BlockSpec((1, tm, 4 * LANES), lambda t, i: (i, t, 0)),
                   pl.BlockSpec((1, 2 * LANES, tm), lambda t, i: (i, 0, t)),
                   pl.BlockSpec((1, tm // BLK, 2 * LANES, BLK), lambda t, i: (i, t, 0, 0))),
        compiler_params=_params(("parallel", "parallel")),
        name="in_proj",
    )(xs, mod, g1, w_in, gain, *tabs)


def _a_kernel(sink_ref, q_ref, k_ref, v_ref, o_ref, *, ctx_queries):
    p = pl.program_id(1)
    nq = 2 * BLK
    strip = 3 * BLK
    head0 = lax.broadcasted_iota(jnp.int32, (1, LANES), 1) < HEAD_DIM
    row = lax.broadcasted_iota(jnp.int32, (nq, 1), 0)
    sink_col = jnp.where(row < BLK, sink_ref[2 * p], sink_ref[2 * p + 1])
    kx = k_ref[0, SEQ:T_ALL, :]
    vx = v_ref[0, SEQ:T_ALL, :]
    q_in_blk = lax.broadcasted_iota(jnp.int32, (nq, strip), 0) % BLK
    k_in_strip = lax.broadcasted_iota(jnp.int32, (nq, strip), 1)

    def stacked(q):
        zq = jnp.zeros_like(q)
        return jnp.concatenate([jnp.where(head0, q, zq), jnp.where(head0, zq, q)], axis=0)

    def attend(parts):
        m = functools.reduce(jnp.maximum, [jnp.max(s, axis=1, keepdims=True) for s, _ in parts] + [sink_col])
        es = [jnp.exp(s - m) for s, _ in parts]
        den = sum(jnp.sum(e, axis=1, keepdims=True) for e in es) + jnp.exp(sink_col - m)
        o = sum(jnp.dot(e.astype(BF16), v, preferred_element_type=F32) for e, (_, v) in zip(es, parts)) / den
        return jnp.where(head0, o[:BLK], o[BLK:]).astype(BF16)

    def block(j, carry):
        q0 = pl.multiple_of(j * BLK, BLK)
        start = pl.multiple_of(jnp.clip(j - 1, 0, N_LAT_BLK - 3) * BLK, BLK)
        qq = stacked(q_ref[0, pl.ds(q0, BLK), :])
        ks = k_ref[0, pl.ds(start, strip), :]
        vs = v_ref[0, pl.ds(start, strip), :]
        s_win = lax.dot_general(qq, ks, _NT, preferred_element_type=F32)
        dist = (q0 + q_in_blk) - (start + k_in_strip)
        s_win = jnp.where(jnp.abs(dist) <= WINDOW, s_win, NEG_INF)
        s_ctx = lax.dot_general(qq, kx, _NT, preferred_element_type=F32)
        o_ref[0, pl.ds(q0, BLK), :] = attend([(s_ctx, vx), (s_win, vs)])
        return carry

    lax.fori_loop(0, N_LAT_BLK, block, 0, unroll=2)
    if ctx_queries:
        for q0 in range(SEQ, T_ALL, BLK):
            qq = stacked(q_ref[0, q0:q0 + BLK, :])
            s_ctx = lax.dot_general(qq, kx, _NT, preferred_element_type=F32)
            o_ref[0, q0:q0 + BLK, :] = attend([(s_ctx, vx)])


def _attn_a(qkv, akv, sink, ctx_queries):
    b = qkv.shape[0]
    t_out = T_ALL if ctx_queries else SEQ
    return pl.pallas_call(
        functools.partial(_a_kernel, ctx_queries=ctx_queries),
        out_shape=jax.ShapeDtypeStruct((b, t_out, 2 * LANES), BF16),
        grid=(b, 2),
        in_specs=[
            pl.BlockSpec(memory_space=pltpu.SMEM),
            pl.BlockSpec((1, T_ALL, LANES), lambda i, p: (i, 0, G_AQ + p)),
            pl.BlockSpec((1, T_ALL, LANES), lambda i, p: (i, 0, p)),
            pl.BlockSpec((1, T_ALL, LANES), lambda i, p: (i, 0, 2 + p)),
        ],
        out_specs=pl.BlockSpec((1, t_out, LANES), lambda i, p: (i, 0, p)),
        compiler_params=_params(("parallel", "parallel")),
        name="attn_window",
    )(sink, qkv, akv, akv)


def _b_kernel(lam_ref, subg_ref, q_ref, kt_ref, v_ref, o_ref, s_ref, m_ref, mx_ref, l_ref, acc_ref, qq_ref,
              *, ctx_queries, lambda_init):
    lv = lam_ref[...]
    lam = (jnp.exp(jnp.sum(lv[0:1] * lv[1:2], axis=1, keepdims=True))
           - jnp.exp(jnp.sum(lv[2:3] * lv[3:4], axis=1, keepdims=True)) + lambda_init)
    lane = lax.broadcasted_iota(jnp.int32, (1, LANES), 1)
    rows = 4 * BLK

    def chunks_of(k0, nk):
        return [(c, min(B_KEY_CHUNK, k0 + nk - c)) for c in range(k0, k0 + nk, B_KEY_CHUNK)]

    def tiles(t):
        return [t[:, c:c + LANES] for c in range(0, t.shape[1], LANES)]

    def stack_q(blk):
        q = q_ref[0, pl.ds(pl.multiple_of(blk * BLK, BLK), BLK), :]
        for n in range(4):
            mine = (lane >= n * B_QK_DIM) & (lane < (n + 1) * B_QK_DIM)
            qq_ref[n * BLK:(n + 1) * BLK] = jnp.where(mine, q, jnp.zeros_like(q))

    def score_chunk(slot, c0, cn, first):
        s = jnp.dot(qq_ref[...], kt_ref[0, :, c0:c0 + cn], preferred_element_type=F32)
        s_ref[slot, :, c0:c0 + cn] = s
        for r in range(0, rows, B_ROW_SLAB):
            t = functools.reduce(jnp.maximum, tiles(s[r:r + B_ROW_SLAB]))
            mx_ref[r:r + B_ROW_SLAB] = t if first else jnp.maximum(mx_ref[r:r + B_ROW_SLAB], t)

    def keep_max(slot):
        m_ref[slot] = jnp.broadcast_to(jnp.max(mx_ref[...], axis=1, keepdims=True), (rows, LANES))

    def prob_chunk(slot, c0, cn, first):
        parts = []
        for r in range(0, rows, B_ROW_SLAB):
            m = m_ref[slot, r:r + B_ROW_SLAB]
            es = [jnp.exp2(t - m) for t in tiles(s_ref[slot, r:r + B_ROW_SLAB, c0:c0 + cn])]
            t = functools.reduce(jnp.add, es)
            l_ref[r:r + B_ROW_SLAB] = t if first else l_ref[r:r + B_ROW_SLAB] + t
            parts.append(jnp.concatenate(es, axis=1).astype(BF16))
        part = jnp.dot(jnp.concatenate(parts, axis=0), v_ref[0, c0:c0 + cn, :],
                       preferred_element_type=F32)
        acc_ref[...] = part if first else acc_ref[...] + part

    def finish(blk):
        o = acc_ref[...] * (1.0 / jnp.sum(l_ref[...], axis=1, keepdims=True))
        outs = []
        for hh in range(2):
            r0 = 2 * hh * BLK
            oh = o[r0:r0 + BLK] - lam * o[r0 + BLK:r0 + 2 * BLK]
            oh = oh[:, hh * HEAD_DIM:(hh + 1) * HEAD_DIM]
            outs.append(_rms(oh) * subg_ref[...] * (1.0 - lambda_init))
        o_ref[0, pl.ds(pl.multiple_of(blk * BLK, BLK), BLK), :] = jnp.concatenate(outs, axis=1).astype(BF16)

    def sweep(score_blk, score_slot, prob_blk, prob_slot, chunks):
        if score_blk is not None:
            stack_q(score_blk)
        for n, (c0, cn) in enumerate(chunks):
            if score_blk is not None:
                score_chunk(score_slot, c0, cn, n == 0)
            if prob_blk is not None:
                prob_chunk(prob_slot, c0, cn, n == 0)
        if score_blk is not None:
            keep_max(score_slot)
        if prob_blk is not None:
            finish(prob_blk)

    all_keys = chunks_of(0, T_ALL)
    sweep(0, 0, None, None, all_keys)

    def body(n, carry):
        blk = 2 * n + 1
        sweep(blk, 1, blk - 1, 0, all_keys)
        sweep(blk + 1, 0, blk, 1, all_keys)
        return carry

    lax.fori_loop(0, (N_LAT_BLK - 2) // 2, body, 0)
    sweep(N_LAT_BLK - 1, 1, N_LAT_BLK - 2, 0, all_keys)
    sweep(None, None, N_LAT_BLK - 1, 1, all_keys)
    if ctx_queries:
        ctx_keys = chunks_of(SEQ, CTX_LEN)
        for blk in range(N_LAT_BLK, N_ALL_BLK):
            sweep(blk, 0, None, None, ctx_keys)
            sweep(None, None, blk, 0, ctx_keys)


def _attn_b(qkv, kt, lam_vecs, subg, ctx_queries, lambda_init):
    b = qkv.shape[0]
    t_out = T_ALL if ctx_queries else SEQ
    return pl.pallas_call(
        functools.partial(_b_kernel, ctx_queries=ctx_queries, lambda_init=lambda_init),
        out_shape=jax.ShapeDtypeStruct((b, t_out, 2 * LANES), BF16),
        grid=(b, 2),
        in_specs=[
            pl.BlockSpec((4, B_QK_DIM), lambda i, p: (0, 0)),
            pl.BlockSpec((1, HEAD_DIM), lambda i, p: (0, 0)),
            pl.BlockSpec((1, T_ALL, LANES), lambda i, p: (i, 0, G_BQ + p)),
            pl.BlockSpec((1, LANES, T_ALL), lambda i, p: (i, p, 0)),
            pl.BlockSpec((1, T_ALL, LANES), lambda i, p: (i, 0, G_BV + p)),
        ],
        out_specs=pl.BlockSpec((1, t_out, LANES), lambda i, p: (i, 0, p)),
        scratch_shapes=[pltpu.VMEM((2, 4 * BLK, T_ALL), F32), pltpu.VMEM((2, 4 * BLK, LANES), F32)]
        + [pltpu.VMEM((4 * BLK, LANES), F32)] * 3 + [pltpu.VMEM((4 * BLK, LANES), BF16)],
        compiler_params=_params(("parallel", "parallel")),
        name="attn_diff",
    )(lam_vecs, subg, qkv, kt, qkv)


def _c_kernel(dec_ref, gn_ref, q_ref, kt_ref, v_ref, g_ref, o_ref, oi_ref, kv_ref, st_ref, tab_ref,
              *, ctx_out):
    n_ctx = CTX_LEN // BLK
    lane = lax.broadcasted_iota(jnp.int32, (1, LANES), 1)
    rowi = lax.broadcasted_iota(jnp.int32, (BLK, 1), 0)
    head0_l = lane < HEAD_DIM
    head0_r = rowi < HEAD_DIM
    same_head = head0_r == head0_l
    rel = (lax.broadcasted_iota(jnp.int32, (BLK, BLK), 0)
           - lax.broadcasted_iota(jnp.int32, (BLK, BLK), 1)).astype(F32)
    pos_r = rowi.astype(F32)
    pos_l = lane.astype(F32)

    def log_gamma(col):
        x = dec_ref[0, :, col:col + 1]
        return jnp.minimum(x, 0.0) - jnp.log1p(jnp.exp(-jnp.abs(x)))

    lg = [log_gamma(n) for n in range(4)]
    lg_l = [jnp.where(head0_l, lg[2 * d], lg[2 * d + 1]) for d in range(2)]
    lg_r = [jnp.where(head0_r, lg[2 * d], lg[2 * d + 1]) for d in range(2)]
    for hh in range(2):
        tab_ref[hh] = (jnp.where(rel >= 0, jnp.exp(rel * lg[hh]), 0.0)
                       + jnp.where(rel <= 0, jnp.exp(-rel * lg[2 + hh]), 0.0))
    tab_ref[2] = jnp.exp((pos_r + 1.0) * lg_l[0])
    tab_ref[3] = jnp.exp((BLK - pos_r) * lg_l[1])
    tab_ref[4] = jnp.exp((BLK - 1.0 - pos_l) * lg_r[0])
    tab_ref[5] = jnp.exp(pos_l * lg_r[1])
    chunk_decay = [jnp.exp(BLK * lg_l[d]) for d in range(2)]

    def local(c, carry):
        r0 = pl.multiple_of(c * BLK, BLK)
        q = q_ref[0, pl.ds(r0, BLK), :]
        kt = kt_ref[0, c]
        v = v_ref[0, pl.ds(r0, BLK), :]
        zq, zv = jnp.zeros_like(q), jnp.zeros_like(v)
        qq = jnp.concatenate([jnp.where(head0_l, q, zq), jnp.where(head0_l, zq, q)], axis=0)
        p = jnp.dot(qq, kt, preferred_element_type=F32)
        inner = jnp.concatenate([p[:BLK] * tab_ref[0], p[BLK:] * tab_ref[1]], axis=1).astype(BF16)
        vv = jnp.concatenate([jnp.where(head0_l, v, zv), jnp.where(head0_l, zv, v)], axis=0)
        oi_ref[pl.ds(r0, BLK), :] = jnp.dot(inner, vv, preferred_element_type=F32)
        ktf = kt.astype(F32)
        kk = jnp.concatenate([ktf * tab_ref[4], ktf * tab_ref[5]], axis=0).astype(BF16)
        kv = jnp.dot(kk, v, preferred_element_type=F32)
        kv_ref[c, :BLK] = jnp.where(same_head, kv[:BLK], 0.0)
        kv_ref[c, BLK:] = jnp.where(same_head, kv[BLK:], 0.0)
        return carry

    lax.fori_loop(0, N_ALL_BLK, local, 0, unroll=4)

    def scan(i, states):
        sf, sb = states
        cf = jnp.where(i < n_ctx, N_LAT_BLK + i, i - n_ctx)
        cb = N_ALL_BLK - 1 - i
        st_ref[cf, :BLK] = sf.astype(BF16)
        st_ref[cb, BLK:] = sb.astype(BF16)
        return sf * chunk_decay[0] + kv_ref[cf, :BLK], sb * chunk_decay[1] + kv_ref[cb, BLK:]

    zero = jnp.zeros((BLK, LANES), F32)
    lax.fori_loop(0, N_ALL_BLK, scan, (zero, zero))

    def finish(c, carry):
        r0 = pl.multiple_of(c * BLK, BLK)
        qf = q_ref[0, pl.ds(r0, BLK), :].astype(F32)
        qd = jnp.concatenate([qf * tab_ref[2], qf * tab_ref[3]], axis=1).astype(BF16)
        tot = oi_ref[pl.ds(r0, BLK), :] + jnp.dot(qd, st_ref[c], preferred_element_type=F32)
        sq = tot * tot
        ms0 = jnp.sum(jnp.where(head0_l, sq, 0.0), axis=1, keepdims=True) * (1.0 / HEAD_DIM)
        ms1 = jnp.sum(jnp.where(head0_l, 0.0, sq), axis=1, keepdims=True) * (1.0 / HEAD_DIM)
        inv = jnp.where(head0_l, lax.rsqrt(ms0 + EPS), lax.rsqrt(ms1 + EPS))
        o_ref[0, pl.ds(r0, BLK), :] = (tot * inv * gn_ref[...] * g_ref[0, pl.ds(r0, BLK), :]).astype(BF16)
        return carry

    lax.fori_loop(0, N_ALL_BLK if ctx_out else N_LAT_BLK, finish, 0, unroll=4)


def _retention(qkv, kt, gate, dec, gn, ctx_out):
    b = qkv.shape[0]
    t_out = T_ALL if ctx_out else SEQ
    col = lambda g: pl.BlockSpec((1, T_ALL, LANES), lambda i, p: (i, 0, g + p))
    return pl.pallas_call(
        functools.partial(_c_kernel, ctx_out=ctx_out),
        out_shape=jax.ShapeDtypeStruct((b, t_out, 2 * LANES), BF16),
        grid=(b, 2),
        in_specs=[
            pl.BlockSpec((1, 1, 4), lambda i, p: (p, 0, 0)),
            pl.BlockSpec((1, LANES), lambda i, p: (0, 0)),
            col(G_CQ),
            pl.BlockSpec((1, N_ALL_BLK, LANES, BLK), lambda i, p: (i, 0, p, 0)),
            col(G_CV),
            pl.BlockSpec((1, T_ALL, LANES), lambda i, p: (i, 0, p)),
        ],
        out_specs=pl.BlockSpec((1, t_out, LANES), lambda i, p: (i, 0, p)),
        scratch_shapes=[pltpu.VMEM((T_ALL, LANES), F32), pltpu.VMEM((N_ALL_BLK, 2 * BLK, LANES), F32),
                        pltpu.VMEM((N_ALL_BLK, 2 * BLK, LANES), BF16), pltpu.VMEM((6, BLK, BLK), F32)],
        compiler_params=_params(("parallel", "parallel")),
        name="retention",
    )(dec, gn, qkv, kt, qkv, gate)


def _d_kernel(q_ref, k_ref, v_ref, bias_ref, o_ref, *, ctx_queries):
    rows = SEQ // GRID_W
    slab = NA_ROWS * GRID_W
    kx = k_ref[0, SEQ:T_ALL, :]
    vx = v_ref[0, SEQ:T_ALL, :]
    head0 = lax.broadcasted_iota(jnp.int32, (1, LANES), 1) < HEAD_DIM

    def attend(q, parts):
        n = q.shape[0]
        zq = jnp.zeros_like(q)
        qq = jnp.concatenate([jnp.where(head0, q, zq), jnp.where(head0, zq, q)], axis=0)
        scores = []
        for k, _, bias in parts:
            s = lax.dot_general(qq, k, _NT, preferred_element_type=F32)
            scores.append(s if bias is None else s + bias)
        m = functools.reduce(jnp.maximum, [jnp.max(s, axis=1, keepdims=True) for s in scores])
        es = [jnp.exp(s - m) for s in scores]
        den = sum(jnp.sum(e, axis=1, keepdims=True) for e in es)
        o = sum(jnp.dot(e.astype(BF16), v, preferred_element_type=F32)
                for e, (_, v, _) in zip(es, parts)) / den
        return jnp.where(head0, o[:n], o[n:]).astype(BF16)

    def row_step(r, carry):
        start = pl.multiple_of(jnp.clip(r - NA_ROWS // 2, 0, rows - NA_ROWS) * GRID_W, GRID_W)
        dr0 = jnp.where(r <= NA_ROWS // 2, NA_ROWS - 1 - r,
                        jnp.where(r >= rows - NA_ROWS // 2, rows - 1 - r, NA_ROWS // 2 - 1))
        q0 = pl.multiple_of(r * GRID_W, GRID_W)
        q = q_ref[0, pl.ds(q0, GRID_W), :]
        ks = k_ref[0, pl.ds(start, slab), :]
        vs = v_ref[0, pl.ds(start, slab), :]
        o_ref[0, pl.ds(q0, GRID_W), :] = attend(q, [(ks, vs, bias_ref[0, dr0]), (kx, vx, None)])
        return carry

    lax.fori_loop(0, rows, row_step, 0, unroll=4)
    if ctx_queries:
        o_ref[0, SEQ:T_ALL, :] = attend(q_ref[0, SEQ:T_ALL, :], [(kx, vx, None)])


def _attn_d(qkv, bias, ctx_queries):
    b = qkv.shape[0]
    t_out = T_ALL if ctx_queries else SEQ
    col = lambda g: pl.BlockSpec((1, T_ALL, LANES), lambda i, p: (i, 0, g + p))
    return pl.pallas_call(
        functools.partial(_d_kernel, ctx_queries=ctx_queries),
        out_shape=jax.ShapeDtypeStruct((b, t_out, 2 * LANES), BF16),
        grid=(b, 2),
        in_specs=[
            col(G_DQ), col(G_DK), col(G_DV),
            pl.BlockSpec((1, NA_ROWS, 2 * GRID_W, NA_ROWS * GRID_W), lambda i, p: (p, 0, 0, 0)),
        ],
        out_specs=pl.BlockSpec((1, t_out, LANES), lambda i, p: (i, 0, p)),
        compiler_params=_params(("parallel", "parallel")),
        name="attn_nbr",
    )(qkv, qkv, qkv, bias)


def _outmlp_kernel(x_ref, a_ref, b_ref, c_ref, d_ref, mod_ref, g2_ref, wo_ref, w1_ref, w2_ref, o_ref):
    gw = 2 * LANES
    mix = None
    for n, r in enumerate((a_ref, b_ref, c_ref, d_ref)):
        part = jnp.dot(r[0], wo_ref[n * gw:(n + 1) * gw, :], preferred_element_type=F32)
        mix = part if mix is None else mix + part
    x1 = x_ref[0] + mod_ref[0, 2:3, :] * mix
    h2 = (_rms(x1) * g2_ref[...] * (1.0 + mod_ref[0, 4:5, :]) + mod_ref[0, 3:4, :]).astype(BF16)
    acc = None
    for n in range(D_FF // D_MODEL):
        cols = slice(n * D_MODEL, (n + 1) * D_MODEL)
        hid = jnp.maximum(jnp.dot(h2, w1_ref[:, cols], preferred_element_type=F32), 0.0)
        part = jnp.dot((hid * hid).astype(BF16), w2_ref[cols, :], preferred_element_type=F32)
        acc = part if acc is None else acc + part
    o_ref[0] = x1 + mod_ref[0, 5:6, :] * acc


def _out_mlp(xs, outs, mod, g2, w_out, w1, w2, ctx_tokens, tm=256):
    b = xs.shape[0]
    t_out = T_ALL if ctx_tokens else SEQ
    nt = t_out // tm
    ctx_row = mod.shape[0] - 1
    ctx_tile = SEQ // tm

    def mod_map(i, t):
        return (jnp.where(t >= ctx_tile, ctx_row, i), 0, 0)

    tile = lambda w: pl.BlockSpec((1, tm, w), lambda i, t: (i, t, 0))
    whole = lambda s: pl.BlockSpec(s, lambda i, t: (0, 0))
    return pl.pallas_call(
        _outmlp_kernel,
        out_shape=jax.ShapeDtypeStruct((b, t_out, D_MODEL), F32),
        grid=(b, nt),
        in_specs=[
            tile(D_MODEL), tile(2 * LANES), tile(2 * LANES), tile(2 * LANES), tile(2 * LANES),
            pl.BlockSpec((1, 6, D_MODEL), mod_map),
            whole((1, D_MODEL)), whole((D_MODEL, D_MODEL)), whole((D_MODEL, D_FF)), whole((D_FF, D_MODEL)),
        ],
        out_specs=tile(D_MODEL),
        compiler_params=_params(("parallel", "parallel")),
        name="out_mlp",
    )(xs, *outs, mod, g2, w_out, w1, w2)


def _rope_tables():
    pos = np.arange(SEQ)
    rows, cols = (pos // GRID_W).astype(np.float32), (pos % GRID_W).astype(np.float32)

    def pattern(half):
        freqs = (np.float32(ROPE_BASE) ** (-np.arange(half, dtype=np.float32) / np.float32(half))).astype(np.float32)
        ang = [(p[:, None] * freqs[None, :]).astype(np.float32) for p in (rows, cols)]
        cos = np.concatenate([np.cos(a) for a in (ang[0], ang[0], ang[1], ang[1])], axis=1)
        sin = np.concatenate([-np.sin(ang[0]), np.sin(ang[0]), -np.sin(ang[1]), np.sin(ang[1])], axis=1)
        reps = LANES // (4 * half)
        cos = np.concatenate([np.tile(cos, (1, reps)), np.ones((CTX_LEN, LANES))], axis=0)
        sin = np.concatenate([np.tile(sin, (1, reps)), np.zeros((CTX_LEN, LANES))], axis=0)
        return jnp.asarray(cos, F32), jnp.asarray(sin, F32)

    ca, sa = pattern(HEAD_DIM // 4)
    cb, sb = pattern(B_QK_DIM // 4)
    return ca, sa, cb, sb


def _gain_row(aq, ak, bq, bk, dq, dk):
    one = jnp.ones((LANES,), F32)
    t = lambda g, s=1.0: jnp.tile(g.astype(F32), LANES // g.shape[0]) * s
    groups = [one] * N_GROUPS
    groups[G_AQ] = groups[G_AQ + 1] = t(aq, HEAD_DIM ** -0.5)
    groups[G_AK] = t(ak)
    groups[G_BQ] = groups[G_BQ + 1] = t(bq, B_QK_DIM ** -0.5 * math.log2(math.e))
    groups[G_BK] = groups[G_BK + 1] = t(bk)
    groups[G_CK] = groups[G_CK + 1] = one * HEAD_DIM ** -0.5
    groups[G_DQ] = groups[G_DQ + 1] = t(dq, HEAD_DIM ** -0.5)
    groups[G_DK] = groups[G_DK + 1] = t(dk)
    return jnp.concatenate(groups)[None, :]


def _nbr_bias(rpb):
    qc = np.arange(GRID_W)[:, None]
    kc = np.arange(GRID_W)[None, :]
    cstart = np.clip(qc - NA_COLS // 2, 0, GRID_W - NA_COLS)
    valid = (kc >= cstart) & (kc < cstart + NA_COLS)
    pad = GRID_W - NA_COLS
    rp = jnp.pad(rpb.astype(F32), ((0, 0), (0, 0), (pad, pad)))
    period = rp.shape[-1]
    flat = jnp.tile(rp, (1, 1, GRID_W + 1))[..., :GRID_W * (period + 1)]
    shifted = flat.reshape(rp.shape[0], rp.shape[1], GRID_W, period + 1)
    toep = shifted[:, :, ::-1, :GRID_W]
    toep = jnp.where(valid[None, None], toep, NEG_INF)
    b = jnp.stack([toep[:, kr:kr + NA_ROWS] for kr in range(NA_ROWS)], axis=3)
    b = b.reshape(rpb.shape[0] // 2, 2, NA_ROWS, GRID_W, NA_ROWS * GRID_W)
    return b.transpose(0, 2, 1, 3, 4).reshape(rpb.shape[0] // 2, NA_ROWS, 2 * GRID_W, NA_ROWS * GRID_W)


def kernel(x, c, ctx, c_ctx, w_mod, b_mod, norm1_g, norm2_g, w_in, w_out, a_qnorm_g, a_knorm_g, a_sink, b_qnorm_g, b_knorm_g, b_lambda_q1, b_lambda_k1, b_lambda_q2, b_lambda_k2, b_subln_g, c_decay_fwd, c_decay_bwd, c_gn_g, d_qnorm_g, d_knorm_g, d_rpb, w_mlp1, w_mlp2):
    depth = w_mod.shape[0]
    b = x.shape[0]
    assert b + 1 <= 8
    cc = jnp.concatenate([c, c_ctx[None, :], jnp.zeros((8 - b - 1, D_MODEL), F32)], axis=0)
    mods = _modulation(cc, w_mod, b_mod)[:, :b + 1].reshape(depth, b + 1, 6, D_MODEL)
    tabs = _rope_tables()
    xs = jnp.concatenate([x, ctx], axis=1)

    for layer in range(depth):
        need_ctx = layer < depth - 1
        lambda_init = 0.8 - 0.6 * math.exp(-0.3 * layer)
        gain = _gain_row(a_qnorm_g[layer], a_knorm_g[layer], b_qnorm_g[layer], b_knorm_g[layer],
                         d_qnorm_g[layer], d_knorm_g[layer])
        qkv, gate, akv, kt_b, kt_c = _in_proj(xs, mods[layer], norm1_g[layer][None, :], w_in[layer].astype(BF16),
                                              gain, tabs)

        out_a = _attn_a(qkv, akv, a_sink[layer].astype(F32), need_ctx)
        lam_vecs = jnp.stack([b_lambda_q1[layer], b_lambda_k1[layer],
                              b_lambda_q2[layer], b_lambda_k2[layer]]).astype(F32)
        out_b = _attn_b(qkv, kt_b, lam_vecs, b_subln_g[layer][None, :], need_ctx, lambda_init)
        dec = jnp.stack([c_decay_fwd[layer].reshape(2, 2), c_decay_bwd[layer].reshape(2, 2)],
                        axis=1).reshape(2, 1, 4).astype(F32)
        out_c = _retention(qkv, kt_c, gate, dec, jnp.tile(c_gn_g[layer], 2)[None, :], need_ctx)
        out_d = _attn_d(qkv, _nbr_bias(d_rpb[layer]), need_ctx)

        xs = _out_mlp(xs, (out_a, out_b, out_c, out_d), mods[layer], norm2_g[layer][None, :],
                      w_out[layer].astype(BF16), w_mlp1[layer].astype(BF16), w_mlp2[layer].astype(BF16),
                      need_ctx)
    return xs
```

```python
import functools
import math

import numpy as np
import jax
import jax.numpy as jnp
from jax import lax
from jax.experimental import pallas as pl
from jax.experimental.pallas import tpu as pltpu

F32 = jnp.float32
BF16 = jnp.bfloat16

D_MODEL = 1024
SEQ = 4096
CTX_LEN = 256
T_ALL = SEQ + CTX_LEN
GRID_W = 64
HEAD_DIM = 64
B_QK_DIM = 32
BLK = 128
N_LAT_BLK = SEQ // BLK
N_ALL_BLK = T_ALL // BLK
B_KEY_CHUNK = 256
D_GROUP = 4
B_ROW_SLAB = 32
B_SAFE_BOUND = 60.0
B_BOUND_SLACK = 1.0 + 2.0 ** -10
WINDOW = 128
NA_ROWS = 8
NA_COLS = 16
D_FF = 4 * D_MODEL
IN_WIDTH = 3072
ROPE_BASE = 10000.0
EPS = 1e-6
NEG_INF = -1e30
LANES = 128
VMEM_LIMIT = 52 * 1024 * 1024

G_AQ, G_AK, G_AV = 0, 2, 3
G_BQ, G_BK, G_BV = 4, 6, 8
G_CQ, G_CK, G_CV, G_CG = 10, 12, 14, 16
G_DQ, G_DK, G_DV = 18, 20, 22
N_GROUPS = IN_WIDTH // LANES

_NT = (((1,), (1,)), ((), ()))
_TN = (((0,), (0,)), ((), ()))


def _params(sem):
    return pltpu.CompilerParams(dimension_semantics=sem, vmem_limit_bytes=VMEM_LIMIT)


def _rms(x):
    return x * lax.rsqrt(jnp.mean(x * x, axis=-1, keepdims=True) + EPS)


def _mod_kernel(c_ref, w_ref, b_ref, o_ref):
    c = c_ref[...]
    s = c / (1.0 + jnp.exp(-c))
    s_hi = s.astype(BF16)
    s_lo = (s - s_hi.astype(F32)).astype(BF16)
    w = w_ref[0]
    w_hi = w.astype(BF16)
    w_lo = (w - w_hi.astype(F32)).astype(BF16)
    acc = jnp.dot(s_hi, w_hi, preferred_element_type=F32)
    acc += jnp.dot(s_lo, w_hi, preferred_element_type=F32)
    acc += jnp.dot(s_hi, w_lo, preferred_element_type=F32)
    o_ref[0] = acc + b_ref[0]


def _modulation(cc, w_mod, b_mod):
    depth = w_mod.shape[0]
    n_col = 6 * D_MODEL // 1024
    return pl.pallas_call(
        _mod_kernel,
        out_shape=jax.ShapeDtypeStruct((depth, 8, 6 * D_MODEL), F32),
        grid=(depth, n_col),
        in_specs=[
            pl.BlockSpec((8, D_MODEL), lambda l, j: (0, 0)),
            pl.BlockSpec((1, D_MODEL, 1024), lambda l, j: (l, 0, j)),
            pl.BlockSpec((1, 1, 1024), lambda l, j: (l, 0, j)),
        ],
        out_specs=pl.BlockSpec((1, 8, 1024), lambda l, j: (l, 0, j)),
        compiler_params=_params(("parallel", "parallel")),
        name="adaln_mod",
    )(cc, w_mod, b_mod.reshape(depth, 1, 6 * D_MODEL))


def _seg_inv_rms(v, seg):
    sq = v * v
    hi = sq.astype(BF16)
    lo = (sq - hi.astype(F32)).astype(BF16)
    r = lax.broadcasted_iota(jnp.int32, (LANES, LANES), 0) // seg
    c = lax.broadcasted_iota(jnp.int32, (LANES, LANES), 1) // seg
    bd = jnp.where(r == c, 1.0, 0.0).astype(BF16)
    ssum = jnp.dot(hi, bd, preferred_element_type=F32) + jnp.dot(lo, bd, preferred_element_type=F32)
    return lax.rsqrt(ssum * (1.0 / seg) + EPS)


def _rotate(v, cos, sin_signed, half):
    up = pltpu.roll(v, LANES - half, 1)
    dn = pltpu.roll(v, half, 1)
    lane = lax.broadcasted_iota(jnp.int32, v.shape, 1)
    partner = jnp.where(lane % (2 * half) < half, up, dn)
    return v * cos + partner * sin_signed


def _inproj_kernel(x_ref, mod_ref, g1_ref, w_ref, gain_ref, ca_ref, sa_ref, cb_ref, sb_ref,
                   qkv_ref, gate_ref, akv_ref, ktb_ref, ktc_ref):
    x = x_ref[0]
    sh = mod_ref[0, 0:1, :]
    sc = mod_ref[0, 1:2, :]
    h = (_rms(x) * g1_ref[...] * (1.0 + sc) + sh).astype(BF16)
    acc = jnp.dot(h, w_ref[...], preferred_element_type=F32)
    tm = acc.shape[0]
    head0 = lax.broadcasted_iota(jnp.int32, (1, LANES), 1) < HEAD_DIM
    for g in range(N_GROUPS):
        cols = slice(g * LANES, (g + 1) * LANES)
        v = acc[:, cols]
        gain = gain_ref[:, cols]
        if g in (G_AQ, G_AQ + 1, G_AK):
            v = v * _seg_inv_rms(v, HEAD_DIM) * gain
            v = _rotate(v, ca_ref[...], sa_ref[...], HEAD_DIM // 4)
        elif g in (G_BQ, G_BQ + 1, G_BK, G_BK + 1):
            v = v * _seg_inv_rms(v, B_QK_DIM) * gain
            v = _rotate(v, cb_ref[...], sb_ref[...], B_QK_DIM // 4)
        elif g in (G_DQ, G_DQ + 1, G_DK, G_DK + 1):
            v = v * _seg_inv_rms(v, HEAD_DIM) * gain
        elif g in (G_CK, G_CK + 1):
            v = v * gain
        elif g in (G_CG, G_CG + 1):
            gcols = slice((g - G_CG) * LANES, (g - G_CG + 1) * LANES)
            gate_ref[0, :, gcols] = v / (1.0 + jnp.exp(-v))
        qkv_ref[0, :, cols] = v.astype(BF16)
        if g in (G_AK, G_AV):
            n = 2 * (0 if g == G_AK else 1)
            swapped = pltpu.roll(v, HEAD_DIM, 1)
            akv_ref[0, :, n * LANES:(n + 1) * LANES] = jnp.where(head0, v, swapped).astype(BF16)
            akv_ref[0, :, (n + 1) * LANES:(n + 2) * LANES] = jnp.where(head0, swapped, v).astype(BF16)
        elif g in (G_BK, G_BK + 1):
            r0 = (g - G_BK) * LANES
            ktb_ref[0, r0:r0 + LANES, :] = v.T.astype(BF16)
        elif g in (G_CK, G_CK + 1):
            r0 = (g - G_CK) * LANES
            for c in range(tm // BLK):
                ktc_ref[0, c, r0:r0 + LANES, :] = v[c * BLK:(c + 1) * BLK].T.astype(BF16)


def _in_proj(xs, mod, g1, w_in, gain, tabs, tm=256):
    b = xs.shape[0]
    nt = T_ALL // tm
    ctx_row = mod.shape[0] - 1

    def mod_map(t, i):
        return (jnp.where(t == nt - 1, ctx_row, i), 0, 0)

    tab_spec = pl.BlockSpec((tm, LANES), lambda t, i: (t, 0))
    return pl.pallas_call(
        _inproj_kernel,
        out_shape=(jax.ShapeDtypeStruct((b, T_ALL, IN_WIDTH), BF16),
                   jax.ShapeDtypeStruct((b, T_ALL, 2 * LANES), F32),
                   jax.ShapeDtypeStruct((b, T_ALL, 4 * LANES), BF16),
                   jax.ShapeDtypeStruct((b, 2 * LANES, T_ALL), BF16),
                   jax.ShapeDtypeStruct((b, N_ALL_BLK, 2 * LANES, BLK), BF16)),
        grid=(nt, b),
        in_specs=[
            pl.BlockSpec((1, tm, D_MODEL), lambda t, i: (i, t, 0)),
            pl.BlockSpec((1, 6, D_MODEL), mod_map),
            pl.BlockSpec((1, D_MODEL), lambda t, i: (0, 0)),
            pl.BlockSpec((D_MODEL, IN_WIDTH), lambda t, i: (0, 0)),
            pl.BlockSpec((1, IN_WIDTH), lambda t, i: (0, 0)),
            tab_spec, tab_spec, tab_spec, tab_spec,
        ],
        out_specs=(pl.BlockSpec((1, tm, IN_WIDTH), lambda t, i: (i, t, 0)),
                   pl.BlockSpec((1, tm, 2 * LANES), lambda t, i: (i, t, 0)),
                   pl.BlockSpec((1, tm, 4 * LANES), lambda t, i: (i, t, 0)),
                   pl.BlockSpec((1, 2 * LANES, tm), lambda t, i: (i, 0, t)),
                   pl.BlockSpec((1, tm // BLK, 2 * LANES, BLK), lambda t, i: (i, t, 0, 0))),
        compiler_params=_params(("parallel", "parallel")),
        name="in_proj",
    )(xs, mod, g1, w_in, gain, *tabs)


def _a_kernel(sink_ref, q_ref, k_ref, v_ref, o_ref, *, ctx_queries):
    p = pl.program_id(1)
    nq = 2 * BLK
    strip = 3 * BLK
    head0 = lax.broadcasted_iota(jnp.int32, (1, LANES), 1) < HEAD_DIM
    row = lax.broadcasted_iota(jnp.int32, (nq, 1), 0)
    sink_col = jnp.where(row < BLK, sink_ref[2 * p], sink_ref[2 * p + 1])
    kx = k_ref[0, SEQ:T_ALL, :]
    vx = v_ref[0, SEQ:T_ALL, :]
    q_in_blk = lax.broadcasted_iota(jnp.int32, (nq, strip), 0) % BLK
    k_in_strip = lax.broadcasted_iota(jnp.int32, (nq, strip), 1)

    def stacked(q):
        zq = jnp.zeros_like(q)
        return jnp.concatenate([jnp.where(head0, q, zq), jnp.where(head0, zq, q)], axis=0)

    def attend(parts):
        m = functools.reduce(jnp.maximum, [jnp.max(s, axis=1, keepdims=True) for s, _ in parts] + [sink_col])
        es = [jnp.exp(s - m) for s, _ in parts]
        den = sum(jnp.sum(e, axis=1, keepdims=True) for e in es) + jnp.exp(sink_col - m)
        o = sum(jnp.dot(e.astype(BF16), v, preferred_element_type=F32) for e, (_, v) in zip(es, parts)) / den
        return jnp.where(head0, o[:BLK], o[BLK:]).astype(BF16)

    def block(j, carry):
        q0 = pl.multiple_of(j * BLK, BLK)
        start = pl.multiple_of(jnp.clip(j - 1, 0, N_LAT_BLK - 3) * BLK, BLK)
        qq = stacked(q_ref[0, pl.ds(q0, BLK), :])
        ks = k_ref[0, pl.ds(start, strip), :]
        vs = v_ref[0, pl.ds(start, strip), :]
        s_win = lax.dot_general(qq, ks, _NT, preferred_element_type=F32)
        dist = (q0 + q_in_blk) - (start + k_in_strip)
        s_win = jnp.where(jnp.abs(dist) <= WINDOW, s_win, NEG_INF)
        s_ctx = lax.dot_general(qq, kx, _NT, preferred_element_type=F32)
        o_ref[0, pl.ds(q0, BLK), :] = attend([(s_ctx, vx), (s_win, vs)])
        return carry

    lax.fori_loop(0, N_LAT_BLK, block, 0, unroll=2)
    if ctx_queries:
        for q0 in range(SEQ, T_ALL, BLK):
            qq = stacked(q_ref[0, q0:q0 + BLK, :])
            s_ctx = lax.dot_general(qq, kx, _NT, preferred_element_type=F32)
            o_ref[0, q0:q0 + BLK, :] = attend([(s_ctx, vx)])


def _attn_a(qkv, akv, sink, ctx_queries):
    b = qkv.shape[0]
    t_out = T_ALL if ctx_queries else SEQ
    return pl.pallas_call(
        functools.partial(_a_kernel, ctx_queries=ctx_queries),
        out_shape=jax.ShapeDtypeStruct((b, t_out, 2 * LANES), BF16),
        grid=(b, 2),
        in_specs=[
            pl.BlockSpec(memory_space=pltpu.SMEM),
            pl.BlockSpec((1, T_ALL, LANES), lambda i, p: (i, 0, G_AQ + p)),
            pl.BlockSpec((1, T_ALL, LANES), lambda i, p: (i, 0, p)),
            pl.BlockSpec((1, T_ALL, LANES), lambda i, p: (i, 0, 2 + p)),
        ],
        out_specs=pl.BlockSpec((1, t_out, LANES), lambda i, p: (i, 0, p)),
        compiler_params=_params(("parallel", "parallel")),
        name="attn_window",
    )(sink, qkv, akv, akv)


def _b_kernel(lam_ref, subg_ref, q_ref, kt_ref, v_ref, o_ref, qq_ref, m_ref, l_ref, acc_ref, s_ref,
              *, ctx_queries, lambda_init):
    lv = lam_ref[...]
    lam = (jnp.exp(jnp.sum(lv[0:1] * lv[1:2], axis=1, keepdims=True))
           - jnp.exp(jnp.sum(lv[2:3] * lv[3:4], axis=1, keepdims=True)) + lambda_init)
    lane = lax.broadcasted_iota(jnp.int32, (1, LANES), 1)
    rows = 4 * BLK
    mine = [(lane >= n * B_QK_DIM) & (lane < (n + 1) * B_QK_DIM) for n in range(4)]

    ksq = kt_ref[0].astype(F32)
    ksq = ksq * ksq
    k_max2 = [jnp.max(jnp.sum(ksq[n * B_QK_DIM:(n + 1) * B_QK_DIM], axis=0, keepdims=True), axis=1, keepdims=True)
              for n in range(4)]

    def chunks_of(k0, nk):
        return [(c, min(B_KEY_CHUNK, k0 + nk - c)) for c in range(k0, k0 + nk, B_KEY_CHUNK)]

    def tiles(t):
        return [t[:, c:c + LANES] for c in range(0, t.shape[1], LANES)]

    def prepare(blk, par):
        q = q_ref[0, pl.ds(pl.multiple_of(blk * BLK, BLK), BLK), :]
        qf = q.astype(F32)
        sq = qf * qf
        top = None
        for n in range(4):
            qq_ref[par, n * BLK:(n + 1) * BLK] = jnp.where(mine[n], q, jnp.zeros_like(q))
            q_norm2 = jnp.sum(jnp.where(mine[n], sq, 0.0), axis=1, keepdims=True)
            bound = jnp.sqrt(q_norm2 * k_max2[n]) * B_BOUND_SLACK
            m_ref[par, n * BLK:(n + 1) * BLK] = jnp.broadcast_to(bound, (BLK, LANES))
            top = bound if top is None else jnp.maximum(top, bound)
        return jnp.max(top) < B_SAFE_BOUND

    def prob_chunk(par, n, c0, cn, s):
        parts = []
        for r in range(0, rows, B_ROW_SLAB):
            m = m_ref[par, r:r + B_ROW_SLAB]
            es = [jnp.exp2(t - m) for t in tiles(s[r:r + B_ROW_SLAB])]
            t = functools.reduce(jnp.add, es)
            l_ref[par, r:r + B_ROW_SLAB] = t if n == 0 else l_ref[par, r:r + B_ROW_SLAB] + t
            parts.append(jnp.concatenate(es, axis=1).astype(BF16))
        part = jnp.dot(jnp.concatenate(parts, axis=0), v_ref[0, c0:c0 + cn, :],
                       preferred_element_type=F32)
        acc_ref[par] = part if n == 0 else acc_ref[par] + part

    def finish(blk, par):
        o = acc_ref[par] * (1.0 / jnp.sum(l_ref[par], axis=1, keepdims=True))
        outs = []
        for hh in range(2):
            r0 = 2 * hh * BLK
            oh = o[r0:r0 + BLK] - lam * o[r0 + BLK:r0 + 2 * BLK]
            oh = oh[:, hh * HEAD_DIM:(hh + 1) * HEAD_DIM]
            outs.append(_rms(oh) * subg_ref[...] * (1.0 - lambda_init))
        o_ref[0, pl.ds(pl.multiple_of(blk * BLK, BLK), BLK), :] = jnp.concatenate(outs, axis=1).astype(BF16)

    def one_sweep(blk, par, chunks):
        for n, (c0, cn) in enumerate(chunks):
            s = jnp.dot(qq_ref[par], kt_ref[0, :, c0:c0 + cn], preferred_element_type=F32)
            prob_chunk(par, n, c0, cn, s)
        finish(blk, par)

    def two_sweeps(blk, par, chunks):
        row_max = None
        for c0, cn in chunks:
            s = jnp.dot(qq_ref[par], kt_ref[0, :, c0:c0 + cn], preferred_element_type=F32)
            s_ref[:, c0:c0 + cn] = s
            t = functools.reduce(jnp.maximum, tiles(s))
            row_max = t if row_max is None else jnp.maximum(row_max, t)
        m_ref[par] = jnp.broadcast_to(jnp.max(row_max, axis=1, keepdims=True), (rows, LANES))
        for n, (c0, cn) in enumerate(chunks):
            prob_chunk(par, n, c0, cn, s_ref[:, c0:c0 + cn])
        finish(blk, par)

    def block_pair(first_blk, chunks):
        ok0 = prepare(first_blk, 0)
        ok1 = prepare(first_blk + 1, 1)
        usable = jnp.logical_and(ok0, ok1)

        @pl.when(usable)
        def _():
            one_sweep(first_blk, 0, chunks)
            one_sweep(first_blk + 1, 1, chunks)

        @pl.when(jnp.logical_not(usable))
        def _():
            two_sweeps(first_blk, 0, chunks)
            two_sweeps(first_blk + 1, 1, chunks)

    all_keys = chunks_of(0, T_ALL)

    def body(n, carry):
        block_pair(2 * n, all_keys)
        return carry

    lax.fori_loop(0, N_LAT_BLK // 2, body, 0)
    if ctx_queries:
        block_pair(N_LAT_BLK, chunks_of(SEQ, CTX_LEN))


def _attn_b(qkv, kt, lam_vecs, subg, ctx_queries, lambda_init):
    b = qkv.shape[0]
    t_out = T_ALL if ctx_queries else SEQ
    assert N_LAT_BLK % 2 == 0 and N_ALL_BLK - N_LAT_BLK == 2
    stacked = lambda dt: pltpu.VMEM((2, 4 * BLK, LANES), dt)
    return pl.pallas_call(
        functools.partial(_b_kernel, ctx_queries=ctx_queries, lambda_init=lambda_init),
        out_shape=jax.ShapeDtypeStruct((b, t_out, 2 * LANES), BF16),
        grid=(b, 2),
        in_specs=[
            pl.BlockSpec((4, B_QK_DIM), lambda i, p: (0, 0)),
            pl.BlockSpec((1, HEAD_DIM), lambda i, p: (0, 0)),
            pl.BlockSpec((1, T_ALL, LANES), lambda i, p: (i, 0, G_BQ + p)),
            pl.BlockSpec((1, LANES, T_ALL), lambda i, p: (i, p, 0)),
            pl.BlockSpec((1, T_ALL, LANES), lambda i, p: (i, 0, G_BV + p)),
        ],
        out_specs=pl.BlockSpec((1, t_out, LANES), lambda i, p: (i, 0, p)),
        scratch_shapes=[stacked(BF16), stacked(F32), stacked(F32), stacked(F32),
                        pltpu.VMEM((4 * BLK, T_ALL), F32)],
        compiler_params=_params(("parallel", "parallel")),
        name="attn_diff",
    )(lam_vecs, subg, qkv, kt, qkv)


def _c_kernel(dec_ref, gn_ref, q_ref, kt_ref, v_ref, g_ref, o_ref, oi_ref, kv_ref, st_ref, tab_ref,
              *, ctx_out):
    n_ctx = CTX_LEN // BLK
    lane = lax.broadcasted_iota(jnp.int32, (1, LANES), 1)
    rowi = lax.broadcasted_iota(jnp.int32, (BLK, 1), 0)
    head0_l = lane < HEAD_DIM
    head0_r = rowi < HEAD_DIM
    same_head = head0_r == head0_l
    rel = (lax.broadcasted_iota(jnp.int32, (BLK, BLK), 0)
           - lax.broadcasted_iota(jnp.int32, (BLK, BLK), 1)).astype(F32)
    pos_r = rowi.astype(F32)
    pos_l = lane.astype(F32)

    def log_gamma(col):
        x = dec_ref[0, :, col:col + 1]
        return jnp.minimum(x, 0.0) - jnp.log1p(jnp.exp(-jnp.abs(x)))

    lg = [log_gamma(n) for n in range(4)]
    lg_l = [jnp.where(head0_l, lg[2 * d], lg[2 * d + 1]) for d in range(2)]
    lg_r = [jnp.where(head0_r, lg[2 * d], lg[2 * d + 1]) for d in range(2)]
    for hh in range(2):
        tab_ref[hh] = (jnp.where(rel >= 0, jnp.exp(rel * lg[hh]), 0.0)
                       + jnp.where(rel <= 0, jnp.exp(-rel * lg[2 + hh]), 0.0))
    tab_ref[2] = jnp.exp((pos_r + 1.0) * lg_l[0])
    tab_ref[3] = jnp.exp((BLK - pos_r) * lg_l[1])
    tab_ref[4] = jnp.exp((BLK - 1.0 - pos_l) * lg_r[0])
    tab_ref[5] = jnp.exp(pos_l * lg_r[1])
    chunk_decay = [jnp.exp(BLK * lg_l[d]) for d in range(2)]

    def local(c, carry):
        r0 = pl.multiple_of(c * BLK, BLK)
        q = q_ref[0, pl.ds(r0, BLK), :]
        kt = kt_ref[0, c]
        v = v_ref[0, pl.ds(r0, BLK), :]
        zq, zv = jnp.zeros_like(q), jnp.zeros_like(v)
        qq = jnp.concatenate([jnp.where(head0_l, q, zq), jnp.where(head0_l, zq, q)], axis=0)
        p = jnp.dot(qq, kt, preferred_element_type=F32)
        inner = jnp.concatenate([p[:BLK] * tab_ref[0], p[BLK:] * tab_ref[1]], axis=1).astype(BF16)
        vv = jnp.concatenate([jnp.where(head0_l, v, zv), jnp.where(head0_l, zv, v)], axis=0)
        oi_ref[pl.ds(r0, BLK), :] = jnp.dot(inner, vv, preferred_element_type=F32)
        ktf = kt.astype(F32)
        kk = jnp.concatenate([ktf * tab_ref[4], ktf * tab_ref[5]], axis=0).astype(BF16)
        kv = jnp.dot(kk, v, preferred_element_type=F32)
        kv_ref[c, :BLK] = jnp.where(same_head, kv[:BLK], 0.0)
        kv_ref[c, BLK:] = jnp.where(same_head, kv[BLK:], 0.0)
        return carry

    lax.fori_loop(0, N_ALL_BLK, local, 0, unroll=4)

    def scan(i, states):
        sf, sb = states
        cf = jnp.where(i < n_ctx, N_LAT_BLK + i, i - n_ctx)
        cb = N_ALL_BLK - 1 - i
        st_ref[cf, :BLK] = sf.astype(BF16)
        st_ref[cb, BLK:] = sb.astype(BF16)
        return sf * chunk_decay[0] + kv_ref[cf, :BLK], sb * chunk_decay[1] + kv_ref[cb, BLK:]

    zero = jnp.zeros((BLK, LANES), F32)
    lax.fori_loop(0, N_ALL_BLK, scan, (zero, zero))

    def finish(c, carry):
        r0 = pl.multiple_of(c * BLK, BLK)
        qf = q_ref[0, pl.ds(r0, BLK), :].astype(F32)
        qd = jnp.concatenate([qf * tab_ref[2], qf * tab_ref[3]], axis=1).astype(BF16)
        tot = oi_ref[pl.ds(r0, BLK), :] + jnp.dot(qd, st_ref[c], preferred_element_type=F32)
        sq = tot * tot
        ms0 = jnp.sum(jnp.where(head0_l, sq, 0.0), axis=1, keepdims=True) * (1.0 / HEAD_DIM)
        ms1 = jnp.sum(jnp.where(head0_l, 0.0, sq), axis=1, keepdims=True) * (1.0 / HEAD_DIM)
        inv = jnp.where(head0_l, lax.rsqrt(ms0 + EPS), lax.rsqrt(ms1 + EPS))
        o_ref[0, pl.ds(r0, BLK), :] = (tot * inv * gn_ref[...] * g_ref[0, pl.ds(r0, BLK), :]).astype(BF16)
        return carry

    lax.fori_loop(0, N_ALL_BLK if ctx_out else N_LAT_BLK, finish, 0, unroll=4)


def _retention(qkv, kt, gate, dec, gn, ctx_out):
    b = qkv.shape[0]
    t_out = T_ALL if ctx_out else SEQ
    col = lambda g: pl.BlockSpec((1, T_ALL, LANES), lambda i, p: (i, 0, g + p))
    return pl.pallas_call(
        functools.partial(_c_kernel, ctx_out=ctx_out),
        out_shape=jax.ShapeDtypeStruct((b, t_out, 2 * LANES), BF16),
        grid=(b, 2),
        in_specs=[
            pl.BlockSpec((1, 1, 4), lambda i, p: (p, 0, 0)),
            pl.BlockSpec((1, LANES), lambda i, p: (0, 0)),
            col(G_CQ),
            pl.BlockSpec((1, N_ALL_BLK, LANES, BLK), lambda i, p: (i, 0, p, 0)),
            col(G_CV),
            pl.BlockSpec((1, T_ALL, LANES), lambda i, p: (i, 0, p)),
        ],
        out_specs=pl.BlockSpec((1, t_out, LANES), lambda i, p: (i, 0, p)),
        scratch_shapes=[pltpu.VMEM((T_ALL, LANES), F32), pltpu.VMEM((N_ALL_BLK, 2 * BLK, LANES), F32),
                        pltpu.VMEM((N_ALL_BLK, 2 * BLK, LANES), BF16), pltpu.VMEM((6, BLK, BLK), F32)],
        compiler_params=_params(("parallel", "parallel")),
        name="retention",
    )(dec, gn, qkv, kt, qkv, gate)


def _d_kernel(q_ref, k_ref, v_ref, bias_ref, o_ref, *, ctx_queries):
    rows = SEQ // GRID_W
    n_grp = rows // D_GROUP
    gq = D_GROUP * GRID_W
    slab_rows = NA_ROWS + D_GROUP
    kx = k_ref[0, SEQ:T_ALL, :]
    vx = v_ref[0, SEQ:T_ALL, :]
    head0 = lax.broadcasted_iota(jnp.int32, (1, LANES), 1) < HEAD_DIM

    def attend(q, parts):
        n = q.shape[0]
        zq = jnp.zeros_like(q)
        qq = jnp.concatenate([jnp.where(head0, q, zq), jnp.where(head0, zq, q)], axis=0)
        scores = []
        for k, _, bias in parts:
            s = lax.dot_general(qq, k, _NT, preferred_element_type=F32)
            scores.append(s if bias is None else s + bias)
        m = functools.reduce(jnp.maximum, [jnp.max(s, axis=1, keepdims=True) for s in scores])
        es = [jnp.exp(s - m) for s in scores]
        den = sum(jnp.sum(e, axis=1, keepdims=True) for e in es)
        o = sum(jnp.dot(e.astype(BF16), v, preferred_element_type=F32)
                for e, (_, v, _) in zip(es, parts)) / den
        return jnp.where(head0, o[:n], o[n:]).astype(BF16)

    def group(g, carry):
        first_key_row = jnp.clip(g * D_GROUP - NA_ROWS // 2, 0, rows - slab_rows)
        start = pl.multiple_of(first_key_row * GRID_W, GRID_W)
        kind = jnp.where(g == 0, 0, jnp.where(g == n_grp - 1, 2, 1))
        q0 = pl.multiple_of(g * gq, gq)
        q = q_ref[0, pl.ds(q0, gq), :]
        ks = k_ref[0, pl.ds(start, slab_rows * GRID_W), :]
        vs = v_ref[0, pl.ds(start, slab_rows * GRID_W), :]
        o_ref[0, pl.ds(q0, gq), :] = attend(q, [(ks, vs, bias_ref[0, kind]), (kx, vx, None)])
        return carry

    lax.fori_loop(0, n_grp, group, 0, unroll=2)
    if ctx_queries:
        o_ref[0, SEQ:T_ALL, :] = attend(q_ref[0, SEQ:T_ALL, :], [(kx, vx, None)])


def _attn_d(qkv, bias, ctx_queries):
    b = qkv.shape[0]
    t_out = T_ALL if ctx_queries else SEQ
    col = lambda g: pl.BlockSpec((1, T_ALL, LANES), lambda i, p: (i, 0, g + p))
    return pl.pallas_call(
        functools.partial(_d_kernel, ctx_queries=ctx_queries),
        out_shape=jax.ShapeDtypeStruct((b, t_out, 2 * LANES), BF16),
        grid=(b, 2),
        in_specs=[
            col(G_DQ), col(G_DK), col(G_DV),
            pl.BlockSpec((1,) + bias.shape[1:], lambda i, p: (p, 0, 0, 0)),
        ],
        out_specs=pl.BlockSpec((1, t_out, LANES), lambda i, p: (i, 0, p)),
        compiler_params=_params(("parallel", "parallel")),
        name="attn_nbr",
    )(qkv, qkv, qkv, bias)


def _outmlp_kernel(x_ref, a_ref, b_ref, c_ref, d_ref, mod_ref, g2_ref, wo_ref, w1_ref, w2_ref, o_ref):
    gw = 2 * LANES
    mix = None
    for n, r in enumerate((a_ref, b_ref, c_ref, d_ref)):
        part = jnp.dot(r[0], wo_ref[n * gw:(n + 1) * gw, :], preferred_element_type=F32)
        mix = part if mix is None else mix + part
    x1 = x_ref[0] + mod_ref[0, 2:3, :] * mix
    h2 = (_rms(x1) * g2_ref[...] * (1.0 + mod_ref[0, 4:5, :]) + mod_ref[0, 3:4, :]).astype(BF16)
    acc = None
    for n in range(D_FF // D_MODEL):
        cols = slice(n * D_MODEL, (n + 1) * D_MODEL)
        hid = jnp.maximum(jnp.dot(h2, w1_ref[:, cols], preferred_element_type=F32), 0.0)
        part = jnp.dot((hid * hid).astype(BF16), w2_ref[cols, :], preferred_element_type=F32)
        acc = part if acc is None else acc + part
    o_ref[0] = x1 + mod_ref[0, 5:6, :] * acc


def _out_mlp(xs, outs, mod, g2, w_out, w1, w2, ctx_tokens, tm=256):
    b = xs.shape[0]
    t_out = T_ALL if ctx_tokens else SEQ
    nt = t_out // tm
    ctx_row = mod.shape[0] - 1
    ctx_tile = SEQ // tm

    def mod_map(i, t):
        return (jnp.where(t >= ctx_tile, ctx_row, i), 0, 0)

    tile = lambda w: pl.BlockSpec((1, tm, w), lambda i, t: (i, t, 0))
    whole = lambda s: pl.BlockSpec(s, lambda i, t: (0, 0))
    return pl.pallas_call(
        _outmlp_kernel,
        out_shape=jax.ShapeDtypeStruct((b, t_out, D_MODEL), F32),
        grid=(b, nt),
        in_specs=[
            tile(D_MODEL), tile(2 * LANES), tile(2 * LANES), tile(2 * LANES), tile(2 * LANES),
            pl.BlockSpec((1, 6, D_MODEL), mod_map),
            whole((1, D_MODEL)), whole((D_MODEL, D_MODEL)), whole((D_MODEL, D_FF)), whole((D_FF, D_MODEL)),
        ],
        out_specs=tile(D_MODEL),
        compiler_params=_params(("parallel", "parallel")),
        name="out_mlp",
    )(xs, *outs, mod, g2, w_out, w1, w2)


def _rope_tables():
    pos = np.arange(SEQ)
    rows, cols = (pos // GRID_W).astype(np.float32), (pos % GRID_W).astype(np.float32)

    def pattern(half):
        freqs = (np.float32(ROPE_BASE) ** (-np.arange(half, dtype=np.float32) / np.float32(half))).astype(np.float32)
        ang = [(p[:, None] * freqs[None, :]).astype(np.float32) for p in (rows, cols)]
        cos = np.concatenate([np.cos(a) for a in (ang[0], ang[0], ang[1], ang[1])], axis=1)
        sin = np.concatenate([-np.sin(ang[0]), np.sin(ang[0]), -np.sin(ang[1]), np.sin(ang[1])], axis=1)
        reps = LANES // (4 * half)
        cos = np.concatenate([np.tile(cos, (1, reps)), np.ones((CTX_LEN, LANES))], axis=0)
        sin = np.concatenate([np.tile(sin, (1, reps)), np.zeros((CTX_LEN, LANES))], axis=0)
        return jnp.asarray(cos, F32), jnp.asarray(sin, F32)

    ca, sa = pattern(HEAD_DIM // 4)
    cb, sb = pattern(B_QK_DIM // 4)
    return ca, sa, cb, sb


def _gain_row(aq, ak, bq, bk, dq, dk):
    one = jnp.ones((LANES,), F32)
    t = lambda g, s=1.0: jnp.tile(g.astype(F32), LANES // g.shape[0]) * s
    groups = [one] * N_GROUPS
    groups[G_AQ] = groups[G_AQ + 1] = t(aq, HEAD_DIM ** -0.5)
    groups[G_AK] = t(ak)
    groups[G_BQ] = groups[G_BQ + 1] = t(bq, B_QK_DIM ** -0.5 * math.log2(math.e))
    groups[G_BK] = groups[G_BK + 1] = t(bk)
    groups[G_CK] = groups[G_CK + 1] = one * HEAD_DIM ** -0.5
    groups[G_DQ] = groups[G_DQ + 1] = t(dq, HEAD_DIM ** -0.5)
    groups[G_DK] = groups[G_DK + 1] = t(dk)
    return jnp.concatenate(groups)[None, :]


def _nbr_bias(rpb):
    assert D_GROUP == NA_ROWS // 2
    qc = np.arange(GRID_W)[:, None]
    kc = np.arange(GRID_W)[None, :]
    cstart = np.clip(qc - NA_COLS // 2, 0, GRID_W - NA_COLS)
    valid = (kc >= cstart) & (kc < cstart + NA_COLS)
    pad = GRID_W - NA_COLS
    rp = jnp.pad(rpb.astype(F32), ((0, 0), (0, 0), (pad, pad)))
    period = rp.shape[-1]
    flat = jnp.tile(rp, (1, 1, GRID_W + 1))[..., :GRID_W * (period + 1)]
    shifted = flat.reshape(rp.shape[0], rp.shape[1], GRID_W, period + 1)
    toep = shifted[:, :, ::-1, :GRID_W]
    toep = jnp.where(valid[None, None], toep, NEG_INF).transpose(0, 2, 1, 3)
    h = rpb.shape[0]
    half = NA_ROWS // 2

    def rows_of(dr0, offset):
        blk = toep[:, :, dr0:dr0 + NA_ROWS]
        blk = jnp.pad(blk, ((0, 0), (0, 0), (offset, D_GROUP - offset), (0, 0)), constant_values=NEG_INF)
        return blk.reshape(h, GRID_W, (NA_ROWS + D_GROUP) * GRID_W)

    kinds = [[rows_of(NA_ROWS - 1 - a, 0) for a in range(D_GROUP)],
             [rows_of(half - 1, a) for a in range(D_GROUP)],
             [rows_of(half - 1 - a, D_GROUP) for a in range(D_GROUP)]]
    b = jnp.stack([jnp.stack(k, axis=1) for k in kinds], axis=1)
    b = b.reshape(h // 2, 2, 3, D_GROUP * GRID_W, -1).transpose(0, 2, 1, 3, 4)
    return b.reshape(h // 2, 3, 2 * D_GROUP * GRID_W, -1)


def kernel(x, c, ctx, c_ctx, w_mod, b_mod, norm1_g, norm2_g, w_in, w_out, a_qnorm_g, a_knorm_g, a_sink, b_qnorm_g, b_knorm_g, b_lambda_q1, b_lambda_k1, b_lambda_q2, b_lambda_k2, b_subln_g, c_decay_fwd, c_decay_bwd, c_gn_g, d_qnorm_g, d_knorm_g, d_rpb, w_mlp1, w_mlp2):
    depth = w_mod.shape[0]
    b = x.shape[0]
    assert b + 1 <= 8
    cc = jnp.concatenate([c, c_ctx[None, :], jnp.zeros((8 - b - 1, D_MODEL), F32)], axis=0)
    mods = _modulation(cc, w_mod, b_mod)[:, :b + 1].reshape(depth, b + 1, 6, D_MODEL)
    tabs = _rope_tables()
    xs = jnp.concatenate([x, ctx], axis=1)

    for layer in range(depth):
        need_ctx = layer < depth - 1
        lambda_init = 0.8 - 0.6 * math.exp(-0.3 * layer)
        gain = _gain_row(a_qnorm_g[layer], a_knorm_g[layer], b_qnorm_g[layer], b_knorm_g[layer],
                         d_qnorm_g[layer], d_knorm_g[layer])
        qkv, gate, akv, kt_b, kt_c = _in_proj(xs, mods[layer], norm1_g[layer][None, :], w_in[layer].astype(BF16),
                                              gain, tabs)

        out_a = _attn_a(qkv, akv, a_sink[layer].astype(F32), need_ctx)
        lam_vecs = jnp.stack([b_lambda_q1[layer], b_lambda_k1[layer],
                              b_lambda_q2[layer], b_lambda_k2[layer]]).astype(F32)
        out_b = _attn_b(qkv, kt_b, lam_vecs, b_subln_g[layer][None, :], need_ctx, lambda_init)
        dec = jnp.stack([c_decay_fwd[layer].reshape(2, 2), c_decay_bwd[layer].reshape(2, 2)],
                        axis=1).reshape(2, 1, 4).astype(F32)
        out_c = _retention(qkv, kt_c, gate, dec, jnp.tile(c_gn_g[layer], 2)[None, :], need_ctx)
        out_d = _attn_d(qkv, _nbr_bias(d_rpb[layer]), need_ctx)

        xs = _out_mlp(xs, (out_a, out_b, out_c, out_d), mods[layer], norm2_g[layer][None, :],
                      w_out[layer].astype(BF16), w_mlp1[layer].astype(BF16), w_mlp2[layer].astype(BF16),
                      need_ctx)
    return xs
```

```python
import functools
import math

import numpy as np
import jax
import jax.numpy as jnp
from jax import lax
from jax.experimental import pallas as pl
from jax.experimental.pallas import tpu as pltpu

F32 = jnp.float32
BF16 = jnp.bfloat16

D_MODEL = 1024
SEQ = 4096
CTX_LEN = 256
T_ALL = SEQ + CTX_LEN
GRID_W = 64
HEAD_DIM = 64
B_QK_DIM = 32
BLK = 128
N_LAT_BLK = SEQ // BLK
N_ALL_BLK = T_ALL // BLK
B_KEY_CHUNK = 256
D_GROUP = 4
B_ROW_SLAB = 32
B_SAFE_BOUND = 60.0
B_BOUND_SLACK = 1.0 + 2.0 ** -10
WINDOW = 128
NA_ROWS = 8
NA_COLS = 16
D_FF = 4 * D_MODEL
IN_WIDTH = 3072
ROPE_BASE = 10000.0
EPS = 1e-6
NEG_INF = -1e30
LANES = 128
VMEM_LIMIT = 52 * 1024 * 1024

G_AQ, G_AK, G_AV = 0, 2, 3
G_BQ, G_BK, G_BV = 4, 6, 8
G_CQ, G_CK, G_CV, G_CG = 10, 12, 14, 16
G_DQ, G_DK, G_DV = 18, 20, 22
N_GROUPS = IN_WIDTH // LANES

_NT = (((1,), (1,)), ((), ()))
_TN = (((0,), (0,)), ((), ()))


def _params(sem):
    return pltpu.CompilerParams(dimension_semantics=sem, vmem_limit_bytes=VMEM_LIMIT)


def _rms(x):
    return x * lax.rsqrt(jnp.mean(x * x, axis=-1, keepdims=True) + EPS)


def _mod_kernel(c_ref, w_ref, b_ref, o_ref):
    c = c_ref[...]
    s = c / (1.0 + jnp.exp(-c))
    s_hi = s.astype(BF16)
    s_lo = (s - s_hi.astype(F32)).astype(BF16)
    w = w_ref[0]
    w_hi = w.astype(BF16)
    w_lo = (w - w_hi.astype(F32)).astype(BF16)
    acc = jnp.dot(s_hi, w_hi, preferred_element_type=F32)
    acc += jnp.dot(s_lo, w_hi, preferred_element_type=F32)
    acc += jnp.dot(s_hi, w_lo, preferred_element_type=F32)
    o_ref[0] = acc + b_ref[0]


def _modulation(cc, w_mod, b_mod):
    depth = w_mod.shape[0]
    n_col = 6 * D_MODEL // 1024
    return pl.pallas_call(
        _mod_kernel,
        out_shape=jax.ShapeDtypeStruct((depth, 8, 6 * D_MODEL), F32),
        grid=(depth, n_col),
        in_specs=[
            pl.BlockSpec((8, D_MODEL), lambda l, j: (0, 0)),
            pl.BlockSpec((1, D_MODEL, 1024), lambda l, j: (l, 0, j)),
            pl.BlockSpec((1, 1, 1024), lambda l, j: (l, 0, j)),
        ],
        out_specs=pl.BlockSpec((1, 8, 1024), lambda l, j: (l, 0, j)),
        compiler_params=_params(("parallel", "parallel")),
        name="adaln_mod",
    )(cc, w_mod, b_mod.reshape(depth, 1, 6 * D_MODEL))


def _seg_inv_rms(v, seg):
    sq = v * v
    hi = sq.astype(BF16)
    lo = (sq - hi.astype(F32)).astype(BF16)
    r = lax.broadcasted_iota(jnp.int32, (LANES, LANES), 0) // seg
    c = lax.broadcasted_iota(jnp.int32, (LANES, LANES), 1) // seg
    bd = jnp.where(r == c, 1.0, 0.0).astype(BF16)
    ssum = jnp.dot(hi, bd, preferred_element_type=F32) + jnp.dot(lo, bd, preferred_element_type=F32)
    return lax.rsqrt(ssum * (1.0 / seg) + EPS)


def _rotate(v, cos, sin_signed, half):
    up = pltpu.roll(v, LANES - half, 1)
    dn = pltpu.roll(v, half, 1)
    lane = lax.broadcasted_iota(jnp.int32, v.shape, 1)
    partner = jnp.where(lane % (2 * half) < half, up, dn)
    return v * cos + partner * sin_signed


def _stream_tile(refs, split_streams, tile_axis):
    if not split_streams:
        return refs[0][0], refs[1:]
    is_ctx = pl.program_id(tile_axis) == pl.num_programs(tile_axis) - 1
    return jnp.where(is_ctx, refs[1][0], refs[0][0]), refs[2:]


def _inproj_kernel(*refs, split_streams):
    x, refs = _stream_tile(refs, split_streams, 0)
    mod_ref, g1_ref, w_ref, gain_ref, ca_ref, sa_ref, cb_ref, sb_ref = refs[:8]
    qkv_ref, gate_ref, akv_ref, ktb_ref, ktc_ref = refs[8:]
    sh = mod_ref[0, 0:1, :]
    sc = mod_ref[0, 1:2, :]
    h = (_rms(x) * g1_ref[...] * (1.0 + sc) + sh).astype(BF16)
    acc = jnp.dot(h, w_ref[...], preferred_element_type=F32)
    tm = acc.shape[0]
    head0 = lax.broadcasted_iota(jnp.int32, (1, LANES), 1) < HEAD_DIM
    for g in range(N_GROUPS):
        cols = slice(g * LANES, (g + 1) * LANES)
        v = acc[:, cols]
        gain = gain_ref[:, cols]
        if g in (G_AQ, G_AQ + 1, G_AK):
            v = v * _seg_inv_rms(v, HEAD_DIM) * gain
            v = _rotate(v, ca_ref[...], sa_ref[...], HEAD_DIM // 4)
        elif g in (G_BQ, G_BQ + 1, G_BK, G_BK + 1):
            v = v * _seg_inv_rms(v, B_QK_DIM) * gain
            v = _rotate(v, cb_ref[...], sb_ref[...], B_QK_DIM // 4)
        elif g in (G_DQ, G_DQ + 1, G_DK, G_DK + 1):
            v = v * _seg_inv_rms(v, HEAD_DIM) * gain
        elif g in (G_CK, G_CK + 1):
            v = v * gain
        elif g in (G_CG, G_CG + 1):
            gcols = slice((g - G_CG) * LANES, (g - G_CG + 1) * LANES)
            gate_ref[0, :, gcols] = v / (1.0 + jnp.exp(-v))
        qkv_ref[0, :, cols] = v.astype(BF16)
        if g in (G_AK, G_AV):
            n = 2 * (0 if g == G_AK else 1)
            swapped = pltpu.roll(v, HEAD_DIM, 1)
            akv_ref[0, :, n * LANES:(n + 1) * LANES] = jnp.where(head0, v, swapped).astype(BF16)
            akv_ref[0, :, (n + 1) * LANES:(n + 2) * LANES] = jnp.where(head0, swapped, v).astype(BF16)
        elif g in (G_BK, G_BK + 1):
            r0 = (g - G_BK) * LANES
            ktb_ref[0, r0:r0 + LANES, :] = v.T.astype(BF16)
        elif g in (G_CK, G_CK + 1):
            r0 = (g - G_CK) * LANES
            for c in range(tm // BLK):
                ktc_ref[0, c, r0:r0 + LANES, :] = v[c * BLK:(c + 1) * BLK].T.astype(BF16)


def _stream_specs(split_streams, tm, order):
    if not split_streams:
        return [pl.BlockSpec((1, tm, D_MODEL), lambda *g: (order(*g)[0], order(*g)[1], 0))]
    last = SEQ // tm - 1
    return [pl.BlockSpec((1, tm, D_MODEL), lambda *g: (order(*g)[0], jnp.minimum(order(*g)[1], last), 0)),
            pl.BlockSpec((1, tm, D_MODEL), lambda *g: (order(*g)[0], 0, 0))]


def _in_proj(streams, mod, g1, w_in, gain, tabs, tm=CTX_LEN):
    b = streams[0].shape[0]
    nt = T_ALL // tm
    ctx_row = mod.shape[0] - 1

    def mod_map(t, i):
        return (jnp.where(t == nt - 1, ctx_row, i), 0, 0)

    tab_spec = pl.BlockSpec((tm, LANES), lambda t, i: (t, 0))
    split = len(streams) == 2
    return pl.pallas_call(
        functools.partial(_inproj_kernel, split_streams=split),
        out_shape=(jax.ShapeDtypeStruct((b, T_ALL, IN_WIDTH), BF16),
                   jax.ShapeDtypeStruct((b, T_ALL, 2 * LANES), F32),
                   jax.ShapeDtypeStruct((b, T_ALL, 4 * LANES), BF16),
                   jax.ShapeDtypeStruct((b, 2 * LANES, T_ALL), BF16),
                   jax.ShapeDtypeStruct((b, N_ALL_BLK, 2 * LANES, BLK), BF16)),
        grid=(nt, b),
        in_specs=_stream_specs(split, tm, lambda t, i: (i, t)) + [
            pl.BlockSpec((1, 6, D_MODEL), mod_map),
            pl.BlockSpec((1, D_MODEL), lambda t, i: (0, 0)),
            pl.BlockSpec((D_MODEL, IN_WIDTH), lambda t, i: (0, 0)),
            pl.BlockSpec((1, IN_WIDTH), lambda t, i: (0, 0)),
            tab_spec, tab_spec, tab_spec, tab_spec,
        ],
        out_specs=(pl.BlockSpec((1, tm, IN_WIDTH), lambda t, i: (i, t, 0)),
                   pl.BlockSpec((1, tm, 2 * LANES), lambda t, i: (i, t, 0)),
                   pl.BlockSpec((1, tm, 4 * LANES), lambda t, i: (i, t, 0)),
                   pl.BlockSpec((1, 2 * LANES, tm), lambda t, i: (i, 0, t)),
                   pl.BlockSpec((1, tm // BLK, 2 * LANES, BLK), lambda t, i: (i, t, 0, 0))),
        compiler_params=_params(("parallel", "parallel")),
        name="in_proj",
    )(*streams, mod, g1, w_in, gain, *tabs)


def _a_kernel(sink_ref, q_ref, k_ref, v_ref, o_ref, *, ctx_queries):
    p = pl.program_id(1)
    nq = 2 * BLK
    strip = 3 * BLK
    head0 = lax.broadcasted_iota(jnp.int32, (1, LANES), 1) < HEAD_DIM
    row = lax.broadcasted_iota(jnp.int32, (nq, 1), 0)
    sink_col = jnp.where(row < BLK, sink_ref[2 * p], sink_ref[2 * p + 1])
    kx = k_ref[0, SEQ:T_ALL, :]
    vx = v_ref[0, SEQ:T_ALL, :]
    q_in_blk = lax.broadcasted_iota(jnp.int32, (nq, strip), 0) % BLK
    k_in_strip = lax.broadcasted_iota(jnp.int32, (nq, strip), 1)

    def stacked(q):
        zq = jnp.zeros_like(q)
        return jnp.concatenate([jnp.where(head0, q, zq), jnp.where(head0, zq, q)], axis=0)

    def attend(parts):
        m = functools.reduce(jnp.maximum, [jnp.max(s, axis=1, keepdims=True) for s, _ in parts] + [sink_col])
        es = [jnp.exp(s - m) for s, _ in parts]
        den = sum(jnp.sum(e, axis=1, keepdims=True) for e in es) + jnp.exp(sink_col - m)
        o = sum(jnp.dot(e.astype(BF16), v, preferred_element_type=F32) for e, (_, v) in zip(es, parts)) / den
        return jnp.where(head0, o[:BLK], o[BLK:]).astype(BF16)

    def block(j, carry):
        q0 = pl.multiple_of(j * BLK, BLK)
        start = pl.multiple_of(jnp.clip(j - 1, 0, N_LAT_BLK - 3) * BLK, BLK)
        qq = stacked(q_ref[0, pl.ds(q0, BLK), :])
        ks = k_ref[0, pl.ds(start, strip), :]
        vs = v_ref[0, pl.ds(start, strip), :]
        s_win = lax.dot_general(qq, ks, _NT, preferred_element_type=F32)
        dist = (q0 + q_in_blk) - (start + k_in_strip)
        s_win = jnp.where(jnp.abs(dist) <= WINDOW, s_win, NEG_INF)
        s_ctx = lax.dot_general(qq, kx, _NT, preferred_element_type=F32)
        o_ref[0, pl.ds(q0, BLK), :] = attend([(s_ctx, vx), (s_win, vs)])
        return carry

    lax.fori_loop(0, N_LAT_BLK, block, 0, unroll=2)
    if ctx_queries:
        for q0 in range(SEQ, T_ALL, BLK):
            qq = stacked(q_ref[0, q0:q0 + BLK, :])
            s_ctx = lax.dot_general(qq, kx, _NT, preferred_element_type=F32)
            o_ref[0, q0:q0 + BLK, :] = attend([(s_ctx, vx)])


def _attn_a(qkv, akv, sink, ctx_queries):
    b = qkv.shape[0]
    t_out = T_ALL if ctx_queries else SEQ
    return pl.pallas_call(
        functools.partial(_a_kernel, ctx_queries=ctx_queries),
        out_shape=jax.ShapeDtypeStruct((b, t_out, 2 * LANES), BF16),
        grid=(b, 2),
        in_specs=[
            pl.BlockSpec(memory_space=pltpu.SMEM),
            pl.BlockSpec((1, T_ALL, LANES), lambda i, p: (i, 0, G_AQ + p)),
            pl.BlockSpec((1, T_ALL, LANES), lambda i, p: (i, 0, p)),
            pl.BlockSpec((1, T_ALL, LANES), lambda i, p: (i, 0, 2 + p)),
        ],
        out_specs=pl.BlockSpec((1, t_out, LANES), lambda i, p: (i, 0, p)),
        compiler_params=_params(("parallel", "parallel")),
        name="attn_window",
    )(sink, qkv, akv, akv)


def _b_kernel(lam_ref, subg_ref, q_ref, kt_ref, v_ref, o_ref, qq_ref, m_ref, l_ref, acc_ref, s_ref,
              *, ctx_queries, lambda_init):
    lv = lam_ref[...]
    lam = (jnp.exp(jnp.sum(lv[0:1] * lv[1:2], axis=1, keepdims=True))
           - jnp.exp(jnp.sum(lv[2:3] * lv[3:4], axis=1, keepdims=True)) + lambda_init)
    lane = lax.broadcasted_iota(jnp.int32, (1, LANES), 1)
    rows = 4 * BLK
    mine = [(lane >= n * B_QK_DIM) & (lane < (n + 1) * B_QK_DIM) for n in range(4)]

    ksq = kt_ref[0].astype(F32)
    ksq = ksq * ksq
    k_max2 = [jnp.max(jnp.sum(ksq[n * B_QK_DIM:(n + 1) * B_QK_DIM], axis=0, keepdims=True), axis=1, keepdims=True)
              for n in range(4)]

    def chunks_of(k0, nk):
        return [(c, min(B_KEY_CHUNK, k0 + nk - c)) for c in range(k0, k0 + nk, B_KEY_CHUNK)]

    def tiles(t):
        return [t[:, c:c + LANES] for c in range(0, t.shape[1], LANES)]

    def row_norm2(blk, top):
        qf = q_ref[0, pl.ds(pl.multiple_of(blk * BLK, BLK), BLK), :].astype(F32)
        return jnp.maximum(top, jnp.sum(qf * qf, axis=1, keepdims=True))

    n_blk = N_ALL_BLK if ctx_queries else N_LAT_BLK
    q_max2 = jnp.max(lax.fori_loop(0, n_blk, row_norm2, jnp.zeros((BLK, 1), F32), unroll=True),
                     axis=0, keepdims=True)
    loose = jnp.sqrt(q_max2 * functools.reduce(jnp.maximum, k_max2)) * B_BOUND_SLACK
    usable = jnp.max(loose) < B_SAFE_BOUND

    def prepare(blk, par, with_bound):
        q = q_ref[0, pl.ds(pl.multiple_of(blk * BLK, BLK), BLK), :]
        qf = q.astype(F32)
        sq = qf * qf
        for n in range(4):
            qq_ref[par, n * BLK:(n + 1) * BLK] = jnp.where(mine[n], q, jnp.zeros_like(q))
            if with_bound:
                q_norm2 = jnp.sum(jnp.where(mine[n], sq, 0.0), axis=1, keepdims=True)
                bound = jnp.sqrt(q_norm2 * k_max2[n]) * B_BOUND_SLACK
                m_ref[par, n * BLK:(n + 1) * BLK] = jnp.broadcast_to(bound, (BLK, LANES))

    def prob_chunk(par, n, c0, cn, s):
        parts = []
        for r in range(0, rows, B_ROW_SLAB):
            m = m_ref[par, r:r + B_ROW_SLAB]
            es = [jnp.exp2(t - m) for t in tiles(s[r:r + B_ROW_SLAB])]
            t = functools.reduce(jnp.add, es)
            l_ref[par, r:r + B_ROW_SLAB] = t if n == 0 else l_ref[par, r:r + B_ROW_SLAB] + t
            parts.append(jnp.concatenate(es, axis=1).astype(BF16))
        part = jnp.dot(jnp.concatenate(parts, axis=0), v_ref[0, c0:c0 + cn, :],
                       preferred_element_type=F32)
        acc_ref[par] = part if n == 0 else acc_ref[par] + part

    def finish(blk, par):
        o = acc_ref[par] * (1.0 / jnp.sum(l_ref[par], axis=1, keepdims=True))
        outs = []
        for hh in range(2):
            r0 = 2 * hh * BLK
            oh = o[r0:r0 + BLK] - lam * o[r0 + BLK:r0 + 2 * BLK]
            oh = oh[:, hh * HEAD_DIM:(hh + 1) * HEAD_DIM]
            outs.append(_rms(oh) * subg_ref[...] * (1.0 - lambda_init))
        o_ref[0, pl.ds(pl.multiple_of(blk * BLK, BLK), BLK), :] = jnp.concatenate(outs, axis=1).astype(BF16)

    def one_sweep(blk, par, chunks):
        for n, (c0, cn) in enumerate(chunks):
            s = jnp.dot(qq_ref[par], kt_ref[0, :, c0:c0 + cn], preferred_element_type=F32)
            prob_chunk(par, n, c0, cn, s)
        finish(blk, par)

    def two_sweeps(blk, par, chunks):
        row_max = None
        for c0, cn in chunks:
            s = jnp.dot(qq_ref[par], kt_ref[0, :, c0:c0 + cn], preferred_element_type=F32)
            s_ref[:, c0:c0 + cn] = s
            t = functools.reduce(jnp.maximum, tiles(s))
            row_max = t if row_max is None else jnp.maximum(row_max, t)
        m_ref[par] = jnp.broadcast_to(jnp.max(row_max, axis=1, keepdims=True), (rows, LANES))
        for n, (c0, cn) in enumerate(chunks):
            prob_chunk(par, n, c0, cn, s_ref[:, c0:c0 + cn])
        finish(blk, par)

    def block_pair(first_blk, chunks, fast):
        for par in range(2):
            prepare(first_blk + par, par, fast)
        for par in range(2):
            (one_sweep if fast else two_sweeps)(first_blk + par, par, chunks)

    def all_blocks(fast):
        def body(n, carry):
            block_pair(2 * n, chunks_of(0, T_ALL), fast)
            return carry

        lax.fori_loop(0, N_LAT_BLK // 2, body, 0)
        if ctx_queries:
            block_pair(N_LAT_BLK, chunks_of(SEQ, CTX_LEN), fast)

    pl.when(usable)(lambda: all_blocks(True))
    pl.when(jnp.logical_not(usable))(lambda: all_blocks(False))


def _attn_b(qkv, kt, lam_vecs, subg, ctx_queries, lambda_init):
    b = qkv.shape[0]
    t_out = T_ALL if ctx_queries else SEQ
    assert N_LAT_BLK % 2 == 0 and N_ALL_BLK - N_LAT_BLK == 2
    stacked = lambda dt: pltpu.VMEM((2, 4 * BLK, LANES), dt)
    return pl.pallas_call(
        functools.partial(_b_kernel, ctx_queries=ctx_queries, lambda_init=lambda_init),
        out_shape=jax.ShapeDtypeStruct((b, t_out, 2 * LANES), BF16),
        grid=(b, 2),
        in_specs=[
            pl.BlockSpec((4, B_QK_DIM), lambda i, p: (0, 0)),
            pl.BlockSpec((1, HEAD_DIM), lambda i, p: (0, 0)),
            pl.BlockSpec((1, T_ALL, LANES), lambda i, p: (i, 0, G_BQ + p)),
            pl.BlockSpec((1, LANES, T_ALL), lambda i, p: (i, p, 0)),
            pl.BlockSpec((1, T_ALL, LANES), lambda i, p: (i, 0, G_BV + p)),
        ],
        out_specs=pl.BlockSpec((1, t_out, LANES), lambda i, p: (i, 0, p)),
        scratch_shapes=[stacked(BF16), stacked(F32), stacked(F32), stacked(F32),
                        pltpu.VMEM((4 * BLK, T_ALL), F32)],
        compiler_params=_params(("parallel", "parallel")),
        name="attn_diff",
    )(lam_vecs, subg, qkv, kt, qkv)


def _c_kernel(dec_ref, gn_ref, q_ref, kt_ref, v_ref, g_ref, o_ref, oi_ref, kv_ref, st_ref, tab_ref,
              *, ctx_out):
    n_ctx = CTX_LEN // BLK
    lane = lax.broadcasted_iota(jnp.int32, (1, LANES), 1)
    rowi = lax.broadcasted_iota(jnp.int32, (BLK, 1), 0)
    head0_l = lane < HEAD_DIM
    head0_r = rowi < HEAD_DIM
    same_head = head0_r == head0_l
    rel = (lax.broadcasted_iota(jnp.int32, (BLK, BLK), 0)
           - lax.broadcasted_iota(jnp.int32, (BLK, BLK), 1)).astype(F32)
    pos_r = rowi.astype(F32)
    pos_l = lane.astype(F32)

    def log_gamma(col):
        x = dec_ref[0, :, col:col + 1]
        return jnp.minimum(x, 0.0) - jnp.log1p(jnp.exp(-jnp.abs(x)))

    lg = [log_gamma(n) for n in range(4)]
    lg_l = [jnp.where(head0_l, lg[2 * d], lg[2 * d + 1]) for d in range(2)]
    lg_r = [jnp.where(head0_r, lg[2 * d], lg[2 * d + 1]) for d in range(2)]
    for hh in range(2):
        tab_ref[hh] = (jnp.where(rel >= 0, jnp.exp(rel * lg[hh]), 0.0)
                       + jnp.where(rel <= 0, jnp.exp(-rel * lg[2 + hh]), 0.0))
    tab_ref[2] = jnp.exp((pos_r + 1.0) * lg_l[0])
    tab_ref[3] = jnp.exp((BLK - pos_r) * lg_l[1])
    tab_ref[4] = jnp.exp((BLK - 1.0 - pos_l) * lg_r[0])
    tab_ref[5] = jnp.exp(pos_l * lg_r[1])
    chunk_decay = [jnp.exp(BLK * lg_l[d]) for d in range(2)]

    def local(c, carry):
        r0 = pl.multiple_of(c * BLK, BLK)
        q = q_ref[0, pl.ds(r0, BLK), :]
        kt = kt_ref[0, c]
        v = v_ref[0, pl.ds(r0, BLK), :]
        zq, zv = jnp.zeros_like(q), jnp.zeros_like(v)
        qq = jnp.concatenate([jnp.where(head0_l, q, zq), jnp.where(head0_l, zq, q)], axis=0)
        p = jnp.dot(qq, kt, preferred_element_type=F32)
        inner = jnp.concatenate([p[:BLK] * tab_ref[0], p[BLK:] * tab_ref[1]], axis=1).astype(BF16)
        vv = jnp.concatenate([jnp.where(head0_l, v, zv), jnp.where(head0_l, zv, v)], axis=0)
        oi_ref[pl.ds(r0, BLK), :] = jnp.dot(inner, vv, preferred_element_type=F32)
        ktf = kt.astype(F32)
        kk = jnp.concatenate([ktf * tab_ref[4], ktf * tab_ref[5]], axis=0).astype(BF16)
        kv = jnp.dot(kk, v, preferred_element_type=F32)
        kv_ref[c, :BLK] = jnp.where(same_head, kv[:BLK], 0.0)
        kv_ref[c, BLK:] = jnp.where(same_head, kv[BLK:], 0.0)
        return carry

    lax.fori_loop(0, N_ALL_BLK, local, 0, unroll=4)

    def scan(i, states):
        sf, sb = states
        cf = jnp.where(i < n_ctx, N_LAT_BLK + i, i - n_ctx)
        cb = N_ALL_BLK - 1 - i
        st_ref[cf, :BLK] = sf.astype(BF16)
        st_ref[cb, BLK:] = sb.astype(BF16)
        return sf * chunk_decay[0] + kv_ref[cf, :BLK], sb * chunk_decay[1] + kv_ref[cb, BLK:]

    zero = jnp.zeros((BLK, LANES), F32)
    lax.fori_loop(0, N_ALL_BLK, scan, (zero, zero))

    def finish(c, carry):
        r0 = pl.multiple_of(c * BLK, BLK)
        qf = q_ref[0, pl.ds(r0, BLK), :].astype(F32)
        qd = jnp.concatenate([qf * tab_ref[2], qf * tab_ref[3]], axis=1).astype(BF16)
        tot = oi_ref[pl.ds(r0, BLK), :] + jnp.dot(qd, st_ref[c], preferred_element_type=F32)
        sq = tot * tot
        ms0 = jnp.sum(jnp.where(head0_l, sq, 0.0), axis=1, keepdims=True) * (1.0 / HEAD_DIM)
        ms1 = jnp.sum(jnp.where(head0_l, 0.0, sq), axis=1, keepdims=True) * (1.0 / HEAD_DIM)
        inv = jnp.where(head0_l, lax.rsqrt(ms0 + EPS), lax.rsqrt(ms1 + EPS))
        o_ref[0, pl.ds(r0, BLK), :] = (tot * inv * gn_ref[...] * g_ref[0, pl.ds(r0, BLK), :]).astype(BF16)
        return carry

    lax.fori_loop(0, N_ALL_BLK if ctx_out else N_LAT_BLK, finish, 0, unroll=4)


def _retention(qkv, kt, gate, dec, gn, ctx_out):
    b = qkv.shape[0]
    t_out = T_ALL if ctx_out else SEQ
    col = lambda g: pl.BlockSpec((1, T_ALL, LANES), lambda i, p: (i, 0, g + p))
    return pl.pallas_call(
        functools.partial(_c_kernel, ctx_out=ctx_out),
        out_shape=jax.ShapeDtypeStruct((b, t_out, 2 * LANES), BF16),
        grid=(b, 2),
        in_specs=[
            pl.BlockSpec((1, 1, 4), lambda i, p: (p, 0, 0)),
            pl.BlockSpec((1, LANES), lambda i, p: (0, 0)),
            col(G_CQ),
            pl.BlockSpec((1, N_ALL_BLK, LANES, BLK), lambda i, p: (i, 0, p, 0)),
            col(G_CV),
            pl.BlockSpec((1, T_ALL, LANES), lambda i, p: (i, 0, p)),
        ],
        out_specs=pl.BlockSpec((1, t_out, LANES), lambda i, p: (i, 0, p)),
        scratch_shapes=[pltpu.VMEM((T_ALL, LANES), F32), pltpu.VMEM((N_ALL_BLK, 2 * BLK, LANES), F32),
                        pltpu.VMEM((N_ALL_BLK, 2 * BLK, LANES), BF16), pltpu.VMEM((6, BLK, BLK), F32)],
        compiler_params=_params(("parallel", "parallel")),
        name="retention",
    )(dec, gn, qkv, kt, qkv, gate)


def _d_kernel(q_ref, k_ref, v_ref, bias_ref, o_ref, *, ctx_queries):
    rows = SEQ // GRID_W
    n_grp = rows // D_GROUP
    gq = D_GROUP * GRID_W
    slab_rows = NA_ROWS + D_GROUP
    kx = k_ref[0, SEQ:T_ALL, :]
    vx = v_ref[0, SEQ:T_ALL, :]
    head0 = lax.broadcasted_iota(jnp.int32, (1, LANES), 1) < HEAD_DIM

    def attend(q, parts):
        n = q.shape[0]
        zq = jnp.zeros_like(q)
        qq = jnp.concatenate([jnp.where(head0, q, zq), jnp.where(head0, zq, q)], axis=0)
        scores = []
        for k, _, bias in parts:
            s = lax.dot_general(qq, k, _NT, preferred_element_type=F32)
            scores.append(s if bias is None else s + bias)
        m = functools.reduce(jnp.maximum, [jnp.max(s, axis=1, keepdims=True) for s in scores])
        es = [jnp.exp(s - m) for s in scores]
        den = sum(jnp.sum(e, axis=1, keepdims=True) for e in es)
        o = sum(jnp.dot(e.astype(BF16), v, preferred_element_type=F32)
                for e, (_, v, _) in zip(es, parts)) / den
        return jnp.where(head0, o[:n], o[n:]).astype(BF16)

    def group(g, carry):
        first_key_row = jnp.clip(g * D_GROUP - NA_ROWS // 2, 0, rows - slab_rows)
        start = pl.multiple_of(first_key_row * GRID_W, GRID_W)
        kind = jnp.where(g == 0, 0, jnp.where(g == n_grp - 1, 2, 1))
        q0 = pl.multiple_of(g * gq, gq)
        q = q_ref[0, pl.ds(q0, gq), :]
        ks = k_ref[0, pl.ds(start, slab_rows * GRID_W), :]
        vs = v_ref[0, pl.ds(start, slab_rows * GRID_W), :]
        o_ref[0, pl.ds(q0, gq), :] = attend(q, [(ks, vs, bias_ref[0, kind]), (kx, vx, None)])
        return carry

    lax.fori_loop(0, n_grp, group, 0, unroll=2)
    if ctx_queries:
        o_ref[0, SEQ:T_ALL, :] = attend(q_ref[0, SEQ:T_ALL, :], [(kx, vx, None)])


def _attn_d(qkv, bias, ctx_queries):
    b = qkv.shape[0]
    t_out = T_ALL if ctx_queries else SEQ
    col = lambda g: pl.BlockSpec((1, T_ALL, LANES), lambda i, p: (i, 0, g + p))
    return pl.pallas_call(
        functools.partial(_d_kernel, ctx_queries=ctx_queries),
        out_shape=jax.ShapeDtypeStruct((b, t_out, 2 * LANES), BF16),
        grid=(b, 2),
        in_specs=[
            col(G_DQ), col(G_DK), col(G_DV),
            pl.BlockSpec((1,) + bias.shape[1:], lambda i, p: (p, 0, 0, 0)),
        ],
        out_specs=pl.BlockSpec((1, t_out, LANES), lambda i, p: (i, 0, p)),
        compiler_params=_params(("parallel", "parallel")),
        name="attn_nbr",
    )(qkv, qkv, qkv, bias)


def _outmlp_kernel(*refs, split_streams):
    x, refs = _stream_tile(refs, split_streams, 1)
    a_ref, b_ref, c_ref, d_ref, mod_ref, g2_ref, wo_ref, w1_ref, w2_ref, o_ref = refs
    gw = 2 * LANES
    mix = None
    for n, r in enumerate((a_ref, b_ref, c_ref, d_ref)):
        part = jnp.dot(r[0], wo_ref[n * gw:(n + 1) * gw, :], preferred_element_type=F32)
        mix = part if mix is None else mix + part
    x1 = x + mod_ref[0, 2:3, :] * mix
    h2 = (_rms(x1) * g2_ref[...] * (1.0 + mod_ref[0, 4:5, :]) + mod_ref[0, 3:4, :]).astype(BF16)
    acc = None
    for n in range(D_FF // D_MODEL):
        cols = slice(n * D_MODEL, (n + 1) * D_MODEL)
        hid = jnp.maximum(jnp.dot(h2, w1_ref[:, cols], preferred_element_type=F32), 0.0)
        part = jnp.dot((hid * hid).astype(BF16), w2_ref[cols, :], preferred_element_type=F32)
        acc = part if acc is None else acc + part
    o_ref[0] = x1 + mod_ref[0, 5:6, :] * acc


def _out_mlp(streams, outs, mod, g2, w_out, w1, w2, ctx_tokens, tm):
    b = streams[0].shape[0]
    split = len(streams) == 2
    assert ctx_tokens or not split
    t_out = T_ALL if ctx_tokens else SEQ
    nt = t_out // tm
    ctx_row = mod.shape[0] - 1
    ctx_tile = SEQ // tm

    def mod_map(i, t):
        return (jnp.where(t >= ctx_tile, ctx_row, i), 0, 0)

    tile = lambda w: pl.BlockSpec((1, tm, w), lambda i, t: (i, t, 0))
    whole = lambda s: pl.BlockSpec(s, lambda i, t: (0, 0), pipeline_mode=pl.Buffered(1))
    return pl.pallas_call(
        functools.partial(_outmlp_kernel, split_streams=split),
        out_shape=jax.ShapeDtypeStruct((b, t_out, D_MODEL), F32),
        grid=(b, nt),
        in_specs=_stream_specs(split, tm, lambda i, t: (i, t)) + [
            tile(2 * LANES), tile(2 * LANES), tile(2 * LANES), tile(2 * LANES),
            pl.BlockSpec((1, 6, D_MODEL), mod_map),
            whole((1, D_MODEL)), whole((D_MODEL, D_MODEL)), whole((D_MODEL, D_FF)), whole((D_FF, D_MODEL)),
        ],
        out_specs=tile(D_MODEL),
        compiler_params=_params(("parallel", "parallel")),
        name="out_mlp",
    )(*streams, *outs, mod, g2, w_out, w1, w2)


def _rope_tables():
    pos = np.arange(SEQ)
    rows, cols = (pos // GRID_W).astype(np.float32), (pos % GRID_W).astype(np.float32)

    def pattern(half):
        freqs = (np.float32(ROPE_BASE) ** (-np.arange(half, dtype=np.float32) / np.float32(half))).astype(np.float32)
        ang = [(p[:, None] * freqs[None, :]).astype(np.float32) for p in (rows, cols)]
        cos = np.concatenate([np.cos(a) for a in (ang[0], ang[0], ang[1], ang[1])], axis=1)
        sin = np.concatenate([-np.sin(ang[0]), np.sin(ang[0]), -np.sin(ang[1]), np.sin(ang[1])], axis=1)
        reps = LANES // (4 * half)
        cos = np.concatenate([np.tile(cos, (1, reps)), np.ones((CTX_LEN, LANES))], axis=0)
        sin = np.concatenate([np.tile(sin, (1, reps)), np.zeros((CTX_LEN, LANES))], axis=0)
        return jnp.asarray(cos, F32), jnp.asarray(sin, F32)

    ca, sa = pattern(HEAD_DIM // 4)
    cb, sb = pattern(B_QK_DIM // 4)
    return ca, sa, cb, sb


def _gain_row(aq, ak, bq, bk, dq, dk):
    one = jnp.ones((LANES,), F32)
    t = lambda g, s=1.0: jnp.tile(g.astype(F32), LANES // g.shape[0]) * s
    groups = [one] * N_GROUPS
    groups[G_AQ] = groups[G_AQ + 1] = t(aq, HEAD_DIM ** -0.5)
    groups[G_AK] = t(ak)
    groups[G_BQ] = groups[G_BQ + 1] = t(bq, B_QK_DIM ** -0.5 * math.log2(math.e))
    groups[G_BK] = groups[G_BK + 1] = t(bk)
    groups[G_CK] = groups[G_CK + 1] = one * HEAD_DIM ** -0.5
    groups[G_DQ] = groups[G_DQ + 1] = t(dq, HEAD_DIM ** -0.5)
    groups[G_DK] = groups[G_DK + 1] = t(dk)
    return jnp.concatenate(groups)[None, :]


def _nbr_bias(rpb):
    assert D_GROUP == NA_ROWS // 2
    qc = np.arange(GRID_W)[:, None]
    kc = np.arange(GRID_W)[None, :]
    cstart = np.clip(qc - NA_COLS // 2, 0, GRID_W - NA_COLS)
    valid = (kc >= cstart) & (kc < cstart + NA_COLS)
    pad = GRID_W - NA_COLS
    rp = jnp.pad(rpb.astype(F32), ((0, 0), (0, 0), (pad, pad)))
    period = rp.shape[-1]
    flat = jnp.tile(rp, (1, 1, GRID_W + 1))[..., :GRID_W * (period + 1)]
    shifted = flat.reshape(rp.shape[0], rp.shape[1], GRID_W, period + 1)
    toep = shifted[:, :, ::-1, :GRID_W]
    toep = jnp.where(valid[None, None], toep, NEG_INF).transpose(0, 2, 1, 3)
    h = rpb.shape[0]
    half = NA_ROWS // 2

    def rows_of(dr0, offset):
        blk = toep[:, :, dr0:dr0 + NA_ROWS]
        blk = jnp.pad(blk, ((0, 0), (0, 0), (offset, D_GROUP - offset), (0, 0)), constant_values=NEG_INF)
        return blk.reshape(h, GRID_W, (NA_ROWS + D_GROUP) * GRID_W)

    kinds = [[rows_of(NA_ROWS - 1 - a, 0) for a in range(D_GROUP)],
             [rows_of(half - 1, a) for a in range(D_GROUP)],
             [rows_of(half - 1 - a, D_GROUP) for a in range(D_GROUP)]]
    b = jnp.stack([jnp.stack(k, axis=1) for k in kinds], axis=1)
    b = b.reshape(h // 2, 2, 3, D_GROUP * GRID_W, -1).transpose(0, 2, 1, 3, 4)
    return b.reshape(h // 2, 3, 2 * D_GROUP * GRID_W, -1)


def kernel(x, c, ctx, c_ctx, w_mod, b_mod, norm1_g, norm2_g, w_in, w_out, a_qnorm_g, a_knorm_g, a_sink, b_qnorm_g, b_knorm_g, b_lambda_q1, b_lambda_k1, b_lambda_q2, b_lambda_k2, b_subln_g, c_decay_fwd, c_decay_bwd, c_gn_g, d_qnorm_g, d_knorm_g, d_rpb, w_mlp1, w_mlp2):
    depth = w_mod.shape[0]
    b = x.shape[0]
    assert b + 1 <= 8
    cc = jnp.concatenate([c, c_ctx[None, :], jnp.zeros((8 - b - 1, D_MODEL), F32)], axis=0)
    mods = _modulation(cc, w_mod, b_mod)[:, :b + 1].reshape(depth, b + 1, 6, D_MODEL)
    tabs = _rope_tables()
    streams = (x, ctx)

    for layer in range(depth):
        need_ctx = layer < depth - 1
        lambda_init = 0.8 - 0.6 * math.exp(-0.3 * layer)
        gain = _gain_row(a_qnorm_g[layer], a_knorm_g[layer], b_qnorm_g[layer], b_knorm_g[layer],
                         d_qnorm_g[layer], d_knorm_g[layer])
        qkv, gate, akv, kt_b, kt_c = _in_proj(streams, mods[layer], norm1_g[layer][None, :],
                                              w_in[layer].astype(BF16), gain, tabs)

        out_a = _attn_a(qkv, akv, a_sink[layer].astype(F32), need_ctx)
        lam_vecs = jnp.stack([b_lambda_q1[layer], b_lambda_k1[layer],
                              b_lambda_q2[layer], b_lambda_k2[layer]]).astype(F32)
        out_b = _attn_b(qkv, kt_b, lam_vecs, b_subln_g[layer][None, :], need_ctx, lambda_init)
        dec = jnp.stack([c_decay_fwd[layer].reshape(2, 2), c_decay_bwd[layer].reshape(2, 2)],
                        axis=1).reshape(2, 1, 4).astype(F32)
        out_c = _retention(qkv, kt_c, gate, dec, jnp.tile(c_gn_g[layer], 2)[None, :], need_ctx)
        out_d = _attn_d(qkv, _nbr_bias(d_rpb[layer]), need_ctx)

        streams = (_out_mlp(streams, (out_a, out_b, out_c, out_d), mods[layer], norm2_g[layer][None, :],
                            w_out[layer].astype(BF16), w_mlp1[layer].astype(BF16), w_mlp2[layer].astype(BF16),
                            need_ctx, tm=CTX_LEN if need_ctx else 2 * CTX_LEN),)
    return streams[0]
```

```python
import functools
import math

import numpy as np
import jax
import jax.numpy as jnp
from jax import lax
from jax.experimental import pallas as pl
from jax.experimental.pallas import tpu as pltpu

F32 = jnp.float32
BF16 = jnp.bfloat16

D_MODEL = 1024
SEQ = 4096
CTX_LEN = 256
T_ALL = SEQ + CTX_LEN
GRID_W = 64
HEAD_DIM = 64
B_QK_DIM = 32
BLK = 128
N_LAT_BLK = SEQ // BLK
N_ALL_BLK = T_ALL // BLK
B_KEY_CHUNK = 256
D_GROUP = 4
B_ROW_SLAB = 32
B_SAFE_BOUND = 60.0
B_BOUND_SLACK = 1.0 + 2.0 ** -10
WINDOW = 128
NA_ROWS = 8
NA_COLS = 16
D_FF = 4 * D_MODEL
IN_WIDTH = 3072
ROPE_BASE = 10000.0
EPS = 1e-6
NEG_INF = -1e30
LANES = 128
VMEM_LIMIT = 52 * 1024 * 1024

G_AQ, G_AK, G_AV = 0, 2, 3
G_BQ, G_BK, G_BV = 4, 6, 8
G_CQ, G_CK, G_CV, G_CG = 10, 12, 14, 16
G_DQ, G_DK, G_DV = 18, 20, 22
N_GROUPS = IN_WIDTH // LANES

_NT = (((1,), (1,)), ((), ()))
_TN = (((0,), (0,)), ((), ()))


def _params(sem):
    return pltpu.CompilerParams(dimension_semantics=sem, vmem_limit_bytes=VMEM_LIMIT)


def _rms(x):
    return x * lax.rsqrt(jnp.mean(x * x, axis=-1, keepdims=True) + EPS)


def _mod_kernel(c_ref, w_ref, b_ref, o_ref):
    c = c_ref[...]
    s = c / (1.0 + jnp.exp(-c))
    s_hi = s.astype(BF16)
    s_lo = (s - s_hi.astype(F32)).astype(BF16)
    w = w_ref[0]
    w_hi = w.astype(BF16)
    w_lo = (w - w_hi.astype(F32)).astype(BF16)
    acc = jnp.dot(s_hi, w_hi, preferred_element_type=F32)
    acc += jnp.dot(s_lo, w_hi, preferred_element_type=F32)
    acc += jnp.dot(s_hi, w_lo, preferred_element_type=F32)
    o_ref[0] = acc + b_ref[0]


def _modulation(cc, w_mod, b_mod):
    depth = w_mod.shape[0]
    n_col = 6 * D_MODEL // 1024
    return pl.pallas_call(
        _mod_kernel,
        out_shape=jax.ShapeDtypeStruct((depth, 8, 6 * D_MODEL), F32),
        grid=(depth, n_col),
        in_specs=[
            pl.BlockSpec((8, D_MODEL), lambda l, j: (0, 0)),
            pl.BlockSpec((1, D_MODEL, 1024), lambda l, j: (l, 0, j)),
            pl.BlockSpec((1, 1, 1024), lambda l, j: (l, 0, j)),
        ],
        out_specs=pl.BlockSpec((1, 8, 1024), lambda l, j: (l, 0, j)),
        compiler_params=_params(("parallel", "parallel")),
        name="adaln_mod",
    )(cc, w_mod, b_mod.reshape(depth, 1, 6 * D_MODEL))


def _seg_inv_rms(v, seg):
    sq = v * v
    hi = sq.astype(BF16)
    lo = (sq - hi.astype(F32)).astype(BF16)
    r = lax.broadcasted_iota(jnp.int32, (LANES, LANES), 0) // seg
    c = lax.broadcasted_iota(jnp.int32, (LANES, LANES), 1) // seg
    bd = jnp.where(r == c, 1.0, 0.0).astype(BF16)
    ssum = jnp.dot(hi, bd, preferred_element_type=F32) + jnp.dot(lo, bd, preferred_element_type=F32)
    return lax.rsqrt(ssum * (1.0 / seg) + EPS)


def _rotate(v, cos, sin_signed, half):
    up = pltpu.roll(v, LANES - half, 1)
    dn = pltpu.roll(v, half, 1)
    lane = lax.broadcasted_iota(jnp.int32, v.shape, 1)
    partner = jnp.where(lane % (2 * half) < half, up, dn)
    return v * cos + partner * sin_signed


def _stream_tile(refs, split_streams, tile_axis):
    if not split_streams:
        return refs[0][0], refs[1:]
    is_ctx = pl.program_id(tile_axis) == pl.num_programs(tile_axis) - 1
    return jnp.where(is_ctx, refs[1][0], refs[0][0]), refs[2:]


def _inproj_kernel(*refs, split_streams):
    x, refs = _stream_tile(refs, split_streams, 0)
    mod_ref, g1_ref, w_ref, gain_ref, ca_ref, sa_ref, cb_ref, sb_ref = refs[:8]
    qkv_ref, gate_ref, akv_ref, ktb_ref, ktc_ref = refs[8:]
    sh = mod_ref[0, 0:1, :]
    sc = mod_ref[0, 1:2, :]
    h = (_rms(x) * g1_ref[...] * (1.0 + sc) + sh).astype(BF16)
    acc = jnp.dot(h, w_ref[...], preferred_element_type=F32)
    tm = acc.shape[0]
    head0 = lax.broadcasted_iota(jnp.int32, (1, LANES), 1) < HEAD_DIM
    for g in range(N_GROUPS):
        cols = slice(g * LANES, (g + 1) * LANES)
        v = acc[:, cols]
        gain = gain_ref[:, cols]
        if g in (G_AQ, G_AQ + 1, G_AK):
            v = v * _seg_inv_rms(v, HEAD_DIM) * gain
            v = _rotate(v, ca_ref[...], sa_ref[...], HEAD_DIM // 4)
        elif g in (G_BQ, G_BQ + 1, G_BK, G_BK + 1):
            v = v * _seg_inv_rms(v, B_QK_DIM) * gain
            v = _rotate(v, cb_ref[...], sb_ref[...], B_QK_DIM // 4)
        elif g in (G_DQ, G_DQ + 1, G_DK, G_DK + 1):
            v = v * _seg_inv_rms(v, HEAD_DIM) * gain
        elif g in (G_CK, G_CK + 1):
            v = v * gain
        elif g in (G_CG, G_CG + 1):
            gcols = slice((g - G_CG) * LANES, (g - G_CG + 1) * LANES)
            gate_ref[0, :, gcols] = v / (1.0 + jnp.exp(-v))
        qkv_ref[0, :, cols] = v.astype(BF16)
        if g in (G_AK, G_AV):
            n = 2 * (0 if g == G_AK else 1)
            swapped = pltpu.roll(v, HEAD_DIM, 1)
            akv_ref[0, :, n * LANES:(n + 1) * LANES] = jnp.where(head0, v, swapped).astype(BF16)
            akv_ref[0, :, (n + 1) * LANES:(n + 2) * LANES] = jnp.where(head0, swapped, v).astype(BF16)
        elif g in (G_BK, G_BK + 1):
            r0 = (g - G_BK) * LANES
            ktb_ref[0, r0:r0 + LANES, :] = v.T.astype(BF16)
        elif g in (G_CK, G_CK + 1):
            r0 = (g - G_CK) * LANES
            for c in range(tm // BLK):
                ktc_ref[0, c, r0:r0 + LANES, :] = v[c * BLK:(c + 1) * BLK].T.astype(BF16)


def _stream_specs(split_streams, tm, order):
    if not split_streams:
        return [pl.BlockSpec((1, tm, D_MODEL), lambda *g: (order(*g)[0], order(*g)[1], 0))]
    last = SEQ // tm - 1
    return [pl.BlockSpec((1, tm, D_MODEL), lambda *g: (order(*g)[0], jnp.minimum(order(*g)[1], last), 0)),
            pl.BlockSpec((1, tm, D_MODEL), lambda *g: (order(*g)[0], 0, 0))]


def _in_proj(streams, mod, g1, w_in, gain, tabs, tm=CTX_LEN):
    b = streams[0].shape[0]
    nt = T_ALL // tm
    ctx_row = mod.shape[0] - 1

    def mod_map(t, i):
        return (jnp.where(t == nt - 1, ctx_row, i), 0, 0)

    tab_spec = pl.BlockSpec((tm, LANES), lambda t, i: (t, 0))
    split = len(streams) == 2
    return pl.pallas_call(
        functools.partial(_inproj_kernel, split_streams=split),
        out_shape=(jax.ShapeDtypeStruct((b, T_ALL, IN_WIDTH), BF16),
                   jax.ShapeDtypeStruct((b, T_ALL, 2 * LANES), F32),
                   jax.ShapeDtypeStruct((b, T_ALL, 4 * LANES), BF16),
                   jax.ShapeDtypeStruct((b, 2 * LANES, T_ALL), BF16),
                   jax.ShapeDtypeStruct((b, N_ALL_BLK, 2 * LANES, BLK), BF16)),
        grid=(nt, b),
        in_specs=_stream_specs(split, tm, lambda t, i: (i, t)) + [
            pl.BlockSpec((1, 6, D_MODEL), mod_map),
            pl.BlockSpec((1, D_MODEL), lambda t, i: (0, 0)),
            pl.BlockSpec((D_MODEL, IN_WIDTH), lambda t, i: (0, 0)),
            pl.BlockSpec((1, IN_WIDTH), lambda t, i: (0, 0)),
            tab_spec, tab_spec, tab_spec, tab_spec,
        ],
        out_specs=(pl.BlockSpec((1, tm, IN_WIDTH), lambda t, i: (i, t, 0)),
                   pl.BlockSpec((1, tm, 2 * LANES), lambda t, i: (i, t, 0)),
                   pl.BlockSpec((1, tm, 4 * LANES), lambda t, i: (i, t, 0)),
                   pl.BlockSpec((1, 2 * LANES, tm), lambda t, i: (i, 0, t)),
                   pl.BlockSpec((1, tm // BLK, 2 * LANES, BLK), lambda t, i: (i, t, 0, 0))),
        compiler_params=_params(("parallel", "parallel")),
        name="in_proj",
    )(*streams, mod, g1, w_in, gain, *tabs)


def _a_kernel(sink_ref, q_ref, k_ref, v_ref, o_ref, *, ctx_queries):
    p = pl.program_id(1)
    nq = 2 * BLK
    strip = 3 * BLK
    head0 = lax.broadcasted_iota(jnp.int32, (1, LANES), 1) < HEAD_DIM
    row = lax.broadcasted_iota(jnp.int32, (nq, 1), 0)
    sink_col = jnp.where(row < BLK, sink_ref[2 * p], sink_ref[2 * p + 1])
    kx = k_ref[0, SEQ:T_ALL, :]
    vx = v_ref[0, SEQ:T_ALL, :]
    q_in_blk = lax.broadcasted_iota(jnp.int32, (nq, strip), 0) % BLK
    k_in_strip = lax.broadcasted_iota(jnp.int32, (nq, strip), 1)

    def stacked(q):
        zq = jnp.zeros_like(q)
        return jnp.concatenate([jnp.where(head0, q, zq), jnp.where(head0, zq, q)], axis=0)

    def attend(parts):
        m = functools.reduce(jnp.maximum, [jnp.max(s, axis=1, keepdims=True) for s, _ in parts] + [sink_col])
        es = [jnp.exp(s - m) for s, _ in parts]
        den = sum(jnp.sum(e, axis=1, keepdims=True) for e in es) + jnp.exp(sink_col - m)
        o = sum(jnp.dot(e.astype(BF16), v, preferred_element_type=F32) for e, (_, v) in zip(es, parts)) / den
        return jnp.where(head0, o[:BLK], o[BLK:]).astype(BF16)

    def block(j, carry):
        q0 = pl.multiple_of(j * BLK, BLK)
        start = pl.multiple_of(jnp.clip(j - 1, 0, N_LAT_BLK - 3) * BLK, BLK)
        qq = stacked(q_ref[0, pl.ds(q0, BLK), :])
        ks = k_ref[0, pl.ds(start, strip), :]
        vs = v_ref[0, pl.ds(start, strip), :]
        s_win = lax.dot_general(qq, ks, _NT, preferred_element_type=F32)
        dist = (q0 + q_in_blk) - (start + k_in_strip)
        s_win = jnp.where(jnp.abs(dist) <= WINDOW, s_win, NEG_INF)
        s_ctx = lax.dot_general(qq, kx, _NT, preferred_element_type=F32)
        o_ref[0, pl.ds(q0, BLK), :] = attend([(s_ctx, vx), (s_win, vs)])
        return carry

    lax.fori_loop(0, N_LAT_BLK, block, 0, unroll=4)
    if ctx_queries:
        for q0 in range(SEQ, T_ALL, BLK):
            qq = stacked(q_ref[0, q0:q0 + BLK, :])
            s_ctx = lax.dot_general(qq, kx, _NT, preferred_element_type=F32)
            o_ref[0, q0:q0 + BLK, :] = attend([(s_ctx, vx)])


def _attn_a(qkv, akv, sink, ctx_queries):
    b = qkv.shape[0]
    t_out = T_ALL if ctx_queries else SEQ
    return pl.pallas_call(
        functools.partial(_a_kernel, ctx_queries=ctx_queries),
        out_shape=jax.ShapeDtypeStruct((b, t_out, 2 * LANES), BF16),
        grid=(b, 2),
        in_specs=[
            pl.BlockSpec(memory_space=pltpu.SMEM),
            pl.BlockSpec((1, T_ALL, LANES), lambda i, p: (i, 0, G_AQ + p)),
            pl.BlockSpec((1, T_ALL, LANES), lambda i, p: (i, 0, p)),
            pl.BlockSpec((1, T_ALL, LANES), lambda i, p: (i, 0, 2 + p)),
        ],
        out_specs=pl.BlockSpec((1, t_out, LANES), lambda i, p: (i, 0, p)),
        compiler_params=_params(("parallel", "parallel")),
        name="attn_window",
    )(sink, qkv, akv, akv)


def _b_kernel(lam_ref, subg_ref, q_ref, kt_ref, v_ref, o_ref, qq_ref, m_ref, l_ref, acc_ref, s_ref,
              *, ctx_queries, lambda_init):
    lv = lam_ref[...]
    lam = (jnp.exp(jnp.sum(lv[0:1] * lv[1:2], axis=1, keepdims=True))
           - jnp.exp(jnp.sum(lv[2:3] * lv[3:4], axis=1, keepdims=True)) + lambda_init)
    lane = lax.broadcasted_iota(jnp.int32, (1, LANES), 1)
    rows = 4 * BLK
    mine = [(lane >= n * B_QK_DIM) & (lane < (n + 1) * B_QK_DIM) for n in range(4)]

    ksq = kt_ref[0].astype(F32)
    ksq = ksq * ksq
    k_max2 = [jnp.max(jnp.sum(ksq[n * B_QK_DIM:(n + 1) * B_QK_DIM], axis=0, keepdims=True), axis=1, keepdims=True)
              for n in range(4)]

    def chunks_of(k0, nk):
        return [(c, min(B_KEY_CHUNK, k0 + nk - c)) for c in range(k0, k0 + nk, B_KEY_CHUNK)]

    def tiles(t):
        return [t[:, c:c + LANES] for c in range(0, t.shape[1], LANES)]

    def row_norm2(blk, top):
        qf = q_ref[0, pl.ds(pl.multiple_of(blk * BLK, BLK), BLK), :].astype(F32)
        return jnp.maximum(top, jnp.sum(qf * qf, axis=1, keepdims=True))

    n_blk = N_ALL_BLK if ctx_queries else N_LAT_BLK
    q_max2 = jnp.max(lax.fori_loop(0, n_blk, row_norm2, jnp.zeros((BLK, 1), F32), unroll=True),
                     axis=0, keepdims=True)
    loose = jnp.sqrt(q_max2 * functools.reduce(jnp.maximum, k_max2)) * B_BOUND_SLACK
    usable = jnp.max(loose) < B_SAFE_BOUND

    def prepare(blk, par, with_bound):
        q = q_ref[0, pl.ds(pl.multiple_of(blk * BLK, BLK), BLK), :]
        qf = q.astype(F32)
        sq = qf * qf
        for n in range(4):
            qq_ref[par, n * BLK:(n + 1) * BLK] = jnp.where(mine[n], q, jnp.zeros_like(q))
            if with_bound:
                q_norm2 = jnp.sum(jnp.where(mine[n], sq, 0.0), axis=1, keepdims=True)
                bound = jnp.sqrt(q_norm2 * k_max2[n]) * B_BOUND_SLACK
                m_ref[par, n * BLK:(n + 1) * BLK] = jnp.broadcast_to(bound, (BLK, LANES))

    def prob_chunk(par, n, c0, cn, s):
        parts = []
        for r in range(0, rows, B_ROW_SLAB):
            m = m_ref[par, r:r + B_ROW_SLAB]
            es = [jnp.exp2(t - m) for t in tiles(s[r:r + B_ROW_SLAB])]
            t = functools.reduce(jnp.add, es)
            l_ref[par, r:r + B_ROW_SLAB] = t if n == 0 else l_ref[par, r:r + B_ROW_SLAB] + t
            parts.append(jnp.concatenate(es, axis=1).astype(BF16))
        part = jnp.dot(jnp.concatenate(parts, axis=0), v_ref[0, c0:c0 + cn, :],
                       preferred_element_type=F32)
        acc_ref[par] = part if n == 0 else acc_ref[par] + part

    def finish(blk, par):
        o = acc_ref[par] * (1.0 / jnp.sum(l_ref[par], axis=1, keepdims=True))
        outs = []
        for hh in range(2):
            r0 = 2 * hh * BLK
            oh = o[r0:r0 + BLK] - lam * o[r0 + BLK:r0 + 2 * BLK]
            oh = oh[:, hh * HEAD_DIM:(hh + 1) * HEAD_DIM]
            outs.append(_rms(oh) * subg_ref[...] * (1.0 - lambda_init))
        o_ref[0, pl.ds(pl.multiple_of(blk * BLK, BLK), BLK), :] = jnp.concatenate(outs, axis=1).astype(BF16)

    def one_sweep(blk, par, chunks):
        for n, (c0, cn) in enumerate(chunks):
            s = jnp.dot(qq_ref[par], kt_ref[0, :, c0:c0 + cn], preferred_element_type=F32)
            prob_chunk(par, n, c0, cn, s)
        finish(blk, par)

    def two_sweeps(blk, par, chunks):
        row_max = None
        for c0, cn in chunks:
            s = jnp.dot(qq_ref[par], kt_ref[0, :, c0:c0 + cn], preferred_element_type=F32)
            s_ref[:, c0:c0 + cn] = s
            t = functools.reduce(jnp.maximum, tiles(s))
            row_max = t if row_max is None else jnp.maximum(row_max, t)
        m_ref[par] = jnp.broadcast_to(jnp.max(row_max, axis=1, keepdims=True), (rows, LANES))
        for n, (c0, cn) in enumerate(chunks):
            prob_chunk(par, n, c0, cn, s_ref[:, c0:c0 + cn])
        finish(blk, par)

    def block_pair(first_blk, chunks, fast):
        for par in range(2):
            prepare(first_blk + par, par, fast)
        for par in range(2):
            (one_sweep if fast else two_sweeps)(first_blk + par, par, chunks)

    def all_blocks(fast):
        def body(n, carry):
            block_pair(2 * n, chunks_of(0, T_ALL), fast)
            return carry

        lax.fori_loop(0, N_LAT_BLK // 2, body, 0)
        if ctx_queries:
            block_pair(N_LAT_BLK, chunks_of(SEQ, CTX_LEN), fast)

    pl.when(usable)(lambda: all_blocks(True))
    pl.when(jnp.logical_not(usable))(lambda: all_blocks(False))


def _attn_b(qkv, kt, lam_vecs, subg, ctx_queries, lambda_init):
    b = qkv.shape[0]
    t_out = T_ALL if ctx_queries else SEQ
    assert N_LAT_BLK % 2 == 0 and N_ALL_BLK - N_LAT_BLK == 2
    stacked = lambda dt: pltpu.VMEM((2, 4 * BLK, LANES), dt)
    return pl.pallas_call(
        functools.partial(_b_kernel, ctx_queries=ctx_queries, lambda_init=lambda_init),
        out_shape=jax.ShapeDtypeStruct((b, t_out, 2 * LANES), BF16),
        grid=(b, 2),
        in_specs=[
            pl.BlockSpec((4, B_QK_DIM), lambda i, p: (0, 0)),
            pl.BlockSpec((1, HEAD_DIM), lambda i, p: (0, 0)),
            pl.BlockSpec((1, T_ALL, LANES), lambda i, p: (i, 0, G_BQ + p)),
            pl.BlockSpec((1, LANES, T_ALL), lambda i, p: (i, p, 0)),
            pl.BlockSpec((1, T_ALL, LANES), lambda i, p: (i, 0, G_BV + p)),
        ],
        out_specs=pl.BlockSpec((1, t_out, LANES), lambda i, p: (i, 0, p)),
        scratch_shapes=[stacked(BF16), stacked(F32), stacked(F32), stacked(F32),
                        pltpu.VMEM((4 * BLK, T_ALL), F32)],
        compiler_params=_params(("parallel", "parallel")),
        name="attn_diff",
    )(lam_vecs, subg, qkv, kt, qkv)


def _c_kernel(dec_ref, gn_ref, q_ref, kt_ref, v_ref, g_ref, o_ref, oi_ref, kv_ref, st_ref, tab_ref,
              *, ctx_out):
    n_ctx = CTX_LEN // BLK
    lane = lax.broadcasted_iota(jnp.int32, (1, LANES), 1)
    rowi = lax.broadcasted_iota(jnp.int32, (BLK, 1), 0)
    head0_l = lane < HEAD_DIM
    head0_r = rowi < HEAD_DIM
    same_head = head0_r == head0_l
    rel = (lax.broadcasted_iota(jnp.int32, (BLK, BLK), 0)
           - lax.broadcasted_iota(jnp.int32, (BLK, BLK), 1)).astype(F32)
    pos_r = rowi.astype(F32)
    pos_l = lane.astype(F32)

    def log_gamma(col):
        x = dec_ref[0, :, col:col + 1]
        return jnp.minimum(x, 0.0) - jnp.log1p(jnp.exp(-jnp.abs(x)))

    lg = [log_gamma(n) for n in range(4)]
    lg_l = [jnp.where(head0_l, lg[2 * d], lg[2 * d + 1]) for d in range(2)]
    lg_r = [jnp.where(head0_r, lg[2 * d], lg[2 * d + 1]) for d in range(2)]
    for hh in range(2):
        tab_ref[hh] = (jnp.where(rel >= 0, jnp.exp(rel * lg[hh]), 0.0)
                       + jnp.where(rel <= 0, jnp.exp(-rel * lg[2 + hh]), 0.0))
    tab_ref[2] = jnp.exp((pos_r + 1.0) * lg_l[0])
    tab_ref[3] = jnp.exp((BLK - pos_r) * lg_l[1])
    tab_ref[4] = jnp.exp((BLK - 1.0 - pos_l) * lg_r[0])
    tab_ref[5] = jnp.exp(pos_l * lg_r[1])
    chunk_decay = [jnp.exp(BLK * lg_l[d]) for d in range(2)]

    def local(c, carry):
        r0 = pl.multiple_of(c * BLK, BLK)
        q = q_ref[0, pl.ds(r0, BLK), :]
        kt = kt_ref[0, c]
        v = v_ref[0, pl.ds(r0, BLK), :]
        zq, zv = jnp.zeros_like(q), jnp.zeros_like(v)
        qq = jnp.concatenate([jnp.where(head0_l, q, zq), jnp.where(head0_l, zq, q)], axis=0)
        p = jnp.dot(qq, kt, preferred_element_type=F32)
        inner = jnp.concatenate([p[:BLK] * tab_ref[0], p[BLK:] * tab_ref[1]], axis=1).astype(BF16)
        vv = jnp.concatenate([jnp.where(head0_l, v, zv), jnp.where(head0_l, zv, v)], axis=0)
        oi_ref[pl.ds(r0, BLK), :] = jnp.dot(inner, vv, preferred_element_type=F32)
        ktf = kt.astype(F32)
        kk = jnp.concatenate([ktf * tab_ref[4], ktf * tab_ref[5]], axis=0).astype(BF16)
        kv = jnp.dot(kk, v, preferred_element_type=F32)
        kv_ref[c, :BLK] = jnp.where(same_head, kv[:BLK], 0.0)
        kv_ref[c, BLK:] = jnp.where(same_head, kv[BLK:], 0.0)
        return carry

    lax.fori_loop(0, N_ALL_BLK, local, 0, unroll=8)

    def scan(i, states):
        sf, sb = states
        cf = jnp.where(i < n_ctx, N_LAT_BLK + i, i - n_ctx)
        cb = N_ALL_BLK - 1 - i
        st_ref[cf, :BLK] = sf.astype(BF16)
        st_ref[cb, BLK:] = sb.astype(BF16)
        return sf * chunk_decay[0] + kv_ref[cf, :BLK], sb * chunk_decay[1] + kv_ref[cb, BLK:]

    zero = jnp.zeros((BLK, LANES), F32)
    lax.fori_loop(0, N_ALL_BLK, scan, (zero, zero))

    def finish(c, carry):
        r0 = pl.multiple_of(c * BLK, BLK)
        qf = q_ref[0, pl.ds(r0, BLK), :].astype(F32)
        qd = jnp.concatenate([qf * tab_ref[2], qf * tab_ref[3]], axis=1).astype(BF16)
        tot = oi_ref[pl.ds(r0, BLK), :] + jnp.dot(qd, st_ref[c], preferred_element_type=F32)
        sq = tot * tot
        ms0 = jnp.sum(jnp.where(head0_l, sq, 0.0), axis=1, keepdims=True) * (1.0 / HEAD_DIM)
        ms1 = jnp.sum(jnp.where(head0_l, 0.0, sq), axis=1, keepdims=True) * (1.0 / HEAD_DIM)
        inv = jnp.where(head0_l, lax.rsqrt(ms0 + EPS), lax.rsqrt(ms1 + EPS))
        o_ref[0, pl.ds(r0, BLK), :] = (tot * inv * gn_ref[...] * g_ref[0, pl.ds(r0, BLK), :]).astype(BF16)
        return carry

    lax.fori_loop(0, N_ALL_BLK if ctx_out else N_LAT_BLK, finish, 0, unroll=4)


def _retention(qkv, kt, gate, dec, gn, ctx_out):
    b = qkv.shape[0]
    t_out = T_ALL if ctx_out else SEQ
    col = lambda g: pl.BlockSpec((1, T_ALL, LANES), lambda i, p: (i, 0, g + p))
    return pl.pallas_call(
        functools.partial(_c_kernel, ctx_out=ctx_out),
        out_shape=jax.ShapeDtypeStruct((b, t_out, 2 * LANES), BF16),
        grid=(b, 2),
        in_specs=[
            pl.BlockSpec((1, 1, 4), lambda i, p: (p, 0, 0)),
            pl.BlockSpec((1, LANES), lambda i, p: (0, 0)),
            col(G_CQ),
            pl.BlockSpec((1, N_ALL_BLK, LANES, BLK), lambda i, p: (i, 0, p, 0)),
            col(G_CV),
            pl.BlockSpec((1, T_ALL, LANES), lambda i, p: (i, 0, p)),
        ],
        out_specs=pl.BlockSpec((1, t_out, LANES), lambda i, p: (i, 0, p)),
        scratch_shapes=[pltpu.VMEM((T_ALL, LANES), F32), pltpu.VMEM((N_ALL_BLK, 2 * BLK, LANES), F32),
                        pltpu.VMEM((N_ALL_BLK, 2 * BLK, LANES), BF16), pltpu.VMEM((6, BLK, BLK), F32)],
        compiler_params=_params(("parallel", "parallel")),
        name="retention",
    )(dec, gn, qkv, kt, qkv, gate)


def _d_kernel(q_ref, k_ref, v_ref, bias_ref, o_ref, *, ctx_queries):
    rows = SEQ // GRID_W
    n_grp = rows // D_GROUP
    gq = D_GROUP * GRID_W
    slab_rows = NA_ROWS + D_GROUP
    kx = k_ref[0, SEQ:T_ALL, :]
    vx = v_ref[0, SEQ:T_ALL, :]
    head0 = lax.broadcasted_iota(jnp.int32, (1, LANES), 1) < HEAD_DIM

    def attend(q, parts):
        n = q.shape[0]
        zq = jnp.zeros_like(q)
        qq = jnp.concatenate([jnp.where(head0, q, zq), jnp.where(head0, zq, q)], axis=0)
        scores = []
        for k, _, bias in parts:
            s = lax.dot_general(qq, k, _NT, preferred_element_type=F32)
            scores.append(s if bias is None else s + bias)
        m = functools.reduce(jnp.maximum, [jnp.max(s, axis=1, keepdims=True) for s in scores])
        es = [jnp.exp(s - m) for s in scores]
        den = sum(jnp.sum(e, axis=1, keepdims=True) for e in es)
        o = sum(jnp.dot(e.astype(BF16), v, preferred_element_type=F32)
                for e, (_, v, _) in zip(es, parts)) / den
        return jnp.where(head0, o[:n], o[n:]).astype(BF16)

    def group(g, carry):
        first_key_row = jnp.clip(g * D_GROUP - NA_ROWS // 2, 0, rows - slab_rows)
        start = pl.multiple_of(first_key_row * GRID_W, GRID_W)
        kind = jnp.where(g == 0, 0, jnp.where(g == n_grp - 1, 2, 1))
        q0 = pl.multiple_of(g * gq, gq)
        q = q_ref[0, pl.ds(q0, gq), :]
        ks = k_ref[0, pl.ds(start, slab_rows * GRID_W), :]
        vs = v_ref[0, pl.ds(start, slab_rows * GRID_W), :]
        o_ref[0, pl.ds(q0, gq), :] = attend(q, [(ks, vs, bias_ref[0, kind]), (kx, vx, None)])
        return carry

    lax.fori_loop(0, n_grp, group, 0, unroll=4)
    if ctx_queries:
        o_ref[0, SEQ:T_ALL, :] = attend(q_ref[0, SEQ:T_ALL, :], [(kx, vx, None)])


def _attn_d(qkv, bias, ctx_queries):
    b = qkv.shape[0]
    t_out = T_ALL if ctx_queries else SEQ
    col = lambda g: pl.BlockSpec((1, T_ALL, LANES), lambda i, p: (i, 0, g + p))
    return pl.pallas_call(
        functools.partial(_d_kernel, ctx_queries=ctx_queries),
        out_shape=jax.ShapeDtypeStruct((b, t_out, 2 * LANES), BF16),
        grid=(b, 2),
        in_specs=[
            col(G_DQ), col(G_DK), col(G_DV),
            pl.BlockSpec((1,) + bias.shape[1:], lambda i, p: (p, 0, 0, 0)),
        ],
        out_specs=pl.BlockSpec((1, t_out, LANES), lambda i, p: (i, 0, p)),
        compiler_params=_params(("parallel", "parallel")),
        name="attn_nbr",
    )(qkv, qkv, qkv, bias)


def _outmlp_kernel(*refs, split_streams):
    x, refs = _stream_tile(refs, split_streams, 1)
    a_ref, b_ref, c_ref, d_ref, mod_ref, g2_ref, wo_ref, w1_ref, w2_ref, o_ref = refs
    gw = 2 * LANES
    mix = None
    for n, r in enumerate((a_ref, b_ref, c_ref, d_ref)):
        part = jnp.dot(r[0], wo_ref[n * gw:(n + 1) * gw, :], preferred_element_type=F32)
        mix = part if mix is None else mix + part
    x1 = x + mod_ref[0, 2:3, :] * mix
    h2 = (_rms(x1) * g2_ref[...] * (1.0 + mod_ref[0, 4:5, :]) + mod_ref[0, 3:4, :]).astype(BF16)
    acc = None
    for n in range(D_FF // D_MODEL):
        cols = slice(n * D_MODEL, (n + 1) * D_MODEL)
        hid = jnp.maximum(jnp.dot(h2, w1_ref[:, cols], preferred_element_type=F32), 0.0)
        part = jnp.dot((hid * hid).astype(BF16), w2_ref[cols, :], preferred_element_type=F32)
        acc = part if acc is None else acc + part
    o_ref[0] = x1 + mod_ref[0, 5:6, :] * acc


def _out_mlp(streams, outs, mod, g2, w_out, w1, w2, ctx_tokens, tm):
    b = streams[0].shape[0]
    split = len(streams) == 2
    assert ctx_tokens or not split
    t_out = T_ALL if ctx_tokens else SEQ
    nt = t_out // tm
    ctx_row = mod.shape[0] - 1
    ctx_tile = SEQ // tm

    def mod_map(i, t):
        return (jnp.where(t >= ctx_tile, ctx_row, i), 0, 0)

    tile = lambda w: pl.BlockSpec((1, tm, w), lambda i, t: (i, t, 0))
    whole = lambda s: pl.BlockSpec(s, lambda i, t: (0, 0), pipeline_mode=pl.Buffered(1))
    return pl.pallas_call(
        functools.partial(_outmlp_kernel, split_streams=split),
        out_shape=jax.ShapeDtypeStruct((b, t_out, D_MODEL), F32),
        grid=(b, nt),
        in_specs=_stream_specs(split, tm, lambda i, t: (i, t)) + [
            tile(2 * LANES), tile(2 * LANES), tile(2 * LANES), tile(2 * LANES),
            pl.BlockSpec((1, 6, D_MODEL), mod_map),
            whole((1, D_MODEL)), whole((D_MODEL, D_MODEL)), whole((D_MODEL, D_FF)), whole((D_FF, D_MODEL)),
        ],
        out_specs=tile(D_MODEL),
        compiler_params=_params(("parallel", "parallel")),
        name="out_mlp",
    )(*streams, *outs, mod, g2, w_out, w1, w2)


def _rope_tables():
    pos = np.arange(SEQ)
    rows, cols = (pos // GRID_W).astype(np.float32), (pos % GRID_W).astype(np.float32)

    def pattern(half):
        freqs = (np.float32(ROPE_BASE) ** (-np.arange(half, dtype=np.float32) / np.float32(half))).astype(np.float32)
        ang = [(p[:, None] * freqs[None, :]).astype(np.float32) for p in (rows, cols)]
        cos = np.concatenate([np.cos(a) for a in (ang[0], ang[0], ang[1], ang[1])], axis=1)
        sin = np.concatenate([-np.sin(ang[0]), np.sin(ang[0]), -np.sin(ang[1]), np.sin(ang[1])], axis=1)
        reps = LANES // (4 * half)
        cos = np.concatenate([np.tile(cos, (1, reps)), np.ones((CTX_LEN, LANES))], axis=0)
        sin = np.concatenate([np.tile(sin, (1, reps)), np.zeros((CTX_LEN, LANES))], axis=0)
        return jnp.asarray(cos, F32), jnp.asarray(sin, F32)

    ca, sa = pattern(HEAD_DIM // 4)
    cb, sb = pattern(B_QK_DIM // 4)
    return ca, sa, cb, sb


def _gain_row(aq, ak, bq, bk, dq, dk):
    one = jnp.ones((LANES,), F32)
    t = lambda g, s=1.0: jnp.tile(g.astype(F32), LANES // g.shape[0]) * s
    groups = [one] * N_GROUPS
    groups[G_AQ] = groups[G_AQ + 1] = t(aq, HEAD_DIM ** -0.5)
    groups[G_AK] = t(ak)
    groups[G_BQ] = groups[G_BQ + 1] = t(bq, B_QK_DIM ** -0.5 * math.log2(math.e))
    groups[G_BK] = groups[G_BK + 1] = t(bk)
    groups[G_CK] = groups[G_CK + 1] = one * HEAD_DIM ** -0.5
    groups[G_DQ] = groups[G_DQ + 1] = t(dq, HEAD_DIM ** -0.5)
    groups[G_DK] = groups[G_DK + 1] = t(dk)
    return jnp.concatenate(groups)[None, :]


def _nbr_bias(rpb):
    assert D_GROUP == NA_ROWS // 2
    qc = np.arange(GRID_W)[:, None]
    kc = np.arange(GRID_W)[None, :]
    cstart = np.clip(qc - NA_COLS // 2, 0, GRID_W - NA_COLS)
    valid = (kc >= cstart) & (kc < cstart + NA_COLS)
    pad = GRID_W - NA_COLS
    rp = jnp.pad(rpb.astype(F32), ((0, 0), (0, 0), (pad, pad)))
    period = rp.shape[-1]
    flat = jnp.tile(rp, (1, 1, GRID_W + 1))[..., :GRID_W * (period + 1)]
    shifted = flat.reshape(rp.shape[0], rp.shape[1], GRID_W, period + 1)
    toep = shifted[:, :, ::-1, :GRID_W]
    toep = jnp.where(valid[None, None], toep, NEG_INF).transpose(0, 2, 1, 3)
    h = rpb.shape[0]
    half = NA_ROWS // 2

    def rows_of(dr0, offset):
        blk = toep[:, :, dr0:dr0 + NA_ROWS]
        blk = jnp.pad(blk, ((0, 0), (0, 0), (offset, D_GROUP - offset), (0, 0)), constant_values=NEG_INF)
        return blk.reshape(h, GRID_W, (NA_ROWS + D_GROUP) * GRID_W)

    kinds = [[rows_of(NA_ROWS - 1 - a, 0) for a in range(D_GROUP)],
             [rows_of(half - 1, a) for a in range(D_GROUP)],
             [rows_of(half - 1 - a, D_GROUP) for a in range(D_GROUP)]]
    b = jnp.stack([jnp.stack(k, axis=1) for k in kinds], axis=1)
    b = b.reshape(h // 2, 2, 3, D_GROUP * GRID_W, -1).transpose(0, 2, 1, 3, 4)
    return b.reshape(h // 2, 3, 2 * D_GROUP * GRID_W, -1)


def kernel(x, c, ctx, c_ctx, w_mod, b_mod, norm1_g, norm2_g, w_in, w_out, a_qnorm_g, a_knorm_g, a_sink, b_qnorm_g, b_knorm_g, b_lambda_q1, b_lambda_k1, b_lambda_q2, b_lambda_k2, b_subln_g, c_decay_fwd, c_decay_bwd, c_gn_g, d_qnorm_g, d_knorm_g, d_rpb, w_mlp1, w_mlp2):
    depth = w_mod.shape[0]
    b = x.shape[0]
    assert b + 1 <= 8
    cc = jnp.concatenate([c, c_ctx[None, :], jnp.zeros((8 - b - 1, D_MODEL), F32)], axis=0)
    mods = _modulation(cc, w_mod, b_mod)[:, :b + 1].reshape(depth, b + 1, 6, D_MODEL)
    tabs = _rope_tables()
    streams = (x, ctx)

    for layer in range(depth):
        need_ctx = layer < depth - 1
        lambda_init = 0.8 - 0.6 * math.exp(-0.3 * layer)
        gain = _gain_row(a_qnorm_g[layer], a_knorm_g[layer], b_qnorm_g[layer], b_knorm_g[layer],
                         d_qnorm_g[layer], d_knorm_g[layer])
        qkv, gate, akv, kt_b, kt_c = _in_proj(streams, mods[layer], norm1_g[layer][None, :],
                                              w_in[layer].astype(BF16), gain, tabs)

        out_a = _attn_a(qkv, akv, a_sink[layer].astype(F32), need_ctx)
        lam_vecs = jnp.stack([b_lambda_q1[layer], b_lambda_k1[layer],
                              b_lambda_q2[layer], b_lambda_k2[layer]]).astype(F32)
        out_b = _attn_b(qkv, kt_b, lam_vecs, b_subln_g[layer][None, :], need_ctx, lambda_init)
        dec = jnp.stack([c_decay_fwd[layer].reshape(2, 2), c_decay_bwd[layer].reshape(2, 2)],
                        axis=1).reshape(2, 1, 4).astype(F32)
        out_c = _retention(qkv, kt_c, gate, dec, jnp.tile(c_gn_g[layer], 2)[None, :], need_ctx)
        out_d = _attn_d(qkv, _nbr_bias(d_rpb[layer]), need_ctx)

        streams = (_out_mlp(streams, (out_a, out_b, out_c, out_d), mods[layer], norm2_g[layer][None, :],
                            w_out[layer].astype(BF16), w_mlp1[layer].astype(BF16), w_mlp2[layer].astype(BF16),
                            need_ctx, tm=CTX_LEN if need_ctx else 2 * CTX_LEN),)
    return streams[0]
```

```python
import functools
import math

import numpy as np
import jax
import jax.numpy as jnp
from jax import lax
from jax.experimental import pallas as pl
from jax.experimental.pallas import tpu as pltpu

F32 = jnp.float32
BF16 = jnp.bfloat16

D_MODEL = 1024
SEQ = 4096
CTX_LEN = 256
T_ALL = SEQ + CTX_LEN
GRID_W = 64
HEAD_DIM = 64
B_QK_DIM = 32
BLK = 128
N_LAT_BLK = SEQ // BLK
N_ALL_BLK = T_ALL // BLK
B_KEY_CHUNK = 256
D_GROUP = 4
B_ROW_SLAB = 32
B_SAFE_BOUND = 60.0
B_BOUND_SLACK = 1.0 + 2.0 ** -10
WINDOW = 128
NA_ROWS = 8
NA_COLS = 16
D_FF = 4 * D_MODEL
IN_WIDTH = 3072
ROPE_BASE = 10000.0
EPS = 1e-6
NEG_INF = -1e30
LOG2_E = math.log2(math.e)
LANES = 128
VMEM_LIMIT = 52 * 1024 * 1024

G_AQ, G_AK, G_AV = 0, 2, 3
G_BQ, G_BK, G_BV = 4, 6, 8
G_CQ, G_CK, G_CV, G_CG = 10, 12, 14, 16
G_DQ, G_DK, G_DV = 18, 20, 22
N_GROUPS = IN_WIDTH // LANES

_NT = (((1,), (1,)), ((), ()))
_TN = (((0,), (0,)), ((), ()))


def _params(sem):
    return pltpu.CompilerParams(dimension_semantics=sem, vmem_limit_bytes=VMEM_LIMIT)


def _rms(x):
    return x * lax.rsqrt(jnp.mean(x * x, axis=-1, keepdims=True) + EPS)


def _mod_kernel(c_ref, w_ref, b_ref, o_ref):
    c = c_ref[...]
    s = c / (1.0 + jnp.exp(-c))
    s_hi = s.astype(BF16)
    s_lo = (s - s_hi.astype(F32)).astype(BF16)
    w = w_ref[0]
    w_hi = w.astype(BF16)
    w_lo = (w - w_hi.astype(F32)).astype(BF16)
    acc = jnp.dot(s_hi, w_hi, preferred_element_type=F32)
    acc += jnp.dot(s_lo, w_hi, preferred_element_type=F32)
    acc += jnp.dot(s_hi, w_lo, preferred_element_type=F32)
    o_ref[0] = acc + b_ref[0]


def _modulation(cc, w_mod, b_mod):
    depth = w_mod.shape[0]
    n_col = 6 * D_MODEL // 1024
    return pl.pallas_call(
        _mod_kernel,
        out_shape=jax.ShapeDtypeStruct((depth, 8, 6 * D_MODEL), F32),
        grid=(depth, n_col),
        in_specs=[
            pl.BlockSpec((8, D_MODEL), lambda l, j: (0, 0)),
            pl.BlockSpec((1, D_MODEL, 1024), lambda l, j: (l, 0, j)),
            pl.BlockSpec((1, 1, 1024), lambda l, j: (l, 0, j)),
        ],
        out_specs=pl.BlockSpec((1, 8, 1024), lambda l, j: (l, 0, j)),
        compiler_params=_params(("parallel", "parallel")),
        name="adaln_mod",
    )(cc, w_mod, b_mod.reshape(depth, 1, 6 * D_MODEL))


def _seg_inv_rms(v, seg):
    sq = v * v
    hi = sq.astype(BF16)
    lo = (sq - hi.astype(F32)).astype(BF16)
    r = lax.broadcasted_iota(jnp.int32, (LANES, LANES), 0) // seg
    c = lax.broadcasted_iota(jnp.int32, (LANES, LANES), 1) // seg
    bd = jnp.where(r == c, 1.0, 0.0).astype(BF16)
    ssum = jnp.dot(hi, bd, preferred_element_type=F32) + jnp.dot(lo, bd, preferred_element_type=F32)
    return lax.rsqrt(ssum * (1.0 / seg) + EPS)


def _rotate(v, cos, sin_signed, half):
    up = pltpu.roll(v, LANES - half, 1)
    dn = pltpu.roll(v, half, 1)
    lane = lax.broadcasted_iota(jnp.int32, v.shape, 1)
    partner = jnp.where(lane % (2 * half) < half, up, dn)
    return v * cos + partner * sin_signed


def _stream_tile(refs, split_streams, tile_axis):
    if not split_streams:
        return refs[0][0], refs[1:]
    is_ctx = pl.program_id(tile_axis) == pl.num_programs(tile_axis) - 1
    return jnp.where(is_ctx, refs[1][0], refs[0][0]), refs[2:]


def _inproj_kernel(*refs, split_streams):
    x, refs = _stream_tile(refs, split_streams, 0)
    mod_ref, g1_ref, w_ref, gain_ref, ca_ref, sa_ref, cb_ref, sb_ref = refs[:8]
    qkv_ref, gate_ref, akv_ref, ktb_ref, ktc_ref = refs[8:]
    sh = mod_ref[0, 0:1, :]
    sc = mod_ref[0, 1:2, :]
    h = (_rms(x) * g1_ref[...] * (1.0 + sc) + sh).astype(BF16)
    acc = jnp.dot(h, w_ref[...], preferred_element_type=F32)
    tm = acc.shape[0]
    head0 = lax.broadcasted_iota(jnp.int32, (1, LANES), 1) < HEAD_DIM
    for g in range(N_GROUPS):
        cols = slice(g * LANES, (g + 1) * LANES)
        v = acc[:, cols]
        gain = gain_ref[:, cols]
        if g in (G_AQ, G_AQ + 1, G_AK):
            v = v * _seg_inv_rms(v, HEAD_DIM) * gain
            v = _rotate(v, ca_ref[...], sa_ref[...], HEAD_DIM // 4)
        elif g in (G_BQ, G_BQ + 1, G_BK, G_BK + 1):
            v = v * _seg_inv_rms(v, B_QK_DIM) * gain
            v = _rotate(v, cb_ref[...], sb_ref[...], B_QK_DIM // 4)
        elif g in (G_DQ, G_DQ + 1, G_DK, G_DK + 1):
            v = v * _seg_inv_rms(v, HEAD_DIM) * gain
        elif g in (G_CK, G_CK + 1):
            v = v * gain
        elif g in (G_CG, G_CG + 1):
            gcols = slice((g - G_CG) * LANES, (g - G_CG + 1) * LANES)
            gate_ref[0, :, gcols] = v / (1.0 + jnp.exp(-v))
        qkv_ref[0, :, cols] = v.astype(BF16)
        if g in (G_AK, G_AV):
            n = 2 * (0 if g == G_AK else 1)
            swapped = pltpu.roll(v, HEAD_DIM, 1)
            akv_ref[0, :, n * LANES:(n + 1) * LANES] = jnp.where(head0, v, swapped).astype(BF16)
            akv_ref[0, :, (n + 1) * LANES:(n + 2) * LANES] = jnp.where(head0, swapped, v).astype(BF16)
        elif g in (G_BK, G_BK + 1):
            r0 = (g - G_BK) * LANES
            ktb_ref[0, r0:r0 + LANES, :] = v.T.astype(BF16)
        elif g in (G_CK, G_CK + 1):
            r0 = (g - G_CK) * LANES
            for c in range(tm // BLK):
                ktc_ref[0, c, r0:r0 + LANES, :] = v[c * BLK:(c + 1) * BLK].T.astype(BF16)


def _stream_specs(split_streams, tm, order):
    if not split_streams:
        return [pl.BlockSpec((1, tm, D_MODEL), lambda *g: (order(*g)[0], order(*g)[1], 0))]
    last = SEQ // tm - 1
    return [pl.BlockSpec((1, tm, D_MODEL), lambda *g: (order(*g)[0], jnp.minimum(order(*g)[1], last), 0)),
            pl.BlockSpec((1, tm, D_MODEL), lambda *g: (order(*g)[0], 0, 0))]


def _in_proj(streams, mod, g1, w_in, gain, tabs, tm=CTX_LEN):
    b = streams[0].shape[0]
    nt = T_ALL // tm
    ctx_row = mod.shape[0] - 1

    def mod_map(t, i):
        return (jnp.where(t == nt - 1, ctx_row, i), 0, 0)

    tab_spec = pl.BlockSpec((tm, LANES), lambda t, i: (t, 0))
    split = len(streams) == 2
    return pl.pallas_call(
        functools.partial(_inproj_kernel, split_streams=split),
        out_shape=(jax.ShapeDtypeStruct((b, T_ALL, IN_WIDTH), BF16),
                   jax.ShapeDtypeStruct((b, T_ALL, 2 * LANES), F32),
                   jax.ShapeDtypeStruct((b, T_ALL, 4 * LANES), BF16),
                   jax.ShapeDtypeStruct((b, 2 * LANES, T_ALL), BF16),
                   jax.ShapeDtypeStruct((b, N_ALL_BLK, 2 * LANES, BLK), BF16)),
        grid=(nt, b),
        in_specs=_stream_specs(split, tm, lambda t, i: (i, t)) + [
            pl.BlockSpec((1, 6, D_MODEL), mod_map),
            pl.BlockSpec((1, D_MODEL), lambda t, i: (0, 0)),
            pl.BlockSpec((D_MODEL, IN_WIDTH), lambda t, i: (0, 0)),
            pl.BlockSpec((1, IN_WIDTH), lambda t, i: (0, 0)),
            tab_spec, tab_spec, tab_spec, tab_spec,
        ],
        out_specs=(pl.BlockSpec((1, tm, IN_WIDTH), lambda t, i: (i, t, 0)),
                   pl.BlockSpec((1, tm, 2 * LANES), lambda t, i: (i, t, 0)),
                   pl.BlockSpec((1, tm, 4 * LANES), lambda t, i: (i, t, 0)),
                   pl.BlockSpec((1, 2 * LANES, tm), lambda t, i: (i, 0, t)),
                   pl.BlockSpec((1, tm // BLK, 2 * LANES, BLK), lambda t, i: (i, t, 0, 0))),
        compiler_params=_params(("parallel", "parallel")),
        name="in_proj",
    )(*streams, mod, g1, w_in, gain, *tabs)


def _a_kernel(sink_ref, q_ref, k_ref, v_ref, o_ref, mask_ref, *, ctx_queries):
    p = pl.program_id(1)
    nq = 2 * BLK
    strip = 3 * BLK
    head0 = lax.broadcasted_iota(jnp.int32, (1, LANES), 1) < HEAD_DIM
    row = lax.broadcasted_iota(jnp.int32, (nq, 1), 0)
    sink_col = jnp.where(row < BLK, sink_ref[2 * p], sink_ref[2 * p + 1]) * LOG2_E
    kx = k_ref[0, SEQ:T_ALL, :]
    vx = v_ref[0, SEQ:T_ALL, :]
    dist = (lax.broadcasted_iota(jnp.int32, (nq, strip), 0) % BLK
            - lax.broadcasted_iota(jnp.int32, (nq, strip), 1))
    for lead in range(3):
        mask_ref[lead] = jnp.where(jnp.abs(dist + lead * BLK) <= WINDOW, 0.0, NEG_INF)

    def stacked(q):
        zq = jnp.zeros_like(q)
        return jnp.concatenate([jnp.where(head0, q, zq), jnp.where(head0, zq, q)], axis=0)

    def attend(parts):
        m = functools.reduce(jnp.maximum, [jnp.max(s, axis=1, keepdims=True) for s, _ in parts] + [sink_col])
        es = [jnp.exp2(s - m) for s, _ in parts]
        den = sum(jnp.sum(e, axis=1, keepdims=True) for e in es) + jnp.exp2(sink_col - m)
        o = sum(jnp.dot(e.astype(BF16), v, preferred_element_type=F32) for e, (_, v) in zip(es, parts)) / den
        return jnp.where(head0, o[:BLK], o[BLK:]).astype(BF16)

    def block(j, carry):
        q0 = pl.multiple_of(j * BLK, BLK)
        first = jnp.clip(j - 1, 0, N_LAT_BLK - 3)
        start = pl.multiple_of(first * BLK, BLK)
        qq = stacked(q_ref[0, pl.ds(q0, BLK), :])
        ks = k_ref[0, pl.ds(start, strip), :]
        vs = v_ref[0, pl.ds(start, strip), :]
        s_win = lax.dot_general(qq, ks, _NT, preferred_element_type=F32) + mask_ref[j - first]
        s_ctx = lax.dot_general(qq, kx, _NT, preferred_element_type=F32)
        o_ref[0, pl.ds(q0, BLK), :] = attend([(s_ctx, vx), (s_win, vs)])
        return carry

    lax.fori_loop(0, N_LAT_BLK, block, 0, unroll=8)
    if ctx_queries:
        for q0 in range(SEQ, T_ALL, BLK):
            qq = stacked(q_ref[0, q0:q0 + BLK, :])
            s_ctx = lax.dot_general(qq, kx, _NT, preferred_element_type=F32)
            o_ref[0, q0:q0 + BLK, :] = attend([(s_ctx, vx)])


def _attn_a(qkv, akv, sink, ctx_queries):
    b = qkv.shape[0]
    t_out = T_ALL if ctx_queries else SEQ
    return pl.pallas_call(
        functools.partial(_a_kernel, ctx_queries=ctx_queries),
        out_shape=jax.ShapeDtypeStruct((b, t_out, 2 * LANES), BF16),
        grid=(b, 2),
        in_specs=[
            pl.BlockSpec(memory_space=pltpu.SMEM),
            pl.BlockSpec((1, T_ALL, LANES), lambda i, p: (i, 0, G_AQ + p)),
            pl.BlockSpec((1, T_ALL, LANES), lambda i, p: (i, 0, p)),
            pl.BlockSpec((1, T_ALL, LANES), lambda i, p: (i, 0, 2 + p)),
        ],
        out_specs=pl.BlockSpec((1, t_out, LANES), lambda i, p: (i, 0, p)),
        scratch_shapes=[pltpu.VMEM((3, 2 * BLK, 3 * BLK), F32)],
        compiler_params=_params(("parallel", "parallel")),
        name="attn_window",
    )(sink, qkv, akv, akv)


def _b_kernel(lam_ref, subg_ref, q_ref, kt_ref, v_ref, o_ref, qq_ref, m_ref, l_ref, acc_ref, s_ref,
              *, ctx_queries, lambda_init):
    lv = lam_ref[...]
    lam = (jnp.exp(jnp.sum(lv[0:1] * lv[1:2], axis=1, keepdims=True))
           - jnp.exp(jnp.sum(lv[2:3] * lv[3:4], axis=1, keepdims=True)) + lambda_init)
    lane = lax.broadcasted_iota(jnp.int32, (1, LANES), 1)
    rows = 4 * BLK
    mine = [(lane >= n * B_QK_DIM) & (lane < (n + 1) * B_QK_DIM) for n in range(4)]

    ksq = kt_ref[0].astype(F32)
    ksq = ksq * ksq
    k_max2 = [jnp.max(jnp.sum(ksq[n * B_QK_DIM:(n + 1) * B_QK_DIM], axis=0, keepdims=True), axis=1, keepdims=True)
              for n in range(4)]

    def chunks_of(k0, nk):
        return [(c, min(B_KEY_CHUNK, k0 + nk - c)) for c in range(k0, k0 + nk, B_KEY_CHUNK)]

    def tiles(t):
        return [t[:, c:c + LANES] for c in range(0, t.shape[1], LANES)]

    def row_norm2(blk, top):
        qf = q_ref[0, pl.ds(pl.multiple_of(blk * BLK, BLK), BLK), :].astype(F32)
        return jnp.maximum(top, jnp.sum(qf * qf, axis=1, keepdims=True))

    n_blk = N_ALL_BLK if ctx_queries else N_LAT_BLK
    q_max2 = jnp.max(lax.fori_loop(0, n_blk, row_norm2, jnp.zeros((BLK, 1), F32), unroll=True),
                     axis=0, keepdims=True)
    loose = jnp.sqrt(q_max2 * functools.reduce(jnp.maximum, k_max2)) * B_BOUND_SLACK
    usable = jnp.max(loose) < B_SAFE_BOUND

    def prepare(blk, par, with_bound):
        q = q_ref[0, pl.ds(pl.multiple_of(blk * BLK, BLK), BLK), :]
        qf = q.astype(F32)
        sq = qf * qf
        for n in range(4):
            qq_ref[par, n * BLK:(n + 1) * BLK] = jnp.where(mine[n], q, jnp.zeros_like(q))
            if with_bound:
                q_norm2 = jnp.sum(jnp.where(mine[n], sq, 0.0), axis=1, keepdims=True)
                bound = jnp.sqrt(q_norm2 * k_max2[n]) * B_BOUND_SLACK
                m_ref[par, n * BLK:(n + 1) * BLK] = jnp.broadcast_to(bound, (BLK, LANES))

    def prob_chunk(par, n, c0, cn, s):
        parts = []
        for r in range(0, rows, B_ROW_SLAB):
            m = m_ref[par, r:r + B_ROW_SLAB]
            es = [jnp.exp2(t - m) for t in tiles(s[r:r + B_ROW_SLAB])]
            t = functools.reduce(jnp.add, es)
            l_ref[par, r:r + B_ROW_SLAB] = t if n == 0 else l_ref[par, r:r + B_ROW_SLAB] + t
            parts.append(jnp.concatenate(es, axis=1).astype(BF16))
        part = jnp.dot(jnp.concatenate(parts, axis=0), v_ref[0, c0:c0 + cn, :],
                       preferred_element_type=F32)
        acc_ref[par] = part if n == 0 else acc_ref[par] + part

    def finish(blk, par):
        o = acc_ref[par] * (1.0 / jnp.sum(l_ref[par], axis=1, keepdims=True))
        outs = []
        for hh in range(2):
            r0 = 2 * hh * BLK
            oh = o[r0:r0 + BLK] - lam * o[r0 + BLK:r0 + 2 * BLK]
            oh = oh[:, hh * HEAD_DIM:(hh + 1) * HEAD_DIM]
            outs.append(_rms(oh) * subg_ref[...] * (1.0 - lambda_init))
        o_ref[0, pl.ds(pl.multiple_of(blk * BLK, BLK), BLK), :] = jnp.concatenate(outs, axis=1).astype(BF16)

    def one_sweep(blk, par, chunks):
        for n, (c0, cn) in enumerate(chunks):
            s = jnp.dot(qq_ref[par], kt_ref[0, :, c0:c0 + cn], preferred_element_type=F32)
            prob_chunk(par, n, c0, cn, s)
        finish(blk, par)

    def two_sweeps(blk, par, chunks):
        row_max = None
        for c0, cn in chunks:
            s = jnp.dot(qq_ref[par], kt_ref[0, :, c0:c0 + cn], preferred_element_type=F32)
            s_ref[:, c0:c0 + cn] = s
            t = functools.reduce(jnp.maximum, tiles(s))
            row_max = t if row_max is None else jnp.maximum(row_max, t)
        m_ref[par] = jnp.broadcast_to(jnp.max(row_max, axis=1, keepdims=True), (rows, LANES))
        for n, (c0, cn) in enumerate(chunks):
            prob_chunk(par, n, c0, cn, s_ref[:, c0:c0 + cn])
        finish(blk, par)

    def block_pair(first_blk, chunks, fast):
        for par in range(2):
            prepare(first_blk + par, par, fast)
        for par in range(2):
            (one_sweep if fast else two_sweeps)(first_blk + par, par, chunks)

    def all_blocks(fast):
        def body(n, carry):
            block_pair(2 * n, chunks_of(0, T_ALL), fast)
            return carry

        lax.fori_loop(0, N_LAT_BLK // 2, body, 0)
        if ctx_queries:
            block_pair(N_LAT_BLK, chunks_of(SEQ, CTX_LEN), fast)

    pl.when(usable)(lambda: all_blocks(True))
    pl.when(jnp.logical_not(usable))(lambda: all_blocks(False))


def _attn_b(qkv, kt, lam_vecs, subg, ctx_queries, lambda_init):
    b = qkv.shape[0]
    t_out = T_ALL if ctx_queries else SEQ
    assert N_LAT_BLK % 2 == 0 and N_ALL_BLK - N_LAT_BLK == 2
    stacked = lambda dt: pltpu.VMEM((2, 4 * BLK, LANES), dt)
    return pl.pallas_call(
        functools.partial(_b_kernel, ctx_queries=ctx_queries, lambda_init=lambda_init),
        out_shape=jax.ShapeDtypeStruct((b, t_out, 2 * LANES), BF16),
        grid=(b, 2),
        in_specs=[
            pl.BlockSpec((4, B_QK_DIM), lambda i, p: (0, 0)),
            pl.BlockSpec((1, HEAD_DIM), lambda i, p: (0, 0)),
            pl.BlockSpec((1, T_ALL, LANES), lambda i, p: (i, 0, G_BQ + p)),
            pl.BlockSpec((1, LANES, T_ALL), lambda i, p: (i, p, 0)),
            pl.BlockSpec((1, T_ALL, LANES), lambda i, p: (i, 0, G_BV + p)),
        ],
        out_specs=pl.BlockSpec((1, t_out, LANES), lambda i, p: (i, 0, p)),
        scratch_shapes=[stacked(BF16), stacked(F32), stacked(F32), stacked(F32),
                        pltpu.VMEM((4 * BLK, T_ALL), F32)],
        compiler_params=_params(("parallel", "parallel")),
        name="attn_diff",
    )(lam_vecs, subg, qkv, kt, qkv)


def _c_kernel(dec_ref, gn_ref, q_ref, kt_ref, v_ref, g_ref, o_ref, oi_ref, kv_ref, st_ref, tab_ref,
              *, ctx_out):
    n_ctx = CTX_LEN // BLK
    lane = lax.broadcasted_iota(jnp.int32, (1, LANES), 1)
    rowi = lax.broadcasted_iota(jnp.int32, (BLK, 1), 0)
    head0_l = lane < HEAD_DIM
    head0_r = rowi < HEAD_DIM
    same_head = head0_r == head0_l
    rel = (lax.broadcasted_iota(jnp.int32, (BLK, BLK), 0)
           - lax.broadcasted_iota(jnp.int32, (BLK, BLK), 1)).astype(F32)
    pos_r = rowi.astype(F32)
    pos_l = lane.astype(F32)

    def log_gamma(col):
        x = dec_ref[0, :, col:col + 1]
        return jnp.minimum(x, 0.0) - jnp.log1p(jnp.exp(-jnp.abs(x)))

    lg = [log_gamma(n) for n in range(4)]
    lg_l = [jnp.where(head0_l, lg[2 * d], lg[2 * d + 1]) for d in range(2)]
    lg_r = [jnp.where(head0_r, lg[2 * d], lg[2 * d + 1]) for d in range(2)]
    for hh in range(2):
        tab_ref[hh] = (jnp.where(rel >= 0, jnp.exp(rel * lg[hh]), 0.0)
                       + jnp.where(rel <= 0, jnp.exp(-rel * lg[2 + hh]), 0.0))
    tab_ref[2] = jnp.exp((pos_r + 1.0) * lg_l[0])
    tab_ref[3] = jnp.exp((BLK - pos_r) * lg_l[1])
    tab_ref[4] = jnp.exp((BLK - 1.0 - pos_l) * lg_r[0])
    tab_ref[5] = jnp.exp(pos_l * lg_r[1])
    chunk_decay = [jnp.exp(BLK * lg_l[d]) for d in range(2)]

    def local(c, carry):
        r0 = pl.multiple_of(c * BLK, BLK)
        q = q_ref[0, pl.ds(r0, BLK), :]
        kt = kt_ref[0, c]
        v = v_ref[0, pl.ds(r0, BLK), :]
        zq, zv = jnp.zeros_like(q), jnp.zeros_like(v)
        qq = jnp.concatenate([jnp.where(head0_l, q, zq), jnp.where(head0_l, zq, q)], axis=0)
        p = jnp.dot(qq, kt, preferred_element_type=F32)
        inner = jnp.concatenate([p[:BLK] * tab_ref[0], p[BLK:] * tab_ref[1]], axis=1).astype(BF16)
        vv = jnp.concatenate([jnp.where(head0_l, v, zv), jnp.where(head0_l, zv, v)], axis=0)
        oi_ref[pl.ds(r0, BLK), :] = jnp.dot(inner, vv, preferred_element_type=F32)
        ktf = kt.astype(F32)
        kk = jnp.concatenate([ktf * tab_ref[4], ktf * tab_ref[5]], axis=0).astype(BF16)
        kv = jnp.dot(kk, v, preferred_element_type=F32)
        kv_ref[c, :BLK] = jnp.where(same_head, kv[:BLK], 0.0)
        kv_ref[c, BLK:] = jnp.where(same_head, kv[BLK:], 0.0)
        return carry

    lax.fori_loop(0, N_ALL_BLK, local, 0, unroll=8)

    def scan(i, states):
        sf, sb = states
        cf = jnp.where(i < n_ctx, N_LAT_BLK + i, i - n_ctx)
        cb = N_ALL_BLK - 1 - i
        st_ref[cf, :BLK] = sf.astype(BF16)
        st_ref[cb, BLK:] = sb.astype(BF16)
        return sf * chunk_decay[0] + kv_ref[cf, :BLK], sb * chunk_decay[1] + kv_ref[cb, BLK:]

    zero = jnp.zeros((BLK, LANES), F32)
    lax.fori_loop(0, N_ALL_BLK, scan, (zero, zero))

    def finish(c, carry):
        r0 = pl.multiple_of(c * BLK, BLK)
        qf = q_ref[0, pl.ds(r0, BLK), :].astype(F32)
        qd = jnp.concatenate([qf * tab_ref[2], qf * tab_ref[3]], axis=1).astype(BF16)
        tot = oi_ref[pl.ds(r0, BLK), :] + jnp.dot(qd, st_ref[c], preferred_element_type=F32)
        sq = tot * tot
        ms0 = jnp.sum(jnp.where(head0_l, sq, 0.0), axis=1, keepdims=True) * (1.0 / HEAD_DIM)
        ms1 = jnp.sum(jnp.where(head0_l, 0.0, sq), axis=1, keepdims=True) * (1.0 / HEAD_DIM)
        inv = jnp.where(head0_l, lax.rsqrt(ms0 + EPS), lax.rsqrt(ms1 + EPS))
        o_ref[0, pl.ds(r0, BLK), :] = (tot * inv * gn_ref[...] * g_ref[0, pl.ds(r0, BLK), :]).astype(BF16)
        return carry

    lax.fori_loop(0, N_ALL_BLK if ctx_out else N_LAT_BLK, finish, 0, unroll=4)


def _retention(qkv, kt, gate, dec, gn, ctx_out):
    b = qkv.shape[0]
    t_out = T_ALL if ctx_out else SEQ
    col = lambda g: pl.BlockSpec((1, T_ALL, LANES), lambda i, p: (i, 0, g + p))
    return pl.pallas_call(
        functools.partial(_c_kernel, ctx_out=ctx_out),
        out_shape=jax.ShapeDtypeStruct((b, t_out, 2 * LANES), BF16),
        grid=(b, 2),
        in_specs=[
            pl.BlockSpec((1, 1, 4), lambda i, p: (p, 0, 0)),
            pl.BlockSpec((1, LANES), lambda i, p: (0, 0)),
            col(G_CQ),
            pl.BlockSpec((1, N_ALL_BLK, LANES, BLK), lambda i, p: (i, 0, p, 0)),
            col(G_CV),
            pl.BlockSpec((1, T_ALL, LANES), lambda i, p: (i, 0, p)),
        ],
        out_specs=pl.BlockSpec((1, t_out, LANES), lambda i, p: (i, 0, p)),
        scratch_shapes=[pltpu.VMEM((T_ALL, LANES), F32), pltpu.VMEM((N_ALL_BLK, 2 * BLK, LANES), F32),
                        pltpu.VMEM((N_ALL_BLK, 2 * BLK, LANES), BF16), pltpu.VMEM((6, BLK, BLK), F32)],
        compiler_params=_params(("parallel", "parallel")),
        name="retention",
    )(dec, gn, qkv, kt, qkv, gate)


def _d_kernel(q_ref, k_ref, v_ref, bias_ref, o_ref, *, ctx_queries):
    rows = SEQ // GRID_W
    n_grp = rows // D_GROUP
    gq = D_GROUP * GRID_W
    slab_rows = NA_ROWS + D_GROUP
    kx = k_ref[0, SEQ:T_ALL, :]
    vx = v_ref[0, SEQ:T_ALL, :]
    head0 = lax.broadcasted_iota(jnp.int32, (1, LANES), 1) < HEAD_DIM

    def attend(q, parts):
        n = q.shape[0]
        zq = jnp.zeros_like(q)
        qq = jnp.concatenate([jnp.where(head0, q, zq), jnp.where(head0, zq, q)], axis=0)
        scores = []
        for k, _, bias in parts:
            s = lax.dot_general(qq, k, _NT, preferred_element_type=F32)
            scores.append(s if bias is None else s + bias)
        m = functools.reduce(jnp.maximum, [jnp.max(s, axis=1, keepdims=True) for s in scores])
        es = [jnp.exp(s - m) for s in scores]
        den = sum(jnp.sum(e, axis=1, keepdims=True) for e in es)
        o = sum(jnp.dot(e.astype(BF16), v, preferred_element_type=F32)
                for e, (_, v, _) in zip(es, parts)) / den
        return jnp.where(head0, o[:n], o[n:]).astype(BF16)

    def group(g, carry):
        first_key_row = jnp.clip(g * D_GROUP - NA_ROWS // 2, 0, rows - slab_rows)
        start = pl.multiple_of(first_key_row * GRID_W, GRID_W)
        kind = jnp.where(g == 0, 0, jnp.where(g == n_grp - 1, 2, 1))
        q0 = pl.multiple_of(g * gq, gq)
        q = q_ref[0, pl.ds(q0, gq), :]
        ks = k_ref[0, pl.ds(start, slab_rows * GRID_W), :]
        vs = v_ref[0, pl.ds(start, slab_rows * GRID_W), :]
        o_ref[0, pl.ds(q0, gq), :] = attend(q, [(ks, vs, bias_ref[0, kind]), (kx, vx, None)])
        return carry

    lax.fori_loop(0, n_grp, group, 0, unroll=8)
    if ctx_queries:
        o_ref[0, SEQ:T_ALL, :] = attend(q_ref[0, SEQ:T_ALL, :], [(kx, vx, None)])


def _attn_d(qkv, bias, layer, ctx_queries):
    b = qkv.shape[0]
    t_out = T_ALL if ctx_queries else SEQ
    col = lambda g: pl.BlockSpec((1, T_ALL, LANES), lambda i, p: (i, 0, g + p))
    return pl.pallas_call(
        functools.partial(_d_kernel, ctx_queries=ctx_queries),
        out_shape=jax.ShapeDtypeStruct((b, t_out, 2 * LANES), BF16),
        grid=(b, 2),
        in_specs=[
            col(G_DQ), col(G_DK), col(G_DV),
            pl.BlockSpec((1,) + bias.shape[1:], lambda i, p: (2 * layer + p, 0, 0, 0)),
        ],
        out_specs=pl.BlockSpec((1, t_out, LANES), lambda i, p: (i, 0, p)),
        compiler_params=_params(("parallel", "parallel")),
        name="attn_nbr",
    )(qkv, qkv, qkv, bias)


def _outmlp_kernel(*refs, split_streams):
    x, refs = _stream_tile(refs, split_streams, 1)
    a_ref, b_ref, c_ref, d_ref, mod_ref, g2_ref, wo_ref, w1_ref, w2_ref, o_ref = refs
    gw = 2 * LANES
    mix = None
    for n, r in enumerate((a_ref, b_ref, c_ref, d_ref)):
        part = jnp.dot(r[0], wo_ref[n * gw:(n + 1) * gw, :], preferred_element_type=F32)
        mix = part if mix is None else mix + part
    x1 = x + mod_ref[0, 2:3, :] * mix
    h2 = (_rms(x1) * g2_ref[...] * (1.0 + mod_ref[0, 4:5, :]) + mod_ref[0, 3:4, :]).astype(BF16)
    acc = None
    for n in range(D_FF // D_MODEL):
        cols = slice(n * D_MODEL, (n + 1) * D_MODEL)
        hid = jnp.maximum(jnp.dot(h2, w1_ref[:, cols], preferred_element_type=F32), 0.0)
        part = jnp.dot((hid * hid).astype(BF16), w2_ref[cols, :], preferred_element_type=F32)
        acc = part if acc is None else acc + part
    o_ref[0] = x1 + mod_ref[0, 5:6, :] * acc


def _out_mlp(streams, outs, mod, g2, w_out, w1, w2, ctx_tokens, tm):
    b = streams[0].shape[0]
    split = len(streams) == 2
    assert ctx_tokens or not split
    t_out = T_ALL if ctx_tokens else SEQ
    nt = t_out // tm
    ctx_row = mod.shape[0] - 1
    ctx_tile = SEQ // tm

    def mod_map(i, t):
        return (jnp.where(t >= ctx_tile, ctx_row, i), 0, 0)

    tile = lambda w: pl.BlockSpec((1, tm, w), lambda i, t: (i, t, 0))
    whole = lambda s: pl.BlockSpec(s, lambda i, t: (0, 0), pipeline_mode=pl.Buffered(1))
    return pl.pallas_call(
        functools.partial(_outmlp_kernel, split_streams=split),
        out_shape=jax.ShapeDtypeStruct((b, t_out, D_MODEL), F32),
        grid=(b, nt),
        in_specs=_stream_specs(split, tm, lambda i, t: (i, t)) + [
            tile(2 * LANES), tile(2 * LANES), tile(2 * LANES), tile(2 * LANES),
            pl.BlockSpec((1, 6, D_MODEL), mod_map),
            whole((1, D_MODEL)), whole((D_MODEL, D_MODEL)), whole((D_MODEL, D_FF)), whole((D_FF, D_MODEL)),
        ],
        out_specs=tile(D_MODEL),
        compiler_params=_params(("parallel", "parallel")),
        name="out_mlp",
    )(*streams, *outs, mod, g2, w_out, w1, w2)


def _rope_tables():
    pos = np.arange(SEQ)
    rows, cols = (pos // GRID_W).astype(np.float32), (pos % GRID_W).astype(np.float32)

    def pattern(half):
        freqs = (np.float32(ROPE_BASE) ** (-np.arange(half, dtype=np.float32) / np.float32(half))).astype(np.float32)
        ang = [(p[:, None] * freqs[None, :]).astype(np.float32) for p in (rows, cols)]
        cos = np.concatenate([np.cos(a) for a in (ang[0], ang[0], ang[1], ang[1])], axis=1)
        sin = np.concatenate([-np.sin(ang[0]), np.sin(ang[0]), -np.sin(ang[1]), np.sin(ang[1])], axis=1)
        reps = LANES // (4 * half)
        cos = np.concatenate([np.tile(cos, (1, reps)), np.ones((CTX_LEN, LANES))], axis=0)
        sin = np.concatenate([np.tile(sin, (1, reps)), np.zeros((CTX_LEN, LANES))], axis=0)
        return jnp.asarray(cos, F32), jnp.asarray(sin, F32)

    ca, sa = pattern(HEAD_DIM // 4)
    cb, sb = pattern(B_QK_DIM // 4)
    return ca, sa, cb, sb


def _gain_row(aq, ak, bq, bk, dq, dk):
    one = jnp.ones((LANES,), F32)
    t = lambda g, s=1.0: jnp.tile(g.astype(F32), LANES // g.shape[0]) * s
    groups = [one] * N_GROUPS
    groups[G_AQ] = groups[G_AQ + 1] = t(aq, HEAD_DIM ** -0.5 * LOG2_E)
    groups[G_AK] = t(ak)
    groups[G_BQ] = groups[G_BQ + 1] = t(bq, B_QK_DIM ** -0.5 * LOG2_E)
    groups[G_BK] = groups[G_BK + 1] = t(bk)
    groups[G_CK] = groups[G_CK + 1] = one * HEAD_DIM ** -0.5
    groups[G_DQ] = groups[G_DQ + 1] = t(dq, HEAD_DIM ** -0.5)
    groups[G_DK] = groups[G_DK + 1] = t(dk)
    return jnp.concatenate(groups)[None, :]


def _nbr_bias(rpb):
    assert D_GROUP == NA_ROWS // 2
    qc = np.arange(GRID_W)[:, None]
    kc = np.arange(GRID_W)[None, :]
    cstart = np.clip(qc - NA_COLS // 2, 0, GRID_W - NA_COLS)
    valid = (kc >= cstart) & (kc < cstart + NA_COLS)
    pad = GRID_W - NA_COLS
    rp = jnp.pad(rpb.astype(F32), ((0, 0), (0, 0), (pad, pad)))
    period = rp.shape[-1]
    flat = jnp.tile(rp, (1, 1, GRID_W + 1))[..., :GRID_W * (period + 1)]
    shifted = flat.reshape(rp.shape[0], rp.shape[1], GRID_W, period + 1)
    toep = shifted[:, :, ::-1, :GRID_W]
    toep = jnp.where(valid[None, None], toep, NEG_INF).transpose(0, 2, 1, 3)
    h = rpb.shape[0]
    half = NA_ROWS // 2

    def rows_of(dr0, offset):
        blk = toep[:, :, dr0:dr0 + NA_ROWS]
        blk = jnp.pad(blk, ((0, 0), (0, 0), (offset, D_GROUP - offset), (0, 0)), constant_values=NEG_INF)
        return blk.reshape(h, GRID_W, (NA_ROWS + D_GROUP) * GRID_W)

    kinds = [[rows_of(NA_ROWS - 1 - a, 0) for a in range(D_GROUP)],
             [rows_of(half - 1, a) for a in range(D_GROUP)],
             [rows_of(half - 1 - a, D_GROUP) for a in range(D_GROUP)]]
    b = jnp.stack([jnp.stack(k, axis=1) for k in kinds], axis=1)
    b = b.reshape(h // 2, 2, 3, D_GROUP * GRID_W, -1).transpose(0, 2, 1, 3, 4)
    return b.reshape(h // 2, 3, 2 * D_GROUP * GRID_W, -1)


def kernel(x, c, ctx, c_ctx, w_mod, b_mod, norm1_g, norm2_g, w_in, w_out, a_qnorm_g, a_knorm_g, a_sink, b_qnorm_g, b_knorm_g, b_lambda_q1, b_lambda_k1, b_lambda_q2, b_lambda_k2, b_subln_g, c_decay_fwd, c_decay_bwd, c_gn_g, d_qnorm_g, d_knorm_g, d_rpb, w_mlp1, w_mlp2):
    depth = w_mod.shape[0]
    b = x.shape[0]
    assert b + 1 <= 8
    cc = jnp.concatenate([c, c_ctx[None, :], jnp.zeros((8 - b - 1, D_MODEL), F32)], axis=0)
    mods = _modulation(cc, w_mod, b_mod)[:, :b + 1].reshape(depth, b + 1, 6, D_MODEL)
    tabs = _rope_tables()
    nbr_bias = _nbr_bias(d_rpb.reshape((-1,) + d_rpb.shape[2:]))
    streams = (x, ctx)

    for layer in range(depth):
        need_ctx = layer < depth - 1
        lambda_init = 0.8 - 0.6 * math.exp(-0.3 * layer)
        gain = _gain_row(a_qnorm_g[layer], a_knorm_g[layer], b_qnorm_g[layer], b_knorm_g[layer],
                         d_qnorm_g[layer], d_knorm_g[layer])
        qkv, gate, akv, kt_b, kt_c = _in_proj(streams, mods[layer], norm1_g[layer][None, :],
                                              w_in[layer].astype(BF16), gain, tabs)

        out_a = _attn_a(qkv, akv, a_sink[layer].astype(F32), need_ctx)
        lam_vecs = jnp.stack([b_lambda_q1[layer], b_lambda_k1[layer],
                              b_lambda_q2[layer], b_lambda_k2[layer]]).astype(F32)
        out_b = _attn_b(qkv, kt_b, lam_vecs, b_subln_g[layer][None, :], need_ctx, lambda_init)
        dec = jnp.stack([c_decay_fwd[layer].reshape(2, 2), c_decay_bwd[layer].reshape(2, 2)],
                        axis=1).reshape(2, 1, 4).astype(F32)
        out_c = _retention(qkv, kt_c, gate, dec, jnp.tile(c_gn_g[layer], 2)[None, :], need_ctx)
        out_d = _attn_d(qkv, nbr_bias, layer, need_ctx)

        streams = (_out_mlp(streams, (out_a, out_b, out_c, out_d), mods[layer], norm2_g[layer][None, :],
                            w_out[layer].astype(BF16), w_mlp1[layer].astype(BF16), w_mlp2[layer].astype(BF16),
                            need_ctx, tm=CTX_LEN if need_ctx else 2 * CTX_LEN),)
    return streams[0]
```

```python
import functools
import math

import numpy as np
import jax
import jax.numpy as jnp
from jax import lax
from jax.experimental import pallas as pl
from jax.experimental.pallas import tpu as pltpu

F32 = jnp.float32
BF16 = jnp.bfloat16

D_MODEL = 1024
SEQ = 4096
CTX_LEN = 256
T_ALL = SEQ + CTX_LEN
GRID_W = 64
HEAD_DIM = 64
B_QK_DIM = 32
BLK = 128
N_LAT_BLK = SEQ // BLK
N_ALL_BLK = T_ALL // BLK
B_KEY_CHUNK = 256
D_GROUP = 4
B_BLOCKS_PER_TRIP = 4
B_ROW_SLAB = 32
B_SAFE_BOUND = 60.0
B_BOUND_SLACK = 1.0 + 2.0 ** -10
WINDOW = 128
NA_ROWS = 8
NA_COLS = 16
D_FF = 4 * D_MODEL
IN_WIDTH = 3072
ROPE_BASE = 10000.0
EPS = 1e-6
NEG_INF = -1e30
LOG2_E = math.log2(math.e)
LANES = 128
VMEM_LIMIT = 52 * 1024 * 1024

G_AQ, G_AK, G_AV = 0, 2, 3
G_BQ, G_BK, G_BV = 4, 6, 8
G_CQ, G_CK, G_CV, G_CG = 10, 12, 14, 16
G_DQ, G_DK, G_DV = 18, 20, 22
N_GROUPS = IN_WIDTH // LANES

_NT = (((1,), (1,)), ((), ()))
_TN = (((0,), (0,)), ((), ()))


def _params(sem):
    return pltpu.CompilerParams(dimension_semantics=sem, vmem_limit_bytes=VMEM_LIMIT)


def _rms(x):
    return x * lax.rsqrt(jnp.mean(x * x, axis=-1, keepdims=True) + EPS)


def _mod_kernel(c_ref, w_ref, b_ref, o_ref):
    c = c_ref[...]
    s = c / (1.0 + jnp.exp(-c))
    s_hi = s.astype(BF16)
    s_lo = (s - s_hi.astype(F32)).astype(BF16)
    w = w_ref[0]
    w_hi = w.astype(BF16)
    w_lo = (w - w_hi.astype(F32)).astype(BF16)
    acc = jnp.dot(s_hi, w_hi, preferred_element_type=F32)
    acc += jnp.dot(s_lo, w_hi, preferred_element_type=F32)
    acc += jnp.dot(s_hi, w_lo, preferred_element_type=F32)
    o_ref[0] = acc + b_ref[0]


def _modulation(cc, w_mod, b_mod):
    depth = w_mod.shape[0]
    n_col = 6 * D_MODEL // 1024
    return pl.pallas_call(
        _mod_kernel,
        out_shape=jax.ShapeDtypeStruct((depth, 8, 6 * D_MODEL), F32),
        grid=(depth, n_col),
        in_specs=[
            pl.BlockSpec((8, D_MODEL), lambda l, j: (0, 0)),
            pl.BlockSpec((1, D_MODEL, 1024), lambda l, j: (l, 0, j)),
            pl.BlockSpec((1, 1, 1024), lambda l, j: (l, 0, j)),
        ],
        out_specs=pl.BlockSpec((1, 8, 1024), lambda l, j: (l, 0, j)),
        compiler_params=_params(("parallel", "parallel")),
        name="adaln_mod",
    )(cc, w_mod, b_mod.reshape(depth, 1, 6 * D_MODEL))


def _seg_inv_rms(v, seg):
    sq = v * v
    hi = sq.astype(BF16)
    lo = (sq - hi.astype(F32)).astype(BF16)
    r = lax.broadcasted_iota(jnp.int32, (LANES, LANES), 0) // seg
    c = lax.broadcasted_iota(jnp.int32, (LANES, LANES), 1) // seg
    bd = jnp.where(r == c, 1.0, 0.0).astype(BF16)
    ssum = jnp.dot(hi, bd, preferred_element_type=F32) + jnp.dot(lo, bd, preferred_element_type=F32)
    return lax.rsqrt(ssum * (1.0 / seg) + EPS)


def _rotate(v, cos, sin_signed, half):
    up = pltpu.roll(v, LANES - half, 1)
    dn = pltpu.roll(v, half, 1)
    lane = lax.broadcasted_iota(jnp.int32, v.shape, 1)
    partner = jnp.where(lane % (2 * half) < half, up, dn)
    return v * cos + partner * sin_signed


def _stream_tile(refs, split_streams, tile_axis):
    if not split_streams:
        return refs[0][0], refs[1:]
    is_ctx = pl.program_id(tile_axis) == pl.num_programs(tile_axis) - 1
    return jnp.where(is_ctx, refs[1][0], refs[0][0]), refs[2:]


def _inproj_kernel(*refs, split_streams):
    x, refs = _stream_tile(refs, split_streams, 0)
    mod_ref, g1_ref, w_ref, gain_ref, ca_ref, sa_ref, cb_ref, sb_ref = refs[:8]
    qkv_ref, gate_ref, akv_ref, ktb_ref, ktc_ref = refs[8:]
    sh = mod_ref[0, 0:1, :]
    sc = mod_ref[0, 1:2, :]
    h = (_rms(x) * g1_ref[...] * (1.0 + sc) + sh).astype(BF16)
    acc = jnp.dot(h, w_ref[...], preferred_element_type=F32)
    tm = acc.shape[0]
    head0 = lax.broadcasted_iota(jnp.int32, (1, LANES), 1) < HEAD_DIM
    for g in range(N_GROUPS):
        cols = slice(g * LANES, (g + 1) * LANES)
        v = acc[:, cols]
        gain = gain_ref[:, cols]
        if g in (G_AQ, G_AQ + 1, G_AK):
            v = v * _seg_inv_rms(v, HEAD_DIM) * gain
            v = _rotate(v, ca_ref[...], sa_ref[...], HEAD_DIM // 4)
        elif g in (G_BQ, G_BQ + 1, G_BK, G_BK + 1):
            v = v * _seg_inv_rms(v, B_QK_DIM) * gain
            v = _rotate(v, cb_ref[...], sb_ref[...], B_QK_DIM // 4)
        elif g in (G_DQ, G_DQ + 1, G_DK, G_DK + 1):
            v = v * _seg_inv_rms(v, HEAD_DIM) * gain
        elif g in (G_CK, G_CK + 1):
            v = v * gain
        elif g in (G_CG, G_CG + 1):
            gcols = slice((g - G_CG) * LANES, (g - G_CG + 1) * LANES)
            gate_ref[0, :, gcols] = v / (1.0 + jnp.exp(-v))
        qkv_ref[0, :, cols] = v.astype(BF16)
        if g in (G_AK, G_AV):
            n = 2 * (0 if g == G_AK else 1)
            swapped = pltpu.roll(v, HEAD_DIM, 1)
            akv_ref[0, :, n * LANES:(n + 1) * LANES] = jnp.where(head0, v, swapped).astype(BF16)
            akv_ref[0, :, (n + 1) * LANES:(n + 2) * LANES] = jnp.where(head0, swapped, v).astype(BF16)
        elif g in (G_BK, G_BK + 1):
            r0 = (g - G_BK) * LANES
            ktb_ref[0, r0:r0 + LANES, :] = v.T.astype(BF16)
        elif g in (G_CK, G_CK + 1):
            r0 = (g - G_CK) * LANES
            for c in range(tm // BLK):
                ktc_ref[0, c, r0:r0 + LANES, :] = v[c * BLK:(c + 1) * BLK].T.astype(BF16)


def _stream_specs(split_streams, tm, order):
    if not split_streams:
        return [pl.BlockSpec((1, tm, D_MODEL), lambda *g: (order(*g)[0], order(*g)[1], 0))]
    last = SEQ // tm - 1
    return [pl.BlockSpec((1, tm, D_MODEL), lambda *g: (order(*g)[0], jnp.minimum(order(*g)[1], last), 0)),
            pl.BlockSpec((1, tm, D_MODEL), lambda *g: (order(*g)[0], 0, 0))]


def _in_proj(streams, mod, g1, w_in, gain, tabs, tm=CTX_LEN):
    b = streams[0].shape[0]
    nt = T_ALL // tm
    ctx_row = mod.shape[0] - 1

    def mod_map(t, i):
        return (jnp.where(t == nt - 1, ctx_row, i), 0, 0)

    tab_spec = pl.BlockSpec((tm, LANES), lambda t, i: (t, 0))
    split = len(streams) == 2
    return pl.pallas_call(
        functools.partial(_inproj_kernel, split_streams=split),
        out_shape=(jax.ShapeDtypeStruct((b, T_ALL, IN_WIDTH), BF16),
                   jax.ShapeDtypeStruct((b, T_ALL, 2 * LANES), F32),
                   jax.ShapeDtypeStruct((b, T_ALL, 4 * LANES), BF16),
                   jax.ShapeDtypeStruct((b, 2 * LANES, T_ALL), BF16),
                   jax.ShapeDtypeStruct((b, N_ALL_BLK, 2 * LANES, BLK), BF16)),
        grid=(nt, b),
        in_specs=_stream_specs(split, tm, lambda t, i: (i, t)) + [
            pl.BlockSpec((1, 6, D_MODEL), mod_map),
            pl.BlockSpec((1, D_MODEL), lambda t, i: (0, 0)),
            pl.BlockSpec((D_MODEL, IN_WIDTH), lambda t, i: (0, 0)),
            pl.BlockSpec((1, IN_WIDTH), lambda t, i: (0, 0)),
            tab_spec, tab_spec, tab_spec, tab_spec,
        ],
        out_specs=(pl.BlockSpec((1, tm, IN_WIDTH), lambda t, i: (i, t, 0)),
                   pl.BlockSpec((1, tm, 2 * LANES), lambda t, i: (i, t, 0)),
                   pl.BlockSpec((1, tm, 4 * LANES), lambda t, i: (i, t, 0)),
                   pl.BlockSpec((1, 2 * LANES, tm), lambda t, i: (i, 0, t)),
                   pl.BlockSpec((1, tm // BLK, 2 * LANES, BLK), lambda t, i: (i, t, 0, 0))),
        compiler_params=_params(("parallel", "parallel")),
        name="in_proj",
    )(*streams, mod, g1, w_in, gain, *tabs)


def _a_kernel(sink_ref, q_ref, k_ref, v_ref, o_ref, mask_ref, *, ctx_queries):
    p = pl.program_id(1)
    nq = 2 * BLK
    strip = 3 * BLK
    head0 = lax.broadcasted_iota(jnp.int32, (1, LANES), 1) < HEAD_DIM
    row = lax.broadcasted_iota(jnp.int32, (nq, 1), 0)
    sink_col = jnp.where(row < BLK, sink_ref[2 * p], sink_ref[2 * p + 1]) * LOG2_E
    kx = k_ref[0, SEQ:T_ALL, :]
    vx = v_ref[0, SEQ:T_ALL, :]
    dist = (lax.broadcasted_iota(jnp.int32, (nq, strip), 0) % BLK
            - lax.broadcasted_iota(jnp.int32, (nq, strip), 1))
    for lead in range(3):
        mask_ref[lead] = jnp.where(jnp.abs(dist + lead * BLK) <= WINDOW, 0.0, NEG_INF)

    def stacked(q):
        zq = jnp.zeros_like(q)
        return jnp.concatenate([jnp.where(head0, q, zq), jnp.where(head0, zq, q)], axis=0)

    def attend(parts):
        m = functools.reduce(jnp.maximum, [jnp.max(s, axis=1, keepdims=True) for s, _ in parts] + [sink_col])
        es = [jnp.exp2(s - m) for s, _ in parts]
        den = sum(jnp.sum(e, axis=1, keepdims=True) for e in es) + jnp.exp2(sink_col - m)
        o = sum(jnp.dot(e.astype(BF16), v, preferred_element_type=F32) for e, (_, v) in zip(es, parts)) / den
        return jnp.where(head0, o[:BLK], o[BLK:]).astype(BF16)

    def block(j, carry):
        q0 = pl.multiple_of(j * BLK, BLK)
        first = jnp.clip(j - 1, 0, N_LAT_BLK - 3)
        start = pl.multiple_of(first * BLK, BLK)
        qq = stacked(q_ref[0, pl.ds(q0, BLK), :])
        ks = k_ref[0, pl.ds(start, strip), :]
        vs = v_ref[0, pl.ds(start, strip), :]
        s_win = lax.dot_general(qq, ks, _NT, preferred_element_type=F32) + mask_ref[j - first]
        s_ctx = lax.dot_general(qq, kx, _NT, preferred_element_type=F32)
        o_ref[0, pl.ds(q0, BLK), :] = attend([(s_ctx, vx), (s_win, vs)])
        return carry

    lax.fori_loop(0, N_LAT_BLK, block, 0, unroll=8)
    if ctx_queries:
        for q0 in range(SEQ, T_ALL, BLK):
            qq = stacked(q_ref[0, q0:q0 + BLK, :])
            s_ctx = lax.dot_general(qq, kx, _NT, preferred_element_type=F32)
            o_ref[0, q0:q0 + BLK, :] = attend([(s_ctx, vx)])


def _attn_a(qkv, akv, sink, ctx_queries):
    b = qkv.shape[0]
    t_out = T_ALL if ctx_queries else SEQ
    return pl.pallas_call(
        functools.partial(_a_kernel, ctx_queries=ctx_queries),
        out_shape=jax.ShapeDtypeStruct((b, t_out, 2 * LANES), BF16),
        grid=(b, 2),
        in_specs=[
            pl.BlockSpec(memory_space=pltpu.SMEM),
            pl.BlockSpec((1, T_ALL, LANES), lambda i, p: (i, 0, G_AQ + p)),
            pl.BlockSpec((1, T_ALL, LANES), lambda i, p: (i, 0, p)),
            pl.BlockSpec((1, T_ALL, LANES), lambda i, p: (i, 0, 2 + p)),
        ],
        out_specs=pl.BlockSpec((1, t_out, LANES), lambda i, p: (i, 0, p)),
        scratch_shapes=[pltpu.VMEM((3, 2 * BLK, 3 * BLK), F32)],
        compiler_params=_params(("parallel", "parallel")),
        name="attn_window",
    )(sink, qkv, akv, akv)


def _b_kernel(lam_ref, subg_ref, q_ref, kt_ref, v_ref, o_ref, qq_ref, m_ref, l_ref, acc_ref, s_ref,
              *, ctx_queries, lambda_init):
    lv = lam_ref[...]
    lam = (jnp.exp(jnp.sum(lv[0:1] * lv[1:2], axis=1, keepdims=True))
           - jnp.exp(jnp.sum(lv[2:3] * lv[3:4], axis=1, keepdims=True)) + lambda_init)
    lane = lax.broadcasted_iota(jnp.int32, (1, LANES), 1)
    rows = 4 * BLK
    mine = [(lane >= n * B_QK_DIM) & (lane < (n + 1) * B_QK_DIM) for n in range(4)]

    ksq = kt_ref[0].astype(F32)
    ksq = ksq * ksq
    k_max2 = [jnp.max(jnp.sum(ksq[n * B_QK_DIM:(n + 1) * B_QK_DIM], axis=0, keepdims=True), axis=1, keepdims=True)
              for n in range(4)]

    def chunks_of(k0, nk):
        return [(c, min(B_KEY_CHUNK, k0 + nk - c)) for c in range(k0, k0 + nk, B_KEY_CHUNK)]

    def tiles(t):
        return [t[:, c:c + LANES] for c in range(0, t.shape[1], LANES)]

    def row_norm2(blk, top):
        qf = q_ref[0, pl.ds(pl.multiple_of(blk * BLK, BLK), BLK), :].astype(F32)
        return jnp.maximum(top, jnp.sum(qf * qf, axis=1, keepdims=True))

    n_blk = N_ALL_BLK if ctx_queries else N_LAT_BLK
    q_max2 = jnp.max(lax.fori_loop(0, n_blk, row_norm2, jnp.zeros((BLK, 1), F32), unroll=True),
                     axis=0, keepdims=True)
    loose = jnp.sqrt(q_max2 * functools.reduce(jnp.maximum, k_max2)) * B_BOUND_SLACK
    usable = jnp.max(loose) < B_SAFE_BOUND

    def prepare(blk, par, with_bound):
        q = q_ref[0, pl.ds(pl.multiple_of(blk * BLK, BLK), BLK), :]
        qf = q.astype(F32)
        sq = qf * qf
        for n in range(4):
            qq_ref[par, n * BLK:(n + 1) * BLK] = jnp.where(mine[n], q, jnp.zeros_like(q))
            if with_bound:
                q_norm2 = jnp.sum(jnp.where(mine[n], sq, 0.0), axis=1, keepdims=True)
                bound = jnp.sqrt(q_norm2 * k_max2[n]) * B_BOUND_SLACK
                m_ref[par, n * BLK:(n + 1) * BLK] = jnp.broadcast_to(bound, (BLK, LANES))

    def prob_chunk(par, n, c0, cn, s):
        parts = []
        for r in range(0, rows, B_ROW_SLAB):
            m = m_ref[par, r:r + B_ROW_SLAB]
            es = [jnp.exp2(t - m) for t in tiles(s[r:r + B_ROW_SLAB])]
            t = functools.reduce(jnp.add, es)
            l_ref[par, r:r + B_ROW_SLAB] = t if n == 0 else l_ref[par, r:r + B_ROW_SLAB] + t
            parts.append(jnp.concatenate(es, axis=1).astype(BF16))
        part = jnp.dot(jnp.concatenate(parts, axis=0), v_ref[0, c0:c0 + cn, :],
                       preferred_element_type=F32)
        acc_ref[par] = part if n == 0 else acc_ref[par] + part

    def finish(blk, par):
        o = acc_ref[par] * (1.0 / jnp.sum(l_ref[par], axis=1, keepdims=True))
        outs = []
        for hh in range(2):
            r0 = 2 * hh * BLK
            oh = o[r0:r0 + BLK] - lam * o[r0 + BLK:r0 + 2 * BLK]
            oh = oh[:, hh * HEAD_DIM:(hh + 1) * HEAD_DIM]
            outs.append(_rms(oh) * subg_ref[...] * (1.0 - lambda_init))
        o_ref[0, pl.ds(pl.multiple_of(blk * BLK, BLK), BLK), :] = jnp.concatenate(outs, axis=1).astype(BF16)

    def one_sweep(blk, par, chunks):
        for n, (c0, cn) in enumerate(chunks):
            s = jnp.dot(qq_ref[par], kt_ref[0, :, c0:c0 + cn], preferred_element_type=F32)
            prob_chunk(par, n, c0, cn, s)
        finish(blk, par)

    def two_sweeps(blk, par, chunks):
        row_max = None
        for c0, cn in chunks:
            s = jnp.dot(qq_ref[par], kt_ref[0, :, c0:c0 + cn], preferred_element_type=F32)
            s_ref[:, c0:c0 + cn] = s
            t = functools.reduce(jnp.maximum, tiles(s))
            row_max = t if row_max is None else jnp.maximum(row_max, t)
        m_ref[par] = jnp.broadcast_to(jnp.max(row_max, axis=1, keepdims=True), (rows, LANES))
        for n, (c0, cn) in enumerate(chunks):
            prob_chunk(par, n, c0, cn, s_ref[:, c0:c0 + cn])
        finish(blk, par)

    def block_group(first_blk, n_blk, chunks, fast):
        for par in range(n_blk):
            prepare(first_blk + par, par, fast)
        for par in range(n_blk):
            (one_sweep if fast else two_sweeps)(first_blk + par, par, chunks)

    def all_blocks(fast):
        per_trip = B_BLOCKS_PER_TRIP if fast else 1

        def body(n, carry):
            block_group(per_trip * n, per_trip, chunks_of(0, T_ALL), fast)
            return carry

        lax.fori_loop(0, N_LAT_BLK // per_trip, body, 0)
        if ctx_queries:
            block_group(N_LAT_BLK, N_ALL_BLK - N_LAT_BLK, chunks_of(SEQ, CTX_LEN), fast)

    pl.when(usable)(lambda: all_blocks(True))
    pl.when(jnp.logical_not(usable))(lambda: all_blocks(False))


def _attn_b(qkv, kt, lam_vecs, subg, ctx_queries, lambda_init):
    b = qkv.shape[0]
    t_out = T_ALL if ctx_queries else SEQ
    assert N_LAT_BLK % B_BLOCKS_PER_TRIP == 0 and N_ALL_BLK - N_LAT_BLK <= B_BLOCKS_PER_TRIP
    stacked = lambda dt: pltpu.VMEM((B_BLOCKS_PER_TRIP, 4 * BLK, LANES), dt)
    return pl.pallas_call(
        functools.partial(_b_kernel, ctx_queries=ctx_queries, lambda_init=lambda_init),
        out_shape=jax.ShapeDtypeStruct((b, t_out, 2 * LANES), BF16),
        grid=(b, 2),
        in_specs=[
            pl.BlockSpec((4, B_QK_DIM), lambda i, p: (0, 0)),
            pl.BlockSpec((1, HEAD_DIM), lambda i, p: (0, 0)),
            pl.BlockSpec((1, T_ALL, LANES), lambda i, p: (i, 0, G_BQ + p)),
            pl.BlockSpec((1, LANES, T_ALL), lambda i, p: (i, p, 0)),
            pl.BlockSpec((1, T_ALL, LANES), lambda i, p: (i, 0, G_BV + p)),
        ],
        out_specs=pl.BlockSpec((1, t_out, LANES), lambda i, p: (i, 0, p)),
        scratch_shapes=[stacked(BF16), stacked(F32), stacked(F32), stacked(F32),
                        pltpu.VMEM((4 * BLK, T_ALL), F32)],
        compiler_params=_params(("parallel", "parallel")),
        name="attn_diff",
    )(lam_vecs, subg, qkv, kt, qkv)


def _c_kernel(dec_ref, gn_ref, q_ref, kt_ref, v_ref, g_ref, o_ref, oi_ref, kv_ref, st_ref, tab_ref,
              *, ctx_out):
    n_ctx = CTX_LEN // BLK
    lane = lax.broadcasted_iota(jnp.int32, (1, LANES), 1)
    rowi = lax.broadcasted_iota(jnp.int32, (BLK, 1), 0)
    head0_l = lane < HEAD_DIM
    head0_r = rowi < HEAD_DIM
    same_head = head0_r == head0_l
    rel = (lax.broadcasted_iota(jnp.int32, (BLK, BLK), 0)
           - lax.broadcasted_iota(jnp.int32, (BLK, BLK), 1)).astype(F32)
    pos_r = rowi.astype(F32)
    pos_l = lane.astype(F32)

    def log_gamma(col):
        x = dec_ref[0, :, col:col + 1]
        return jnp.minimum(x, 0.0) - jnp.log1p(jnp.exp(-jnp.abs(x)))

    lg = [log_gamma(n) for n in range(4)]
    lg_l = [jnp.where(head0_l, lg[2 * d], lg[2 * d + 1]) for d in range(2)]
    lg_r = [jnp.where(head0_r, lg[2 * d], lg[2 * d + 1]) for d in range(2)]
    for hh in range(2):
        tab_ref[hh] = (jnp.where(rel >= 0, jnp.exp(rel * lg[hh]), 0.0)
                       + jnp.where(rel <= 0, jnp.exp(-rel * lg[2 + hh]), 0.0))
    tab_ref[2] = jnp.exp((pos_r + 1.0) * lg_l[0])
    tab_ref[3] = jnp.exp((BLK - pos_r) * lg_l[1])
    tab_ref[4] = jnp.exp((BLK - 1.0 - pos_l) * lg_r[0])
    tab_ref[5] = jnp.exp(pos_l * lg_r[1])
    chunk_decay = [jnp.exp(BLK * lg_l[d]) for d in range(2)]

    def local(c, carry):
        r0 = pl.multiple_of(c * BLK, BLK)
        q = q_ref[0, pl.ds(r0, BLK), :]
        kt = kt_ref[0, c]
        v = v_ref[0, pl.ds(r0, BLK), :]
        zq, zv = jnp.zeros_like(q), jnp.zeros_like(v)
        qq = jnp.concatenate([jnp.where(head0_l, q, zq), jnp.where(head0_l, zq, q)], axis=0)
        p = jnp.dot(qq, kt, preferred_element_type=F32)
        inner = jnp.concatenate([p[:BLK] * tab_ref[0], p[BLK:] * tab_ref[1]], axis=1).astype(BF16)
        vv = jnp.concatenate([jnp.where(head0_l, v, zv), jnp.where(head0_l, zv, v)], axis=0)
        oi_ref[pl.ds(r0, BLK), :] = jnp.dot(inner, vv, preferred_element_type=F32)
        ktf = kt.astype(F32)
        kk = jnp.concatenate([ktf * tab_ref[4], ktf * tab_ref[5]], axis=0).astype(BF16)
        kv = jnp.dot(kk, v, preferred_element_type=F32)
        kv_ref[c, :BLK] = jnp.where(same_head, kv[:BLK], 0.0)
        kv_ref[c, BLK:] = jnp.where(same_head, kv[BLK:], 0.0)
        return carry

    lax.fori_loop(0, N_ALL_BLK, local, 0, unroll=8)

    def scan(i, states):
        sf, sb = states
        cf = jnp.where(i < n_ctx, N_LAT_BLK + i, i - n_ctx)
        cb = N_ALL_BLK - 1 - i
        st_ref[cf, :BLK] = sf.astype(BF16)
        st_ref[cb, BLK:] = sb.astype(BF16)
        return sf * chunk_decay[0] + kv_ref[cf, :BLK], sb * chunk_decay[1] + kv_ref[cb, BLK:]

    zero = jnp.zeros((BLK, LANES), F32)
    lax.fori_loop(0, N_ALL_BLK, scan, (zero, zero))

    def finish(c, carry):
        r0 = pl.multiple_of(c * BLK, BLK)
        qf = q_ref[0, pl.ds(r0, BLK), :].astype(F32)
        qd = jnp.concatenate([qf * tab_ref[2], qf * tab_ref[3]], axis=1).astype(BF16)
        tot = oi_ref[pl.ds(r0, BLK), :] + jnp.dot(qd, st_ref[c], preferred_element_type=F32)
        sq = tot * tot
        ms0 = jnp.sum(jnp.where(head0_l, sq, 0.0), axis=1, keepdims=True) * (1.0 / HEAD_DIM)
        ms1 = jnp.sum(jnp.where(head0_l, 0.0, sq), axis=1, keepdims=True) * (1.0 / HEAD_DIM)
        inv = jnp.where(head0_l, lax.rsqrt(ms0 + EPS), lax.rsqrt(ms1 + EPS))
        o_ref[0, pl.ds(r0, BLK), :] = (tot * inv * gn_ref[...] * g_ref[0, pl.ds(r0, BLK), :]).astype(BF16)
        return carry

    lax.fori_loop(0, N_ALL_BLK if ctx_out else N_LAT_BLK, finish, 0, unroll=4)


def _retention(qkv, kt, gate, dec, gn, ctx_out):
    b = qkv.shape[0]
    t_out = T_ALL if ctx_out else SEQ
    col = lambda g: pl.BlockSpec((1, T_ALL, LANES), lambda i, p: (i, 0, g + p))
    return pl.pallas_call(
        functools.partial(_c_kernel, ctx_out=ctx_out),
        out_shape=jax.ShapeDtypeStruct((b, t_out, 2 * LANES), BF16),
        grid=(b, 2),
        in_specs=[
            pl.BlockSpec((1, 1, 4), lambda i, p: (p, 0, 0)),
            pl.BlockSpec((1, LANES), lambda i, p: (0, 0)),
            col(G_CQ),
            pl.BlockSpec((1, N_ALL_BLK, LANES, BLK), lambda i, p: (i, 0, p, 0)),
            col(G_CV),
            pl.BlockSpec((1, T_ALL, LANES), lambda i, p: (i, 0, p)),
        ],
        out_specs=pl.BlockSpec((1, t_out, LANES), lambda i, p: (i, 0, p)),
        scratch_shapes=[pltpu.VMEM((T_ALL, LANES), F32), pltpu.VMEM((N_ALL_BLK, 2 * BLK, LANES), F32),
                        pltpu.VMEM((N_ALL_BLK, 2 * BLK, LANES), BF16), pltpu.VMEM((6, BLK, BLK), F32)],
        compiler_params=_params(("parallel", "parallel")),
        name="retention",
    )(dec, gn, qkv, kt, qkv, gate)


def _d_kernel(q_ref, k_ref, v_ref, bias_ref, o_ref, *, ctx_queries):
    rows = SEQ // GRID_W
    n_grp = rows // D_GROUP
    gq = D_GROUP * GRID_W
    slab_rows = NA_ROWS + D_GROUP
    kx = k_ref[0, SEQ:T_ALL, :]
    vx = v_ref[0, SEQ:T_ALL, :]
    head0 = lax.broadcasted_iota(jnp.int32, (1, LANES), 1) < HEAD_DIM

    def attend(q, parts):
        n = q.shape[0]
        zq = jnp.zeros_like(q)
        qq = jnp.concatenate([jnp.where(head0, q, zq), jnp.where(head0, zq, q)], axis=0)
        scores = []
        for k, _, bias in parts:
            s = lax.dot_general(qq, k, _NT, preferred_element_type=F32)
            scores.append(s if bias is None else s + bias)
        m = functools.reduce(jnp.maximum, [jnp.max(s, axis=1, keepdims=True) for s in scores])
        es = [jnp.exp(s - m) for s in scores]
        den = sum(jnp.sum(e, axis=1, keepdims=True) for e in es)
        o = sum(jnp.dot(e.astype(BF16), v, preferred_element_type=F32)
                for e, (_, v, _) in zip(es, parts)) / den
        return jnp.where(head0, o[:n], o[n:]).astype(BF16)

    def group(g, carry):
        first_key_row = jnp.clip(g * D_GROUP - NA_ROWS // 2, 0, rows - slab_rows)
        start = pl.multiple_of(first_key_row * GRID_W, GRID_W)
        kind = jnp.where(g == 0, 0, jnp.where(g == n_grp - 1, 2, 1))
        q0 = pl.multiple_of(g * gq, gq)
        q = q_ref[0, pl.ds(q0, gq), :]
        ks = k_ref[0, pl.ds(start, slab_rows * GRID_W), :]
        vs = v_ref[0, pl.ds(start, slab_rows * GRID_W), :]
        o_ref[0, pl.ds(q0, gq), :] = attend(q, [(ks, vs, bias_ref[0, kind]), (kx, vx, None)])
        return carry

    lax.fori_loop(0, n_grp, group, 0, unroll=8)
    if ctx_queries:
        o_ref[0, SEQ:T_ALL, :] = attend(q_ref[0, SEQ:T_ALL, :], [(kx, vx, None)])


def _attn_d(qkv, bias, layer, ctx_queries):
    b = qkv.shape[0]
    t_out = T_ALL if ctx_queries else SEQ
    col = lambda g: pl.BlockSpec((1, T_ALL, LANES), lambda i, p: (i, 0, g + p))
    return pl.pallas_call(
        functools.partial(_d_kernel, ctx_queries=ctx_queries),
        out_shape=jax.ShapeDtypeStruct((b, t_out, 2 * LANES), BF16),
        grid=(b, 2),
        in_specs=[
            col(G_DQ), col(G_DK), col(G_DV),
            pl.BlockSpec((1,) + bias.shape[1:], lambda i, p: (2 * layer + p, 0, 0, 0)),
        ],
        out_specs=pl.BlockSpec((1, t_out, LANES), lambda i, p: (i, 0, p)),
        compiler_params=_params(("parallel", "parallel")),
        name="attn_nbr",
    )(qkv, qkv, qkv, bias)


def _outmlp_kernel(*refs, split_streams):
    x, refs = _stream_tile(refs, split_streams, 1)
    a_ref, b_ref, c_ref, d_ref, mod_ref, g2_ref, wo_ref, w1_ref, w2_ref, o_ref = refs
    gw = 2 * LANES
    mix = None
    for n, r in enumerate((a_ref, b_ref, c_ref, d_ref)):
        part = jnp.dot(r[0], wo_ref[n * gw:(n + 1) * gw, :], preferred_element_type=F32)
        mix = part if mix is None else mix + part
    x1 = x + mod_ref[0, 2:3, :] * mix
    h2 = (_rms(x1) * g2_ref[...] * (1.0 + mod_ref[0, 4:5, :]) + mod_ref[0, 3:4, :]).astype(BF16)
    acc = None
    for n in range(D_FF // D_MODEL):
        cols = slice(n * D_MODEL, (n + 1) * D_MODEL)
        hid = jnp.maximum(jnp.dot(h2, w1_ref[:, cols], preferred_element_type=F32), 0.0)
        part = jnp.dot((hid * hid).astype(BF16), w2_ref[cols, :], preferred_element_type=F32)
        acc = part if acc is None else acc + part
    o_ref[0] = x1 + mod_ref[0, 5:6, :] * acc


def _out_mlp(streams, outs, mod, g2, w_out, w1, w2, ctx_tokens, tm):
    b = streams[0].shape[0]
    split = len(streams) == 2
    assert ctx_tokens or not split
    t_out = T_ALL if ctx_tokens else SEQ
    nt = t_out // tm
    ctx_row = mod.shape[0] - 1
    ctx_tile = SEQ // tm

    def mod_map(i, t):
        return (jnp.where(t >= ctx_tile, ctx_row, i), 0, 0)

    tile = lambda w: pl.BlockSpec((1, tm, w), lambda i, t: (i, t, 0))
    whole = lambda s: pl.BlockSpec(s, lambda i, t: (0, 0), pipeline_mode=pl.Buffered(1))
    return pl.pallas_call(
        functools.partial(_outmlp_kernel, split_streams=split),
        out_shape=jax.ShapeDtypeStruct((b, t_out, D_MODEL), F32),
        grid=(b, nt),
        in_specs=_stream_specs(split, tm, lambda i, t: (i, t)) + [
            tile(2 * LANES), tile(2 * LANES), tile(2 * LANES), tile(2 * LANES),
            pl.BlockSpec((1, 6, D_MODEL), mod_map),
            whole((1, D_MODEL)), whole((D_MODEL, D_MODEL)), whole((D_MODEL, D_FF)), whole((D_FF, D_MODEL)),
        ],
        out_specs=tile(D_MODEL),
        compiler_params=_params(("parallel", "parallel")),
        name="out_mlp",
    )(*streams, *outs, mod, g2, w_out, w1, w2)


def _rope_tables():
    pos = np.arange(SEQ)
    rows, cols = (pos // GRID_W).astype(np.float32), (pos % GRID_W).astype(np.float32)

    def pattern(half):
        freqs = (np.float32(ROPE_BASE) ** (-np.arange(half, dtype=np.float32) / np.float32(half))).astype(np.float32)
        ang = [(p[:, None] * freqs[None, :]).astype(np.float32) for p in (rows, cols)]
        cos = np.concatenate([np.cos(a) for a in (ang[0], ang[0], ang[1], ang[1])], axis=1)
        sin = np.concatenate([-np.sin(ang[0]), np.sin(ang[0]), -np.sin(ang[1]), np.sin(ang[1])], axis=1)
        reps = LANES // (4 * half)
        cos = np.concatenate([np.tile(cos, (1, reps)), np.ones((CTX_LEN, LANES))], axis=0)
        sin = np.concatenate([np.tile(sin, (1, reps)), np.zeros((CTX_LEN, LANES))], axis=0)
        return jnp.asarray(cos, F32), jnp.asarray(sin, F32)

    ca, sa = pattern(HEAD_DIM // 4)
    cb, sb = pattern(B_QK_DIM // 4)
    return ca, sa, cb, sb


def _gain_row(aq, ak, bq, bk, dq, dk):
    one = jnp.ones((LANES,), F32)
    t = lambda g, s=1.0: jnp.tile(g.astype(F32), LANES // g.shape[0]) * s
    groups = [one] * N_GROUPS
    groups[G_AQ] = groups[G_AQ + 1] = t(aq, HEAD_DIM ** -0.5 * LOG2_E)
    groups[G_AK] = t(ak)
    groups[G_BQ] = groups[G_BQ + 1] = t(bq, B_QK_DIM ** -0.5 * LOG2_E)
    groups[G_BK] = groups[G_BK + 1] = t(bk)
    groups[G_CK] = groups[G_CK + 1] = one * HEAD_DIM ** -0.5
    groups[G_DQ] = groups[G_DQ + 1] = t(dq, HEAD_DIM ** -0.5)
    groups[G_DK] = groups[G_DK + 1] = t(dk)
    return jnp.concatenate(groups)[None, :]


def _nbr_bias(rpb):
    assert D_GROUP == NA_ROWS // 2
    qc = np.arange(GRID_W)[:, None]
    kc = np.arange(GRID_W)[None, :]
    cstart = np.clip(qc - NA_COLS // 2, 0, GRID_W - NA_COLS)
    valid = (kc >= cstart) & (kc < cstart + NA_COLS)
    pad = GRID_W - NA_COLS
    rp = jnp.pad(rpb.astype(F32), ((0, 0), (0, 0), (pad, pad)))
    period = rp.shape[-1]
    flat = jnp.tile(rp, (1, 1, GRID_W + 1))[..., :GRID_W * (period + 1)]
    shifted = flat.reshape(rp.shape[0], rp.shape[1], GRID_W, period + 1)
    toep = shifted[:, :, ::-1, :GRID_W]
    toep = jnp.where(valid[None, None], toep, NEG_INF).transpose(0, 2, 1, 3)
    h = rpb.shape[0]
    half = NA_ROWS // 2

    def rows_of(dr0, offset):
        blk = toep[:, :, dr0:dr0 + NA_ROWS]
        blk = jnp.pad(blk, ((0, 0), (0, 0), (offset, D_GROUP - offset), (0, 0)), constant_values=NEG_INF)
        return blk.reshape(h, GRID_W, (NA_ROWS + D_GROUP) * GRID_W)

    kinds = [[rows_of(NA_ROWS - 1 - a, 0) for a in range(D_GROUP)],
             [rows_of(half - 1, a) for a in range(D_GROUP)],
             [rows_of(half - 1 - a, D_GROUP) for a in range(D_GROUP)]]
    b = jnp.stack([jnp.stack(k, axis=1) for k in kinds], axis=1)
    b = b.reshape(h // 2, 2, 3, D_GROUP * GRID_W, -1).transpose(0, 2, 1, 3, 4)
    return b.reshape(h // 2, 3, 2 * D_GROUP * GRID_W, -1)


def kernel(x, c, ctx, c_ctx, w_mod, b_mod, norm1_g, norm2_g, w_in, w_out, a_qnorm_g, a_knorm_g, a_sink, b_qnorm_g, b_knorm_g, b_lambda_q1, b_lambda_k1, b_lambda_q2, b_lambda_k2, b_subln_g, c_decay_fwd, c_decay_bwd, c_gn_g, d_qnorm_g, d_knorm_g, d_rpb, w_mlp1, w_mlp2):
    depth = w_mod.shape[0]
    b = x.shape[0]
    assert b + 1 <= 8
    cc = jnp.concatenate([c, c_ctx[None, :], jnp.zeros((8 - b - 1, D_MODEL), F32)], axis=0)
    mods = _modulation(cc, w_mod, b_mod)[:, :b + 1].reshape(depth, b + 1, 6, D_MODEL)
    tabs = _rope_tables()
    nbr_bias = _nbr_bias(d_rpb.reshape((-1,) + d_rpb.shape[2:]))
    streams = (x, ctx)

    for layer in range(depth):
        need_ctx = layer < depth - 1
        lambda_init = 0.8 - 0.6 * math.exp(-0.3 * layer)
        gain = _gain_row(a_qnorm_g[layer], a_knorm_g[layer], b_qnorm_g[layer], b_knorm_g[layer],
                         d_qnorm_g[layer], d_knorm_g[layer])
        qkv, gate, akv, kt_b, kt_c = _in_proj(streams, mods[layer], norm1_g[layer][None, :],
                                              w_in[layer].astype(BF16), gain, tabs)

        out_a = _attn_a(qkv, akv, a_sink[layer].astype(F32), need_ctx)
        lam_vecs = jnp.stack([b_lambda_q1[layer], b_lambda_k1[layer],
                              b_lambda_q2[layer], b_lambda_k2[layer]]).astype(F32)
        out_b = _attn_b(qkv, kt_b, lam_vecs, b_subln_g[layer][None, :], need_ctx, lambda_init)
        dec = jnp.stack([c_decay_fwd[layer].reshape(2, 2), c_decay_bwd[layer].reshape(2, 2)],
                        axis=1).reshape(2, 1, 4).astype(F32)
        out_c = _retention(qkv, kt_c, gate, dec, jnp.tile(c_gn_g[layer], 2)[None, :], need_ctx)
        out_d = _attn_d(qkv, nbr_bias, layer, need_ctx)

        streams = (_out_mlp(streams, (out_a, out_b, out_c, out_d), mods[layer], norm2_g[layer][None, :],
                            w_out[layer].astype(BF16), w_mlp1[layer].astype(BF16), w_mlp2[layer].astype(BF16),
                            need_ctx, tm=CTX_LEN if need_ctx else 2 * CTX_LEN),)
    return streams[0]
```

```python
import functools
import math

import numpy as np
import jax
import jax.numpy as jnp
from jax import lax
from jax.experimental import pallas as pl
from jax.experimental.pallas import tpu as pltpu

F32 = jnp.float32
BF16 = jnp.bfloat16

D_MODEL = 1024
SEQ = 4096
CTX_LEN = 256
T_ALL = SEQ + CTX_LEN
GRID_W = 64
HEAD_DIM = 64
B_QK_DIM = 32
BLK = 128
N_LAT_BLK = SEQ // BLK
N_ALL_BLK = T_ALL // BLK
B_KEY_CHUNK = 256
IN_PROJ_BATCH = 2
D_GROUP = 4
B_BLOCKS_PER_TRIP = 4
B_ROW_SLAB = 32
B_SAFE_BOUND = 60.0
B_BOUND_SLACK = 1.0 + 2.0 ** -10
WINDOW = 128
NA_ROWS = 8
NA_COLS = 16
D_FF = 4 * D_MODEL
IN_WIDTH = 3072
ROPE_BASE = 10000.0
EPS = 1e-6
NEG_INF = -1e30
LOG2_E = math.log2(math.e)
LANES = 128
VMEM_LIMIT = 52 * 1024 * 1024

G_AQ, G_AK, G_AV = 0, 2, 3
G_BQ, G_BK, G_BV = 4, 6, 8
G_CQ, G_CK, G_CV, G_CG = 10, 12, 14, 16
G_DQ, G_DK, G_DV = 18, 20, 22
N_GROUPS = IN_WIDTH // LANES

_NT = (((1,), (1,)), ((), ()))
_TN = (((0,), (0,)), ((), ()))


def _params(sem):
    return pltpu.CompilerParams(dimension_semantics=sem, vmem_limit_bytes=VMEM_LIMIT)


def _rms(x):
    return x * lax.rsqrt(jnp.mean(x * x, axis=-1, keepdims=True) + EPS)


def _mod_kernel(c_ref, w_ref, b_ref, o_ref):
    c = c_ref[...]
    s = c / (1.0 + jnp.exp(-c))
    s_hi = s.astype(BF16)
    s_lo = (s - s_hi.astype(F32)).astype(BF16)
    w = w_ref[0]
    w_hi = w.astype(BF16)
    w_lo = (w - w_hi.astype(F32)).astype(BF16)
    acc = jnp.dot(s_hi, w_hi, preferred_element_type=F32)
    acc += jnp.dot(s_lo, w_hi, preferred_element_type=F32)
    acc += jnp.dot(s_hi, w_lo, preferred_element_type=F32)
    o_ref[0] = acc + b_ref[0]


def _modulation(cc, w_mod, b_mod):
    depth = w_mod.shape[0]
    n_col = 6 * D_MODEL // 1024
    return pl.pallas_call(
        _mod_kernel,
        out_shape=jax.ShapeDtypeStruct((depth, 8, 6 * D_MODEL), F32),
        grid=(depth, n_col),
        in_specs=[
            pl.BlockSpec((8, D_MODEL), lambda l, j: (0, 0)),
            pl.BlockSpec((1, D_MODEL, 1024), lambda l, j: (l, 0, j)),
            pl.BlockSpec((1, 1, 1024), lambda l, j: (l, 0, j)),
        ],
        out_specs=pl.BlockSpec((1, 8, 1024), lambda l, j: (l, 0, j)),
        compiler_params=_params(("parallel", "parallel")),
        name="adaln_mod",
    )(cc, w_mod, b_mod.reshape(depth, 1, 6 * D_MODEL))


def _seg_inv_rms(v, seg):
    sq = v * v
    hi = sq.astype(BF16)
    lo = (sq - hi.astype(F32)).astype(BF16)
    r = lax.broadcasted_iota(jnp.int32, (LANES, LANES), 0) // seg
    c = lax.broadcasted_iota(jnp.int32, (LANES, LANES), 1) // seg
    bd = jnp.where(r == c, 1.0, 0.0).astype(BF16)
    ssum = jnp.dot(hi, bd, preferred_element_type=F32) + jnp.dot(lo, bd, preferred_element_type=F32)
    return lax.rsqrt(ssum * (1.0 / seg) + EPS)


def _rotate(v, cos, sin_signed, half):
    up = pltpu.roll(v, LANES - half, 1)
    dn = pltpu.roll(v, half, 1)
    lane = lax.broadcasted_iota(jnp.int32, v.shape, 1)
    partner = jnp.where(lane % (2 * half) < half, up, dn)
    return v * cos + partner * sin_signed


def _stream_tiles(refs, split_streams, tile_axis):
    n = refs[0].shape[0]
    if not split_streams:
        return [refs[0][e] for e in range(n)], refs[1:]
    is_ctx = pl.program_id(tile_axis) == pl.num_programs(tile_axis) - 1
    return [jnp.where(is_ctx, refs[1][e], refs[0][e]) for e in range(n)], refs[2:]


def _inproj_kernel(*refs, split_streams):
    xs, refs = _stream_tiles(refs, split_streams, 0)
    mod_ref, g1_ref, w_ref, gain_ref, ca_ref, sa_ref, cb_ref, sb_ref = refs[:8]
    qkv_ref, gate_ref, akv_ref, ktb_ref, ktc_ref = refs[8:]
    nb, tm = len(xs), xs[0].shape[0]
    h = jnp.concatenate([(_rms(x) * g1_ref[...] * (1.0 + mod_ref[e, 1:2, :]) + mod_ref[e, 0:1, :]).astype(BF16)
                         for e, x in enumerate(xs)], axis=0)
    acc = jnp.dot(h, w_ref[...], preferred_element_type=F32)
    rope_a = [jnp.concatenate([r[...]] * nb, axis=0) for r in (ca_ref, sa_ref)]
    rope_b = [jnp.concatenate([r[...]] * nb, axis=0) for r in (cb_ref, sb_ref)]
    rows = [slice(e * tm, (e + 1) * tm) for e in range(nb)]
    head0 = lax.broadcasted_iota(jnp.int32, (1, LANES), 1) < HEAD_DIM
    for g in range(N_GROUPS):
        cols = slice(g * LANES, (g + 1) * LANES)
        v = acc[:, cols]
        gain = gain_ref[:, cols]
        if g in (G_AQ, G_AQ + 1, G_AK):
            v = v * _seg_inv_rms(v, HEAD_DIM) * gain
            v = _rotate(v, *rope_a, HEAD_DIM // 4)
        elif g in (G_BQ, G_BQ + 1, G_BK, G_BK + 1):
            v = v * _seg_inv_rms(v, B_QK_DIM) * gain
            v = _rotate(v, *rope_b, B_QK_DIM // 4)
        elif g in (G_DQ, G_DQ + 1, G_DK, G_DK + 1):
            v = v * _seg_inv_rms(v, HEAD_DIM) * gain
        elif g in (G_CK, G_CK + 1):
            v = v * gain
        elif g in (G_CG, G_CG + 1):
            gcols = slice((g - G_CG) * LANES, (g - G_CG + 1) * LANES)
            silu = v / (1.0 + jnp.exp(-v))
            for e in range(nb):
                gate_ref[e, :, gcols] = silu[rows[e]]
        vb = v.astype(BF16)
        for e in range(nb):
            qkv_ref[e, :, cols] = vb[rows[e]]
        if g in (G_AK, G_AV):
            n = 2 * (0 if g == G_AK else 1)
            swapped = pltpu.roll(v, HEAD_DIM, 1)
            for e in range(nb):
                akv_ref[e, :, n * LANES:(n + 1) * LANES] = jnp.where(head0, v, swapped)[rows[e]].astype(BF16)
                akv_ref[e, :, (n + 1) * LANES:(n + 2) * LANES] = jnp.where(head0, swapped, v)[rows[e]].astype(BF16)
        elif g in (G_BK, G_BK + 1):
            r0 = (g - G_BK) * LANES
            for e in range(nb):
                ktb_ref[e, r0:r0 + LANES, :] = v[rows[e]].T.astype(BF16)
        elif g in (G_CK, G_CK + 1):
            r0 = (g - G_CK) * LANES
            for e in range(nb):
                for c in range(tm // BLK):
                    chunk = v[e * tm + c * BLK:e * tm + (c + 1) * BLK]
                    ktc_ref[e, c, r0:r0 + LANES, :] = chunk.T.astype(BF16)


def _stream_specs(split_streams, tm, order, nb=1):
    if not split_streams:
        return [pl.BlockSpec((nb, tm, D_MODEL), lambda *g: (order(*g)[0], order(*g)[1], 0))]
    last = SEQ // tm - 1
    return [pl.BlockSpec((nb, tm, D_MODEL), lambda *g: (order(*g)[0], jnp.minimum(order(*g)[1], last), 0)),
            pl.BlockSpec((nb, tm, D_MODEL), lambda *g: (order(*g)[0], 0, 0))]


def _in_proj(streams, mod, g1, w_in, gain, tabs, tm=CTX_LEN):
    b = streams[0].shape[0]
    nb = IN_PROJ_BATCH
    assert b % nb == 0 and mod.shape[0] == b + nb
    nt = T_ALL // tm

    def mod_map(t, i):
        return (jnp.where(t == nt - 1, b // nb, i), 0, 0)

    tab_spec = pl.BlockSpec((tm, LANES), lambda t, i: (t, 0))
    split = len(streams) == 2
    return pl.pallas_call(
        functools.partial(_inproj_kernel, split_streams=split),
        out_shape=(jax.ShapeDtypeStruct((b, T_ALL, IN_WIDTH), BF16),
                   jax.ShapeDtypeStruct((b, T_ALL, 2 * LANES), F32),
                   jax.ShapeDtypeStruct((b, T_ALL, 4 * LANES), BF16),
                   jax.ShapeDtypeStruct((b, 2 * LANES, T_ALL), BF16),
                   jax.ShapeDtypeStruct((b, N_ALL_BLK, 2 * LANES, BLK), BF16)),
        grid=(nt, b // nb),
        in_specs=_stream_specs(split, tm, lambda t, i: (i, t), nb) + [
            pl.BlockSpec((nb, 6, D_MODEL), mod_map),
            pl.BlockSpec((1, D_MODEL), lambda t, i: (0, 0)),
            pl.BlockSpec((D_MODEL, IN_WIDTH), lambda t, i: (0, 0)),
            pl.BlockSpec((1, IN_WIDTH), lambda t, i: (0, 0)),
            tab_spec, tab_spec, tab_spec, tab_spec,
        ],
        out_specs=(pl.BlockSpec((nb, tm, IN_WIDTH), lambda t, i: (i, t, 0)),
                   pl.BlockSpec((nb, tm, 2 * LANES), lambda t, i: (i, t, 0)),
                   pl.BlockSpec((nb, tm, 4 * LANES), lambda t, i: (i, t, 0)),
                   pl.BlockSpec((nb, 2 * LANES, tm), lambda t, i: (i, 0, t)),
                   pl.BlockSpec((nb, tm // BLK, 2 * LANES, BLK), lambda t, i: (i, t, 0, 0))),
        compiler_params=_params(("parallel", "parallel")),
        name="in_proj",
    )(*streams, mod, g1, w_in, gain, *tabs)


def _a_kernel(sink_ref, q_ref, k_ref, v_ref, o_ref, mask_ref, *, ctx_queries):
    p = pl.program_id(1)
    nq = 2 * BLK
    strip = 3 * BLK
    head0 = lax.broadcasted_iota(jnp.int32, (1, LANES), 1) < HEAD_DIM
    row = lax.broadcasted_iota(jnp.int32, (nq, 1), 0)
    sink_col = jnp.where(row < BLK, sink_ref[2 * p], sink_ref[2 * p + 1]) * LOG2_E
    kx = k_ref[0, SEQ:T_ALL, :]
    vx = v_ref[0, SEQ:T_ALL, :]
    dist = (lax.broadcasted_iota(jnp.int32, (nq, strip), 0) % BLK
            - lax.broadcasted_iota(jnp.int32, (nq, strip), 1))
    for lead in range(3):
        mask_ref[lead] = jnp.where(jnp.abs(dist + lead * BLK) <= WINDOW, 0.0, NEG_INF)

    def stacked(q):
        zq = jnp.zeros_like(q)
        return jnp.concatenate([jnp.where(head0, q, zq), jnp.where(head0, zq, q)], axis=0)

    def attend(parts):
        m = functools.reduce(jnp.maximum, [jnp.max(s, axis=1, keepdims=True) for s, _ in parts] + [sink_col])
        es = [jnp.exp2(s - m) for s, _ in parts]
        den = sum(jnp.sum(e, axis=1, keepdims=True) for e in es) + jnp.exp2(sink_col - m)
        o = sum(jnp.dot(e.astype(BF16), v, preferred_element_type=F32) for e, (_, v) in zip(es, parts)) / den
        return jnp.where(head0, o[:BLK], o[BLK:]).astype(BF16)

    def block(j, carry):
        q0 = pl.multiple_of(j * BLK, BLK)
        first = jnp.clip(j - 1, 0, N_LAT_BLK - 3)
        start = pl.multiple_of(first * BLK, BLK)
        qq = stacked(q_ref[0, pl.ds(q0, BLK), :])
        ks = k_ref[0, pl.ds(start, strip), :]
        vs = v_ref[0, pl.ds(start, strip), :]
        s_win = lax.dot_general(qq, ks, _NT, preferred_element_type=F32) + mask_ref[j - first]
        s_ctx = lax.dot_general(qq, kx, _NT, preferred_element_type=F32)
        o_ref[0, pl.ds(q0, BLK), :] = attend([(s_ctx, vx), (s_win, vs)])
        return carry

    lax.fori_loop(0, N_LAT_BLK, block, 0, unroll=8)
    if ctx_queries:
        for q0 in range(SEQ, T_ALL, BLK):
            qq = stacked(q_ref[0, q0:q0 + BLK, :])
            s_ctx = lax.dot_general(qq, kx, _NT, preferred_element_type=F32)
            o_ref[0, q0:q0 + BLK, :] = attend([(s_ctx, vx)])


def _attn_a(qkv, akv, sink, ctx_queries):
    b = qkv.shape[0]
    t_out = T_ALL if ctx_queries else SEQ
    return pl.pallas_call(
        functools.partial(_a_kernel, ctx_queries=ctx_queries),
        out_shape=jax.ShapeDtypeStruct((b, t_out, 2 * LANES), BF16),
        grid=(b, 2),
        in_specs=[
            pl.BlockSpec(memory_space=pltpu.SMEM),
            pl.BlockSpec((1, T_ALL, LANES), lambda i, p: (i, 0, G_AQ + p)),
            pl.BlockSpec((1, T_ALL, LANES), lambda i, p: (i, 0, p)),
            pl.BlockSpec((1, T_ALL, LANES), lambda i, p: (i, 0, 2 + p)),
        ],
        out_specs=pl.BlockSpec((1, t_out, LANES), lambda i, p: (i, 0, p)),
        scratch_shapes=[pltpu.VMEM((3, 2 * BLK, 3 * BLK), F32)],
        compiler_params=_params(("parallel", "parallel")),
        name="attn_window",
    )(sink, qkv, akv, akv)


def _b_kernel(lam_ref, subg_ref, q_ref, kt_ref, v_ref, o_ref, qq_ref, m_ref, l_ref, acc_ref, s_ref,
              *, ctx_queries, lambda_init):
    lv = lam_ref[...]
    lam = (jnp.exp(jnp.sum(lv[0:1] * lv[1:2], axis=1, keepdims=True))
           - jnp.exp(jnp.sum(lv[2:3] * lv[3:4], axis=1, keepdims=True)) + lambda_init)
    lane = lax.broadcasted_iota(jnp.int32, (1, LANES), 1)
    rows = 4 * BLK
    mine = [(lane >= n * B_QK_DIM) & (lane < (n + 1) * B_QK_DIM) for n in range(4)]

    ksq = kt_ref[0].astype(F32)
    ksq = ksq * ksq
    k_max2 = [jnp.max(jnp.sum(ksq[n * B_QK_DIM:(n + 1) * B_QK_DIM], axis=0, keepdims=True), axis=1, keepdims=True)
              for n in range(4)]

    def chunks_of(k0, nk):
        return [(c, min(B_KEY_CHUNK, k0 + nk - c)) for c in range(k0, k0 + nk, B_KEY_CHUNK)]

    def tiles(t):
        return [t[:, c:c + LANES] for c in range(0, t.shape[1], LANES)]

    def row_norm2(blk, top):
        qf = q_ref[0, pl.ds(pl.multiple_of(blk * BLK, BLK), BLK), :].astype(F32)
        return jnp.maximum(top, jnp.sum(qf * qf, axis=1, keepdims=True))

    n_blk = N_ALL_BLK if ctx_queries else N_LAT_BLK
    q_max2 = jnp.max(lax.fori_loop(0, n_blk, row_norm2, jnp.zeros((BLK, 1), F32), unroll=True),
                     axis=0, keepdims=True)
    loose = jnp.sqrt(q_max2 * functools.reduce(jnp.maximum, k_max2)) * B_BOUND_SLACK
    usable = jnp.max(loose) < B_SAFE_BOUND

    def prepare(blk, par, with_bound):
        q = q_ref[0, pl.ds(pl.multiple_of(blk * BLK, BLK), BLK), :]
        qf = q.astype(F32)
        sq = qf * qf
        for n in range(4):
            qq_ref[par, n * BLK:(n + 1) * BLK] = jnp.where(mine[n], q, jnp.zeros_like(q))
            if with_bound:
                q_norm2 = jnp.sum(jnp.where(mine[n], sq, 0.0), axis=1, keepdims=True)
                bound = jnp.sqrt(q_norm2 * k_max2[n]) * B_BOUND_SLACK
                m_ref[par, n * BLK:(n + 1) * BLK] = jnp.broadcast_to(bound, (BLK, LANES))

    def prob_chunk(par, n, c0, cn, s):
        parts = []
        for r in range(0, rows, B_ROW_SLAB):
            m = m_ref[par, r:r + B_ROW_SLAB]
            es = [jnp.exp2(t - m) for t in tiles(s[r:r + B_ROW_SLAB])]
            t = functools.reduce(jnp.add, es)
            l_ref[par, r:r + B_ROW_SLAB] = t if n == 0 else l_ref[par, r:r + B_ROW_SLAB] + t
            parts.append(jnp.concatenate(es, axis=1).astype(BF16))
        part = jnp.dot(jnp.concatenate(parts, axis=0), v_ref[0, c0:c0 + cn, :],
                       preferred_element_type=F32)
        acc_ref[par] = part if n == 0 else acc_ref[par] + part

    def finish(blk, par):
        o = acc_ref[par] * (1.0 / jnp.sum(l_ref[par], axis=1, keepdims=True))
        outs = []
        for hh in range(2):
            r0 = 2 * hh * BLK
            oh = o[r0:r0 + BLK] - lam * o[r0 + BLK:r0 + 2 * BLK]
            oh = oh[:, hh * HEAD_DIM:(hh + 1) * HEAD_DIM]
            outs.append(_rms(oh) * subg_ref[...] * (1.0 - lambda_init))
        o_ref[0, pl.ds(pl.multiple_of(blk * BLK, BLK), BLK), :] = jnp.concatenate(outs, axis=1).astype(BF16)

    def one_sweep(blk, par, chunks):
        for n, (c0, cn) in enumerate(chunks):
            s = jnp.dot(qq_ref[par], kt_ref[0, :, c0:c0 + cn], preferred_element_type=F32)
            prob_chunk(par, n, c0, cn, s)
        finish(blk, par)

    def two_sweeps(blk, par, chunks):
        row_max = None
        for c0, cn in chunks:
            s = jnp.dot(qq_ref[par], kt_ref[0, :, c0:c0 + cn], preferred_element_type=F32)
            s_ref[:, c0:c0 + cn] = s
            t = functools.reduce(jnp.maximum, tiles(s))
            row_max = t if row_max is None else jnp.maximum(row_max, t)
        m_ref[par] = jnp.broadcast_to(jnp.max(row_max, axis=1, keepdims=True), (rows, LANES))
        for n, (c0, cn) in enumerate(chunks):
            prob_chunk(par, n, c0, cn, s_ref[:, c0:c0 + cn])
        finish(blk, par)

    def block_group(first_blk, n_blk, chunks, fast):
        for par in range(n_blk):
            prepare(first_blk + par, par, fast)
        for par in range(n_blk):
            (one_sweep if fast else two_sweeps)(first_blk + par, par, chunks)

    def all_blocks(fast):
        per_trip = B_BLOCKS_PER_TRIP if fast else 1

        def body(n, carry):
            block_group(per_trip * n, per_trip, chunks_of(0, T_ALL), fast)
            return carry

        lax.fori_loop(0, N_LAT_BLK // per_trip, body, 0)
        if ctx_queries:
            block_group(N_LAT_BLK, N_ALL_BLK - N_LAT_BLK, chunks_of(SEQ, CTX_LEN), fast)

    pl.when(usable)(lambda: all_blocks(True))
    pl.when(jnp.logical_not(usable))(lambda: all_blocks(False))


def _attn_b(qkv, kt, lam_vecs, subg, ctx_queries, lambda_init):
    b = qkv.shape[0]
    t_out = T_ALL if ctx_queries else SEQ
    assert N_LAT_BLK % B_BLOCKS_PER_TRIP == 0 and N_ALL_BLK - N_LAT_BLK <= B_BLOCKS_PER_TRIP
    stacked = lambda dt: pltpu.VMEM((B_BLOCKS_PER_TRIP, 4 * BLK, LANES), dt)
    return pl.pallas_call(
        functools.partial(_b_kernel, ctx_queries=ctx_queries, lambda_init=lambda_init),
        out_shape=jax.ShapeDtypeStruct((b, t_out, 2 * LANES), BF16),
        grid=(b, 2),
        in_specs=[
            pl.BlockSpec((4, B_QK_DIM), lambda i, p: (0, 0)),
            pl.BlockSpec((1, HEAD_DIM), lambda i, p: (0, 0)),
            pl.BlockSpec((1, T_ALL, LANES), lambda i, p: (i, 0, G_BQ + p)),
            pl.BlockSpec((1, LANES, T_ALL), lambda i, p: (i, p, 0)),
            pl.BlockSpec((1, T_ALL, LANES), lambda i, p: (i, 0, G_BV + p)),
        ],
        out_specs=pl.BlockSpec((1, t_out, LANES), lambda i, p: (i, 0, p)),
        scratch_shapes=[stacked(BF16), stacked(F32), stacked(F32), stacked(F32),
                        pltpu.VMEM((4 * BLK, T_ALL), F32)],
        compiler_params=_params(("parallel", "parallel")),
        name="attn_diff",
    )(lam_vecs, subg, qkv, kt, qkv)


def _c_kernel(dec_ref, gn_ref, q_ref, kt_ref, v_ref, g_ref, o_ref, oi_ref, kv_ref, st_ref, tab_ref,
              *, ctx_out):
    n_ctx = CTX_LEN // BLK
    lane = lax.broadcasted_iota(jnp.int32, (1, LANES), 1)
    rowi = lax.broadcasted_iota(jnp.int32, (BLK, 1), 0)
    head0_l = lane < HEAD_DIM
    head0_r = rowi < HEAD_DIM
    same_head = head0_r == head0_l
    rel = (lax.broadcasted_iota(jnp.int32, (BLK, BLK), 0)
           - lax.broadcasted_iota(jnp.int32, (BLK, BLK), 1)).astype(F32)
    pos_r = rowi.astype(F32)
    pos_l = lane.astype(F32)

    def log_gamma(col):
        x = dec_ref[0, :, col:col + 1]
        return jnp.minimum(x, 0.0) - jnp.log1p(jnp.exp(-jnp.abs(x)))

    lg = [log_gamma(n) for n in range(4)]
    lg_l = [jnp.where(head0_l, lg[2 * d], lg[2 * d + 1]) for d in range(2)]
    lg_r = [jnp.where(head0_r, lg[2 * d], lg[2 * d + 1]) for d in range(2)]
    for hh in range(2):
        tab_ref[hh] = (jnp.where(rel >= 0, jnp.exp(rel * lg[hh]), 0.0)
                       + jnp.where(rel <= 0, jnp.exp(-rel * lg[2 + hh]), 0.0))
    tab_ref[2] = jnp.exp((pos_r + 1.0) * lg_l[0])
    tab_ref[3] = jnp.exp((BLK - pos_r) * lg_l[1])
    tab_ref[4] = jnp.exp((BLK - 1.0 - pos_l) * lg_r[0])
    tab_ref[5] = jnp.exp(pos_l * lg_r[1])
    chunk_decay = [jnp.exp(BLK * lg_l[d]) for d in range(2)]

    def local(c, carry):
        r0 = pl.multiple_of(c * BLK, BLK)
        q = q_ref[0, pl.ds(r0, BLK), :]
        kt = kt_ref[0, c]
        v = v_ref[0, pl.ds(r0, BLK), :]
        zq, zv = jnp.zeros_like(q), jnp.zeros_like(v)
        qq = jnp.concatenate([jnp.where(head0_l, q, zq), jnp.where(head0_l, zq, q)], axis=0)
        p = jnp.dot(qq, kt, preferred_element_type=F32)
        inner = jnp.concatenate([p[:BLK] * tab_ref[0], p[BLK:] * tab_ref[1]], axis=1).astype(BF16)
        vv = jnp.concatenate([jnp.where(head0_l, v, zv), jnp.where(head0_l, zv, v)], axis=0)
        oi_ref[pl.ds(r0, BLK), :] = jnp.dot(inner, vv, preferred_element_type=F32)
        ktf = kt.astype(F32)
        kk = jnp.concatenate([ktf * tab_ref[4], ktf * tab_ref[5]], axis=0).astype(BF16)
        kv = jnp.dot(kk, v, preferred_element_type=F32)
        kv_ref[c, :BLK] = jnp.where(same_head, kv[:BLK], 0.0)
        kv_ref[c, BLK:] = jnp.where(same_head, kv[BLK:], 0.0)
        return carry

    lax.fori_loop(0, N_ALL_BLK, local, 0, unroll=8)

    def scan(i, states):
        sf, sb = states
        cf = jnp.where(i < n_ctx, N_LAT_BLK + i, i - n_ctx)
        cb = N_ALL_BLK - 1 - i
        st_ref[cf, :BLK] = sf.astype(BF16)
        st_ref[cb, BLK:] = sb.astype(BF16)
        return sf * chunk_decay[0] + kv_ref[cf, :BLK], sb * chunk_decay[1] + kv_ref[cb, BLK:]

    zero = jnp.zeros((BLK, LANES), F32)
    lax.fori_loop(0, N_ALL_BLK, scan, (zero, zero))

    def finish(c, carry):
        r0 = pl.multiple_of(c * BLK, BLK)
        qf = q_ref[0, pl.ds(r0, BLK), :].astype(F32)
        qd = jnp.concatenate([qf * tab_ref[2], qf * tab_ref[3]], axis=1).astype(BF16)
        tot = oi_ref[pl.ds(r0, BLK), :] + jnp.dot(qd, st_ref[c], preferred_element_type=F32)
        sq = tot * tot
        ms0 = jnp.sum(jnp.where(head0_l, sq, 0.0), axis=1, keepdims=True) * (1.0 / HEAD_DIM)
        ms1 = jnp.sum(jnp.where(head0_l, 0.0, sq), axis=1, keepdims=True) * (1.0 / HEAD_DIM)
        inv = jnp.where(head0_l, lax.rsqrt(ms0 + EPS), lax.rsqrt(ms1 + EPS))
        o_ref[0, pl.ds(r0, BLK), :] = (tot * inv * gn_ref[...] * g_ref[0, pl.ds(r0, BLK), :]).astype(BF16)
        return carry

    lax.fori_loop(0, N_ALL_BLK if ctx_out else N_LAT_BLK, finish, 0, unroll=4)


def _retention(qkv, kt, gate, dec, gn, ctx_out):
    b = qkv.shape[0]
    t_out = T_ALL if ctx_out else SEQ
    col = lambda g: pl.BlockSpec((1, T_ALL, LANES), lambda i, p: (i, 0, g + p))
    return pl.pallas_call(
        functools.partial(_c_kernel, ctx_out=ctx_out),
        out_shape=jax.ShapeDtypeStruct((b, t_out, 2 * LANES), BF16),
        grid=(b, 2),
        in_specs=[
            pl.BlockSpec((1, 1, 4), lambda i, p: (p, 0, 0)),
            pl.BlockSpec((1, LANES), lambda i, p: (0, 0)),
            col(G_CQ),
            pl.BlockSpec((1, N_ALL_BLK, LANES, BLK), lambda i, p: (i, 0, p, 0)),
            col(G_CV),
            pl.BlockSpec((1, T_ALL, LANES), lambda i, p: (i, 0, p)),
        ],
        out_specs=pl.BlockSpec((1, t_out, LANES), lambda i, p: (i, 0, p)),
        scratch_shapes=[pltpu.VMEM((T_ALL, LANES), F32), pltpu.VMEM((N_ALL_BLK, 2 * BLK, LANES), F32),
                        pltpu.VMEM((N_ALL_BLK, 2 * BLK, LANES), BF16), pltpu.VMEM((6, BLK, BLK), F32)],
        compiler_params=_params(("parallel", "parallel")),
        name="retention",
    )(dec, gn, qkv, kt, qkv, gate)


def _d_kernel(q_ref, k_ref, v_ref, bias_ref, o_ref, *, ctx_queries):
    rows = SEQ // GRID_W
    n_grp = rows // D_GROUP
    gq = D_GROUP * GRID_W
    slab_rows = NA_ROWS + D_GROUP
    kx = k_ref[0, SEQ:T_ALL, :]
    vx = v_ref[0, SEQ:T_ALL, :]
    head0 = lax.broadcasted_iota(jnp.int32, (1, LANES), 1) < HEAD_DIM

    def attend(q, parts):
        n = q.shape[0]
        zq = jnp.zeros_like(q)
        qq = jnp.concatenate([jnp.where(head0, q, zq), jnp.where(head0, zq, q)], axis=0)
        scores = []
        for k, _, bias in parts:
            s = lax.dot_general(qq, k, _NT, preferred_element_type=F32)
            scores.append(s if bias is None else s + bias)
        m = functools.reduce(jnp.maximum, [jnp.max(s, axis=1, keepdims=True) for s in scores])
        es = [jnp.exp(s - m) for s in scores]
        den = sum(jnp.sum(e, axis=1, keepdims=True) for e in es)
        o = sum(jnp.dot(e.astype(BF16), v, preferred_element_type=F32)
                for e, (_, v, _) in zip(es, parts)) / den
        return jnp.where(head0, o[:n], o[n:]).astype(BF16)

    def group(g, carry):
        first_key_row = jnp.clip(g * D_GROUP - NA_ROWS // 2, 0, rows - slab_rows)
        start = pl.multiple_of(first_key_row * GRID_W, GRID_W)
        kind = jnp.where(g == 0, 0, jnp.where(g == n_grp - 1, 2, 1))
        q0 = pl.multiple_of(g * gq, gq)
        q = q_ref[0, pl.ds(q0, gq), :]
        ks = k_ref[0, pl.ds(start, slab_rows * GRID_W), :]
        vs = v_ref[0, pl.ds(start, slab_rows * GRID_W), :]
        o_ref[0, pl.ds(q0, gq), :] = attend(q, [(ks, vs, bias_ref[0, kind]), (kx, vx, None)])
        return carry

    lax.fori_loop(0, n_grp, group, 0, unroll=8)
    if ctx_queries:
        o_ref[0, SEQ:T_ALL, :] = attend(q_ref[0, SEQ:T_ALL, :], [(kx, vx, None)])


def _attn_d(qkv, bias, layer, ctx_queries):
    b = qkv.shape[0]
    t_out = T_ALL if ctx_queries else SEQ
    col = lambda g: pl.BlockSpec((1, T_ALL, LANES), lambda i, p: (i, 0, g + p))
    return pl.pallas_call(
        functools.partial(_d_kernel, ctx_queries=ctx_queries),
        out_shape=jax.ShapeDtypeStruct((b, t_out, 2 * LANES), BF16),
        grid=(b, 2),
        in_specs=[
            col(G_DQ), col(G_DK), col(G_DV),
            pl.BlockSpec((1,) + bias.shape[1:], lambda i, p: (2 * layer + p, 0, 0, 0)),
        ],
        out_specs=pl.BlockSpec((1, t_out, LANES), lambda i, p: (i, 0, p)),
        compiler_params=_params(("parallel", "parallel")),
        name="attn_nbr",
    )(qkv, qkv, qkv, bias)


def _outmlp_kernel(*refs, split_streams):
    (x,), refs = _stream_tiles(refs, split_streams, 1)
    a_ref, b_ref, c_ref, d_ref, mod_ref, g2_ref, wo_ref, w1_ref, w2_ref, o_ref = refs
    gw = 2 * LANES
    mix = None
    for n, r in enumerate((a_ref, b_ref, c_ref, d_ref)):
        part = jnp.dot(r[0], wo_ref[n * gw:(n + 1) * gw, :], preferred_element_type=F32)
        mix = part if mix is None else mix + part
    x1 = x + mod_ref[0, 2:3, :] * mix
    h2 = (_rms(x1) * g2_ref[...] * (1.0 + mod_ref[0, 4:5, :]) + mod_ref[0, 3:4, :]).astype(BF16)
    acc = None
    for n in range(D_FF // D_MODEL):
        cols = slice(n * D_MODEL, (n + 1) * D_MODEL)
        hid = jnp.maximum(jnp.dot(h2, w1_ref[:, cols], preferred_element_type=F32), 0.0)
        part = jnp.dot((hid * hid).astype(BF16), w2_ref[cols, :], preferred_element_type=F32)
        acc = part if acc is None else acc + part
    o_ref[0] = x1 + mod_ref[0, 5:6, :] * acc


def _out_mlp(streams, outs, mod, g2, w_out, w1, w2, ctx_tokens, tm):
    b = streams[0].shape[0]
    split = len(streams) == 2
    assert ctx_tokens or not split
    t_out = T_ALL if ctx_tokens else SEQ
    nt = t_out // tm
    ctx_row = mod.shape[0] - 1
    ctx_tile = SEQ // tm

    def mod_map(i, t):
        return (jnp.where(t >= ctx_tile, ctx_row, i), 0, 0)

    tile = lambda w: pl.BlockSpec((1, tm, w), lambda i, t: (i, t, 0))
    whole = lambda s: pl.BlockSpec(s, lambda i, t: (0, 0), pipeline_mode=pl.Buffered(1))
    return pl.pallas_call(
        functools.partial(_outmlp_kernel, split_streams=split),
        out_shape=jax.ShapeDtypeStruct((b, t_out, D_MODEL), F32),
        grid=(b, nt),
        in_specs=_stream_specs(split, tm, lambda i, t: (i, t)) + [
            tile(2 * LANES), tile(2 * LANES), tile(2 * LANES), tile(2 * LANES),
            pl.BlockSpec((1, 6, D_MODEL), mod_map),
            whole((1, D_MODEL)), whole((D_MODEL, D_MODEL)), whole((D_MODEL, D_FF)), whole((D_FF, D_MODEL)),
        ],
        out_specs=tile(D_MODEL),
        compiler_params=_params(("parallel", "parallel")),
        name="out_mlp",
    )(*streams, *outs, mod, g2, w_out, w1, w2)


def _rope_tables():
    pos = np.arange(SEQ)
    rows, cols = (pos // GRID_W).astype(np.float32), (pos % GRID_W).astype(np.float32)

    def pattern(half):
        freqs = (np.float32(ROPE_BASE) ** (-np.arange(half, dtype=np.float32) / np.float32(half))).astype(np.float32)
        ang = [(p[:, None] * freqs[None, :]).astype(np.float32) for p in (rows, cols)]
        cos = np.concatenate([np.cos(a) for a in (ang[0], ang[0], ang[1], ang[1])], axis=1)
        sin = np.concatenate([-np.sin(ang[0]), np.sin(ang[0]), -np.sin(ang[1]), np.sin(ang[1])], axis=1)
        reps = LANES // (4 * half)
        cos = np.concatenate([np.tile(cos, (1, reps)), np.ones((CTX_LEN, LANES))], axis=0)
        sin = np.concatenate([np.tile(sin, (1, reps)), np.zeros((CTX_LEN, LANES))], axis=0)
        return jnp.asarray(cos, F32), jnp.asarray(sin, F32)

    ca, sa = pattern(HEAD_DIM // 4)
    cb, sb = pattern(B_QK_DIM // 4)
    return ca, sa, cb, sb


def _gain_row(aq, ak, bq, bk, dq, dk):
    one = jnp.ones((LANES,), F32)
    t = lambda g, s=1.0: jnp.tile(g.astype(F32), LANES // g.shape[0]) * s
    groups = [one] * N_GROUPS
    groups[G_AQ] = groups[G_AQ + 1] = t(aq, HEAD_DIM ** -0.5 * LOG2_E)
    groups[G_AK] = t(ak)
    groups[G_BQ] = groups[G_BQ + 1] = t(bq, B_QK_DIM ** -0.5 * LOG2_E)
    groups[G_BK] = groups[G_BK + 1] = t(bk)
    groups[G_CK] = groups[G_CK + 1] = one * HEAD_DIM ** -0.5
    groups[G_DQ] = groups[G_DQ + 1] = t(dq, HEAD_DIM ** -0.5)
    groups[G_DK] = groups[G_DK + 1] = t(dk)
    return jnp.concatenate(groups)[None, :]


def _nbr_bias(rpb):
    assert D_GROUP == NA_ROWS // 2
    qc = np.arange(GRID_W)[:, None]
    kc = np.arange(GRID_W)[None, :]
    cstart = np.clip(qc - NA_COLS // 2, 0, GRID_W - NA_COLS)
    valid = (kc >= cstart) & (kc < cstart + NA_COLS)
    pad = GRID_W - NA_COLS
    rp = jnp.pad(rpb.astype(F32), ((0, 0), (0, 0), (pad, pad)))
    period = rp.shape[-1]
    flat = jnp.tile(rp, (1, 1, GRID_W + 1))[..., :GRID_W * (period + 1)]
    shifted = flat.reshape(rp.shape[0], rp.shape[1], GRID_W, period + 1)
    toep = shifted[:, :, ::-1, :GRID_W]
    toep = jnp.where(valid[None, None], toep, NEG_INF).transpose(0, 2, 1, 3)
    h = rpb.shape[0]
    half = NA_ROWS // 2

    def rows_of(dr0, offset):
        blk = toep[:, :, dr0:dr0 + NA_ROWS]
        blk = jnp.pad(blk, ((0, 0), (0, 0), (offset, D_GROUP - offset), (0, 0)), constant_values=NEG_INF)
        return blk.reshape(h, GRID_W, (NA_ROWS + D_GROUP) * GRID_W)

    kinds = [[rows_of(NA_ROWS - 1 - a, 0) for a in range(D_GROUP)],
             [rows_of(half - 1, a) for a in range(D_GROUP)],
             [rows_of(half - 1 - a, D_GROUP) for a in range(D_GROUP)]]
    b = jnp.stack([jnp.stack(k, axis=1) for k in kinds], axis=1)
    b = b.reshape(h // 2, 2, 3, D_GROUP * GRID_W, -1).transpose(0, 2, 1, 3, 4)
    return b.reshape(h // 2, 3, 2 * D_GROUP * GRID_W, -1)


def kernel(x, c, ctx, c_ctx, w_mod, b_mod, norm1_g, norm2_g, w_in, w_out, a_qnorm_g, a_knorm_g, a_sink, b_qnorm_g, b_knorm_g, b_lambda_q1, b_lambda_k1, b_lambda_q2, b_lambda_k2, b_subln_g, c_decay_fwd, c_decay_bwd, c_gn_g, d_qnorm_g, d_knorm_g, d_rpb, w_mlp1, w_mlp2):
    depth = w_mod.shape[0]
    b = x.shape[0]
    n_mod = b + IN_PROJ_BATCH
    assert n_mod <= 8
    cc = jnp.concatenate([c] + [c_ctx[None, :]] * IN_PROJ_BATCH + [jnp.zeros((8 - n_mod, D_MODEL), F32)], axis=0)
    mods = _modulation(cc, w_mod, b_mod)[:, :n_mod].reshape(depth, n_mod, 6, D_MODEL)
    tabs = _rope_tables()
    nbr_bias = _nbr_bias(d_rpb.reshape((-1,) + d_rpb.shape[2:]))
    streams = (x, ctx)

    for layer in range(depth):
        need_ctx = layer < depth - 1
        lambda_init = 0.8 - 0.6 * math.exp(-0.3 * layer)
        gain = _gain_row(a_qnorm_g[layer], a_knorm_g[layer], b_qnorm_g[layer], b_knorm_g[layer],
                         d_qnorm_g[layer], d_knorm_g[layer])
        qkv, gate, akv, kt_b, kt_c = _in_proj(streams, mods[layer], norm1_g[layer][None, :],
                                              w_in[layer].astype(BF16), gain, tabs)

        out_a = _attn_a(qkv, akv, a_sink[layer].astype(F32), need_ctx)
        lam_vecs = jnp.stack([b_lambda_q1[layer], b_lambda_k1[layer],
                              b_lambda_q2[layer], b_lambda_k2[layer]]).astype(F32)
        out_b = _attn_b(qkv, kt_b, lam_vecs, b_subln_g[layer][None, :], need_ctx, lambda_init)
        dec = jnp.stack([c_decay_fwd[layer].reshape(2, 2), c_decay_bwd[layer].reshape(2, 2)],
                        axis=1).reshape(2, 1, 4).astype(F32)
        out_c = _retention(qkv, kt_c, gate, dec, jnp.tile(c_gn_g[layer], 2)[None, :], need_ctx)
        out_d = _attn_d(qkv, nbr_bias, layer, need_ctx)

        streams = (_out_mlp(streams, (out_a, out_b, out_c, out_d), mods[layer], norm2_g[layer][None, :],
                            w_out[layer].astype(BF16), w_mlp1[layer].astype(BF16), w_mlp2[layer].astype(BF16),
                            need_ctx, tm=CTX_LEN if need_ctx else 2 * CTX_LEN),)
    return streams[0]
```
